```python
import math
import jax, jax.numpy as jnp
from jax import lax
import numpy as np

D_MODEL = 2048
BATCH = 4
SEQ = 2048
DEPTH = 4

N_BRANCHES = 3
POOL_WINDOWS = (2, 4, 8, 16)
N_POOL_GROUPS = 4
POOL_DIM = D_MODEL // 2
POOL_GROUP_DIM = POOL_DIM // N_POOL_GROUPS
CONV_DIM = D_MODEL // 2
CONV_WIDTH = 3
N_HEADS = 8
HEAD_DIM = 128
N_KV_HEADS = 2
GROUP = N_HEADS // N_KV_HEADS
ATTN_DIM = N_HEADS * HEAD_DIM
IDX_HEADS = 16
IDX_DIM = 64
TOPK_MAX = 256
Q_BLOCK = 128
NUM_BUCKETS = 32
MAX_EXACT = NUM_BUCKETS // 2
MAX_DISTANCE = 128
BRANCH_DIM = D_MODEL // 2
MLP_HIDDEN = 4 * D_MODEL
NORM_EPS = 1e-6
IN_SPLITS = (POOL_DIM, CONV_DIM, CONV_DIM, CONV_DIM, ATTN_DIM, N_KV_HEADS * HEAD_DIM,
             N_KV_HEADS * HEAD_DIM, IDX_HEADS * IDX_DIM, IDX_DIM, IDX_HEADS)
IN_WIDTH = sum(IN_SPLITS)

kernel_name = 'hybrid_gated_pool_conv_dsa_block'


def _split_points():
    pts, acc = [], 0
    for w in IN_SPLITS[:-1]:
        acc += w
        pts.append(acc)
    return pts


def rmsnorm(x, g):
    xf = x.astype(jnp.float32)
    y = xf * lax.rsqrt(jnp.mean(jnp.square(xf), axis=-1, keepdims=True) + NORM_EPS)
    return (y * g.astype(jnp.float32)).astype(x.dtype)


def t5_bucket(dist):
    n = jnp.maximum(dist, 0)
    nf = jnp.maximum(n, 1).astype(jnp.float32)
    large = MAX_EXACT + (jnp.log(nf / MAX_EXACT) / math.log(MAX_DISTANCE / MAX_EXACT)
                         * (NUM_BUCKETS - MAX_EXACT)).astype(jnp.int32)
    large = jnp.minimum(large, NUM_BUCKETS - 1)
    return jnp.where(n < MAX_EXACT, n, large)


def pool_mixer(u, pool_w, pool_scale):
    B, S, _ = u.shape
    ug = u.astype(jnp.float32).reshape(B, S, N_POOL_GROUPS, POOL_GROUP_DIM)
    c0 = jnp.concatenate([jnp.zeros((B, 1, N_POOL_GROUPS, POOL_GROUP_DIM), jnp.float32),
                          jnp.cumsum(ug, axis=1)], axis=1)
    t = jnp.arange(S)
    win = jnp.array(POOL_WINDOWS, jnp.int32)
    lo = jnp.maximum(t[:, None] + 1 - win[None, :], 0)
    g_ar = jnp.arange(N_POOL_GROUPS)[None, :]
    window_sum = c0[:, 1:] - c0[:, lo, g_ar]
    cnt = (t[:, None] + 1 - lo).astype(jnp.float32)[None, :, :, None]
    d = window_sum / cnt - ug
    y = jnp.einsum('bsgc,gcd->bsgd', d, pool_w.astype(jnp.float32))
    return (y.reshape(B, S, POOL_DIM) * pool_scale.astype(jnp.float32)).astype(u.dtype)


def conv_mixer(u, gate_c, gate_b, conv_w):
    z = gate_c * u
    y = lax.conv_general_dilated(z, conv_w[:, None, :], window_strides=(1,),
                                 padding=[(CONV_WIDTH - 1, 0)],
                                 dimension_numbers=('NWC', 'WIO', 'NWC'),
                                 feature_group_count=CONV_DIM)
    return gate_b * y


def sparse_attention(q, k, v, q_idx, k_idx, w_idx, rel_bias):
    B, S = q.shape[0], q.shape[1]
    topk = min(TOPK_MAX, S // 4)
    nblk = S // Q_BLOCK
    key_pos = jnp.arange(S)
    k_idx32 = k_idx.astype(jnp.float32)
    table = rel_bias.astype(jnp.float32)

    def to_blocks(a):
        return jnp.moveaxis(a.reshape((B, nblk, Q_BLOCK) + a.shape[2:]), 1, 0)

    def block_fn(args):
        qb, qib, wb, blk = args
        q_pos = blk * Q_BLOCK + jnp.arange(Q_BLOCK)
        dots = jnp.einsum('bthd,bsd->bths', qib.astype(jnp.float32), k_idx32)
        score = jnp.einsum('bth,bths->bts', wb.astype(jnp.float32) * IDX_HEADS ** -0.5,
                           jax.nn.relu(dots))
        causal = key_pos[None, :] <= q_pos[:, None]
        score = jnp.where(causal[None], score, -jnp.inf)
        _, idx = lax.top_k(score, topk)
        k_sel = jax.vmap(lambda kb, ib: kb[ib])(k, idx)
        v_sel = jax.vmap(lambda vb, ib: vb[ib])(v, idx)
        qg = qb.reshape(B, Q_BLOCK, N_KV_HEADS, GROUP, HEAD_DIM).astype(jnp.float32)
        logits = jnp.einsum('btgrd,btkgd->btgrk', qg, k_sel.astype(jnp.float32)) * HEAD_DIM ** -0.5
        dist = q_pos[None, :, None] - idx
        bias = table[t5_bucket(dist)]
        bias = bias.reshape(B, Q_BLOCK, topk, N_KV_HEADS, GROUP).transpose(0, 1, 3, 4, 2)
        logits = jnp.where((dist >= 0)[:, :, None, None, :], logits + bias, -jnp.inf)
        p = jax.nn.softmax(logits, axis=-1)
        o = jnp.einsum('btgrk,btkgd->btgrd', p, v_sel.astype(jnp.float32))
        return o.reshape(B, Q_BLOCK, ATTN_DIM).astype(q.dtype)

    out = lax.map(block_fn, (to_blocks(q), to_blocks(q_idx), to_blocks(w_idx),
                             jnp.arange(nblk, dtype=jnp.int32)))
    return jnp.moveaxis(out, 0, 1).reshape(B, S, ATTN_DIM)


def setup_inputs(seed: int = 0) -> dict:
    key = jax.random.key(seed)
    ks = jax.random.split(key, 13)

    def nrm(k, shape, scale):
        return jax.random.normal(k, shape, jnp.float32) * scale

    return {
        'x': nrm(ks[0], (BATCH, SEQ, D_MODEL), 1.0),
        'norm_gains': 1.0 + nrm(ks[1], (DEPTH, 4, D_MODEL), 0.05),
        'w_in': nrm(ks[2], (DEPTH, D_MODEL, IN_WIDTH), D_MODEL ** -0.5),
        'pool_w': nrm(ks[3], (DEPTH, N_POOL_GROUPS, POOL_GROUP_DIM, POOL_GROUP_DIM), POOL_GROUP_DIM ** -0.5),
        'pool_scale': 1.0 + nrm(ks[4], (DEPTH, POOL_DIM), 0.05),
        'conv_w': nrm(ks[5], (DEPTH, CONV_WIDTH, CONV_DIM), CONV_WIDTH ** -0.5),
        'rel_bias': nrm(ks[6], (NUM_BUCKETS, N_HEADS), 0.5),
        'w_branch': nrm(ks[7], (DEPTH, N_BRANCHES, BRANCH_DIM, D_MODEL), BRANCH_DIM ** -0.5),
        'w_gate': nrm(ks[8], (DEPTH, D_MODEL, N_BRANCHES * D_MODEL), D_MODEL ** -0.5),
        'b_gate': nrm(ks[9], (DEPTH, N_BRANCHES, D_MODEL), 0.1),
        'w_out': nrm(ks[10], (DEPTH, D_MODEL, D_MODEL), D_MODEL ** -0.5),
        'w_up': nrm(ks[11], (DEPTH, D_MODEL, MLP_HIDDEN), D_MODEL ** -0.5),
        'w_down': nrm(ks[12], (DEPTH, MLP_HIDDEN, D_MODEL), MLP_HIDDEN ** -0.5),
    }


def reference(x, norm_gains, w_in, pool_w, pool_scale, conv_w, rel_bias, w_branch,
              w_gate, b_gate, w_out, w_up, w_down):
    B, S, D = x.shape
    pts = _split_points()
    for l in range(DEPTH):
        g = norm_gains[l]
        h = rmsnorm(x, g[0])
        proj = h @ w_in[l]
        (u_pool, u_conv, c_gate, b_gate_conv, q, k, v, qi, ki, wi) = jnp.split(proj, pts, axis=-1)
        y_pool = pool_mixer(u_pool, pool_w[l], pool_scale[l])
        y_conv = conv_mixer(u_conv, c_gate, b_gate_conv, conv_w[l])
        y_attn = sparse_attention(q.reshape(B, S, N_HEADS, HEAD_DIM),
                                  k.reshape(B, S, N_KV_HEADS, HEAD_DIM),
                                  v.reshape(B, S, N_KV_HEADS, HEAD_DIM),
                                  qi.reshape(B, S, IDX_HEADS, IDX_DIM), ki, wi, rel_bias)
        branches = jnp.stack([y_pool, y_conv, y_attn], axis=2)
        up = jnp.einsum('bsnc,ncd->bsnd', branches, w_branch[l])
        gates = jax.nn.sigmoid((h @ w_gate[l]).reshape(B, S, N_BRANCHES, D) + b_gate[l])
        mixed = jnp.sum(gates * up, axis=2) @ w_out[l]
        x = x + rmsnorm(mixed, g[1])
        h = rmsnorm(x, g[2])
        m = jnp.square(jax.nn.relu(h @ w_up[l])) @ w_down[l]
        x = x + rmsnorm(m, g[3])
    return x
```

```python
import functools
import math

import jax
import jax.numpy as jnp
import numpy as np
from jax import lax
from jax.experimental import pallas as pl
from jax.experimental.pallas import tpu as pltpu

F32 = jnp.float32
BF16 = jnp.bfloat16
I32 = jnp.int32

NORM_EPS = 1e-6
POOL_WINDOWS = (2, 4, 8, 16)
POOL_GROUP_DIM = 256
POOL_HALO = 16
CONV_WIDTH = 3
N_HEADS = 8
N_KV_HEADS = 2
GROUP = N_HEADS // N_KV_HEADS
HEAD_DIM = 128
IDX_HEADS = 16
IDX_DIM = 64
TOPK_MAX = 256
NUM_BUCKETS = 32
MAX_EXACT = NUM_BUCKETS // 2
MAX_DISTANCE = 128
ATT_BLOCK = 256
IDX_PAD = 128
INT_MIN = -(2 ** 31)
INT_MAX = 2 ** 31 - 1
NEG = -1e30
VMEM_LIMIT = 56 * 1024 * 1024


def _bucket_thresholds():
    n = np.arange(MAX_EXACT, 8 * MAX_DISTANCE, dtype=np.int64)
    nf = n.astype(np.float32)
    large = MAX_EXACT + (np.log(nf / np.float32(MAX_EXACT)) / np.float32(math.log(MAX_DISTANCE / MAX_EXACT))
                         * np.float32(NUM_BUCKETS - MAX_EXACT)).astype(np.int32)
    large = np.minimum(large, NUM_BUCKETS - 1)
    thr = []
    for b in range(MAX_EXACT + 1, NUM_BUCKETS):
        thr.append(int(n[np.argmax(large >= b)]))
    return tuple(thr)


BUCKET_THRESHOLDS = _bucket_thresholds()


def _params(*sem):
    return pltpu.CompilerParams(dimension_semantics=sem, vmem_limit_bytes=VMEM_LIMIT)


def _rms(x, gain):
    ms = jnp.mean(x * x, axis=-1, keepdims=True)
    return x * lax.rsqrt(ms + NORM_EPS) * gain


def _dot_nt(a, b):
    return lax.dot_general(a, b, (((1,), (1,)), ((), ())), preferred_element_type=F32)


def _norm_matmul_body(x_ref, g_ref, w_ref, post_ref, o_ref, h_ref, *, act):
    @pl.when(pl.program_id(1) == 0)
    def _():
        h_ref[...] = _rms(x_ref[...], g_ref[...]).astype(BF16)

    y = jnp.dot(h_ref[...], w_ref[...], preferred_element_type=F32)
    y = y * post_ref[0:1, :] + post_ref[1:2, :]
    if act == "sigmoid":
        y = 1.0 / (1.0 + jnp.exp(-y))
    o_ref[...] = y.astype(o_ref.dtype)


def _norm_matmul(x, gain, w, layer, post, *, tm, tn, out_dtype, act=None):
    tok, d = x.shape
    n = w.shape[-1]
    return pl.pallas_call(
        functools.partial(_norm_matmul_body, act=act),
        grid=(tok // tm, n // tn),
        in_specs=[
            pl.BlockSpec((tm, d), lambda i, j: (i, 0)),
            pl.BlockSpec((1, d), lambda i, j: (0, 0)),
            pl.BlockSpec((None, d, tn), lambda i, j: (layer, 0, j)),
            pl.BlockSpec((2, tn), lambda i, j: (0, j)),
        ],
        out_specs=pl.BlockSpec((tm, tn), lambda i, j: (i, j)),
        out_shape=jax.ShapeDtypeStruct((tok, n), out_dtype),
        scratch_shapes=[pltpu.VMEM((tm, d), BF16)],
        compiler_params=_params("parallel", "arbitrary"),
    )(x, gain, w, post)


def _bias_tiles_body(rel_ref, o_ref, *, t):
    row = lax.broadcasted_iota(I32, (t, t), 0)
    col = lax.broadcasted_iota(I32, (t, t), 1)
    for off in range(2):
        n = jnp.maximum(row - col + off * t, 0)
        large = jnp.full((t, t), MAX_EXACT, I32)
        for thr in BUCKET_THRESHOLDS:
            large = large + jnp.where(n >= thr, 1, 0)
        bucket = jnp.where(n < MAX_EXACT, n, large)
        for h in range(N_HEADS):
            val = jnp.zeros((t, t), F32)
            for b in range(NUM_BUCKETS):
                val = jnp.where(bucket == b, rel_ref[b, h], val)
            o_ref[h, off] = val


def _bias_tiles(rel_bias, t):
    return pl.pallas_call(
        functools.partial(_bias_tiles_body, t=t),
        in_specs=[pl.BlockSpec(memory_space=pltpu.SMEM)],
        out_specs=pl.BlockSpec(memory_space=pltpu.VMEM),
        out_shape=jax.ShapeDtypeStruct((N_HEADS, 2, t, t), F32),
        compiler_params=pltpu.CompilerParams(vmem_limit_bytes=VMEM_LIMIT),
    )(rel_bias)


def _attn_body(rel_ref, q_ref, k_ref, v_ref, qi_ref, kiw_ref, bias_ref, o_ref,
               sc_ref, thr_ref, m_ref, l_ref, acc_ref, *, t, topk):
    i = pl.program_id(1)
    q0 = i * t
    row = lax.broadcasted_iota(I32, (t, 1), 0)
    col = lax.broadcasted_iota(I32, (1, t), 1)
    causal = col <= row

    qi = qi_ref[...].astype(BF16)
    w = kiw_ref[pl.ds(pl.multiple_of(q0, t), t), :][:, IDX_DIM:IDX_DIM + IDX_HEADS] * (IDX_HEADS ** -0.5)

    def score_block(kb, diag):
        kblk = kiw_ref[pl.ds(pl.multiple_of(kb * t, t), t), :][:, 0:IDX_DIM].astype(BF16)
        acc = jnp.zeros((t, t), F32)
        for h in range(IDX_HEADS):
            d = _dot_nt(qi[:, h * IDX_DIM:(h + 1) * IDX_DIM], kblk)
            acc = acc + w[:, h:h + 1] * jnp.maximum(d, 0.0)
        if diag:
            acc = jnp.where(causal, acc, -jnp.inf)
        bits = pltpu.bitcast(acc, I32)
        sc_ref[kb] = bits ^ ((bits >> 31) & INT_MAX)

    def score_loop(kb, carry):
        score_block(kb, False)
        return carry

    lax.fori_loop(0, i, score_loop, 0)
    score_block(i, True)

    @pl.when(q0 + t > topk)
    def _():
        def bisect(_, carry):
            lo, hi = carry
            mid = (lo & hi) + ((lo ^ hi) >> 1)

            def count(kb, c):
                return c + jnp.sum(jnp.where(sc_ref[kb] >= mid, 1.0, 0.0), axis=-1, keepdims=True)

            c = lax.fori_loop(0, i + 1, count, jnp.zeros((t, 1), F32))
            ge = c >= topk
            return jnp.where(ge, mid, lo), jnp.where(ge, hi, mid)

        lo, _ = lax.fori_loop(0, 32, bisect,
                              (jnp.full((t, 1), INT_MIN, I32), jnp.full((t, 1), INT_MAX, I32)))
        thr_ref[...] = jnp.where(q0 + row >= topk, lo, INT_MIN)

    @pl.when(q0 + t <= topk)
    def _():
        thr_ref[...] = jnp.full((t, 1), INT_MIN, I32)

    m_ref[...] = jnp.full(m_ref.shape, NEG, F32)
    l_ref[...] = jnp.zeros(l_ref.shape, F32)
    acc_ref[...] = jnp.zeros(acc_ref.shape, F32)
    thr = thr_ref[...]

    def attn_block(kb, mode):
        sel = sc_ref[kb] >= thr
        ks = pl.ds(pl.multiple_of(kb * t, t), t)
        for h in range(N_HEADS):
            g = h // GROUP
            s = _dot_nt(q_ref[:, h * HEAD_DIM:(h + 1) * HEAD_DIM], k_ref[ks, g * HEAD_DIM:(g + 1) * HEAD_DIM])
            if mode == "far":
                s = s + rel_ref[NUM_BUCKETS - 1, h]
            else:
                s = s + bias_ref[h, mode]
            s = jnp.where(sel, s, NEG)
            if mode == 0:
                s = jnp.where(causal, s, NEG)
            m_old = m_ref[h][:, :1]
            m_new = jnp.maximum(m_old, jnp.max(s, axis=-1, keepdims=True))
            alpha = jnp.exp(m_old - m_new)
            p = jnp.exp(s - m_new)
            l_new = alpha * l_ref[h][:, :1] + jnp.sum(p, axis=-1, keepdims=True)
            pv = jnp.dot(p.astype(BF16), v_ref[ks, g * HEAD_DIM:(g + 1) * HEAD_DIM], preferred_element_type=F32)
            acc_ref[h] = alpha * acc_ref[h] + pv
            m_ref[h] = jnp.broadcast_to(m_new, (t, HEAD_DIM))
            l_ref[h] = jnp.broadcast_to(l_new, (t, HEAD_DIM))

    def far_loop(kb, carry):
        attn_block(kb, "far")
        return carry

    lax.fori_loop(0, jnp.maximum(i - 1, 0), far_loop, 0)

    @pl.when(i >= 1)
    def _():
        attn_block(i - 1, 1)

    attn_block(i, 0)

    for h in range(N_HEADS):
        o_ref[:, h * HEAD_DIM:(h + 1) * HEAD_DIM] = (acc_ref[h] / l_ref[h][:, :1]).astype(o_ref.dtype)


def _attention(rel_bias, qkv, idx, bias_tiles, batch, seq):
    t = ATT_BLOCK
    nq = seq // t
    topk = min(TOPK_MAX, seq // 4)
    attn_dim = N_HEADS * HEAD_DIM
    kv_dim = N_KV_HEADS * HEAD_DIM
    qi_dim = IDX_HEADS * IDX_DIM
    return pl.pallas_call(
        functools.partial(_attn_body, t=t, topk=topk),
        grid=(batch, nq),
        in_specs=[
            pl.BlockSpec(memory_space=pltpu.SMEM),
            pl.BlockSpec((t, attn_dim), lambda b, i: (b * nq + i, 0)),
            pl.BlockSpec((seq, kv_dim), lambda b, i: (b, attn_dim // kv_dim)),
            pl.BlockSpec((seq, kv_dim), lambda b, i: (b, attn_dim // kv_dim + 1)),
            pl.BlockSpec((t, qi_dim), lambda b, i: (b * nq + i, 0)),
            pl.BlockSpec((seq, IDX_PAD), lambda b, i: (b, qi_dim // IDX_PAD)),
            pl.BlockSpec((N_HEADS, 2, t, t), lambda b, i: (0, 0, 0, 0)),
        ],
        out_specs=pl.BlockSpec((t, attn_dim), lambda b, i: (b * nq + i, 0)),
        out_shape=jax.ShapeDtypeStruct((batch * seq, attn_dim), BF16),
        scratch_shapes=[
            pltpu.VMEM((nq, t, t), I32),
            pltpu.VMEM((t, 1), I32),
            pltpu.VMEM((N_HEADS, t, HEAD_DIM), F32),
            pltpu.VMEM((N_HEADS, t, HEAD_DIM), F32),
            pltpu.VMEM((N_HEADS, t, HEAD_DIM), F32),
        ],
        compiler_params=_params("parallel", "arbitrary"),
    )(rel_bias, qkv, qkv, qkv, idx, idx, bias_tiles)


def _mixer_body(main_ref, halo_ref, ya_ref, g0_ref, g1_ref, g2_ref, wb_ref, pw_ref, ps_ref, cw_ref,
                o_ref, y_ref, *, tm, seq, pool_dim, conv_dim):
    i = pl.program_id(0)

    @pl.when(pl.program_id(1) == 0)
    def _():
        pos0 = (i * tm) % seq
        pos = lax.broadcasted_iota(I32, (tm, 1), 0) + pos0
        keep = jnp.where(pos0 == 0, 0.0, 1.0)

        def ext(c0, c1):
            halo = halo_ref[:, c0:c1].astype(F32) * keep
            return jnp.concatenate([halo, main_ref[:, c0:c1].astype(F32)], axis=0)

        for g, win in enumerate(POOL_WINDOWS):
            c0 = g * POOL_GROUP_DIM
            u = ext(c0, c0 + POOL_GROUP_DIM)
            s, sh = u, 1
            while sh < win:
                s = s + pltpu.roll(s, sh, axis=0)
                sh *= 2
            cnt = jnp.minimum(pos + 1, win).astype(F32)
            dlt = s[POOL_HALO:] / cnt - u[POOL_HALO:]
            yp = jnp.dot(dlt.astype(BF16), pw_ref[g], preferred_element_type=F32)
            y_ref[0, :, c0:c0 + POOL_GROUP_DIM] = (yp * ps_ref[:, c0:c0 + POOL_GROUP_DIM]).astype(BF16)

        z = ext(pool_dim, pool_dim + conv_dim) * ext(pool_dim + conv_dim, pool_dim + 2 * conv_dim)
        yc = z[POOL_HALO:] * cw_ref[CONV_WIDTH - 1:CONV_WIDTH, :]
        for tap in range(1, CONV_WIDTH):
            yc = yc + pltpu.roll(z, tap, axis=0)[POOL_HALO:] * cw_ref[CONV_WIDTH - 1 - tap:CONV_WIDTH - tap, :]
        gate_b = main_ref[:, pool_dim + 2 * conv_dim:pool_dim + 3 * conv_dim].astype(F32)
        y_ref[1] = (gate_b * yc).astype(BF16)

    up0 = jnp.dot(y_ref[0], wb_ref[0], preferred_element_type=F32)
    up1 = jnp.dot(y_ref[1], wb_ref[1], preferred_element_type=F32)
    up2 = jnp.dot(ya_ref[...], wb_ref[2], preferred_element_type=F32)
    mixed = g0_ref[...].astype(F32) * up0 + g1_ref[...].astype(F32) * up1 + g2_ref[...].astype(F32) * up2
    o_ref[...] = mixed.astype(o_ref.dtype)


def _mixer(main, y_attn, gates, w_branch, pool_w, pool_scale, conv_w, layer, seq, *, tm, tn):
    tok, width = main.shape
    branch_dim = y_attn.shape[-1]
    d = w_branch.shape[-1]
    nj = d // tn
    halo_blocks = tm // POOL_HALO
    return pl.pallas_call(
        functools.partial(_mixer_body, tm=tm, seq=seq, pool_dim=branch_dim, conv_dim=branch_dim),
        grid=(tok // tm, nj),
        in_specs=[
            pl.BlockSpec((tm, width), lambda i, j: (i, 0)),
            pl.BlockSpec((POOL_HALO, width), lambda i, j: (jnp.maximum(i * halo_blocks - 1, 0), 0)),
            pl.BlockSpec((tm, branch_dim), lambda i, j: (i, 0)),
            pl.BlockSpec((tm, tn), lambda i, j: (i, j)),
            pl.BlockSpec((tm, tn), lambda i, j: (i, nj + j)),
            pl.BlockSpec((tm, tn), lambda i, j: (i, 2 * nj + j)),
            pl.BlockSpec((None, 3, branch_dim, tn), lambda i, j: (layer, 0, 0, j)),
            pl.BlockSpec((None,) + pool_w.shape[1:], lambda i, j: (layer, 0, 0, 0)),
            pl.BlockSpec((None, 1, branch_dim), lambda i, j: (layer, 0, 0)),
            pl.BlockSpec((None, CONV_WIDTH, branch_dim), lambda i, j: (layer, 0, 0)),
        ],
        out_specs=pl.BlockSpec((tm, tn), lambda i, j: (i, j)),
        out_shape=jax.ShapeDtypeStruct((tok, d), BF16),
        scratch_shapes=[pltpu.VMEM((2, tm, branch_dim), BF16)],
        compiler_params=_params("parallel", "arbitrary"),
    )(main, main, y_attn, gates, gates, gates, w_branch, pool_w, pool_scale, conv_w)


def _proj_norm_res_body(a_ref, w_ref, x_ref, g_ref, o_ref):
    m = jnp.dot(a_ref[...], w_ref[...], preferred_element_type=F32)
    o_ref[...] = x_ref[...] + _rms(m, g_ref[...])


def _proj_norm_res(a, w, layer, x, gain, *, tm):
    tok, d = x.shape
    k = a.shape[-1]
    return pl.pallas_call(
        _proj_norm_res_body,
        grid=(tok // tm,),
        in_specs=[
            pl.BlockSpec((tm, k), lambda i: (i, 0)),
            pl.BlockSpec((None, k, d), lambda i: (layer, 0, 0)),
            pl.BlockSpec((tm, d), lambda i: (i, 0)),
            pl.BlockSpec((1, d), lambda i: (0, 0)),
        ],
        out_specs=pl.BlockSpec((tm, d), lambda i: (i, 0)),
        out_shape=jax.ShapeDtypeStruct((tok, d), F32),
        compiler_params=_params("parallel"),
    )(a, w, x, gain)


def _mlp_body(x_ref, gpre_ref, gpost_ref, wu_ref, wd_ref, o_ref, h_ref, acc_ref):
    j = pl.program_id(1)

    @pl.when(j == 0)
    def _():
        h_ref[...] = _rms(x_ref[...], gpre_ref[...]).astype(BF16)
        acc_ref[...] = jnp.zeros(acc_ref.shape, F32)

    hid = jnp.dot(h_ref[...], wu_ref[...], preferred_element_type=F32)
    hid = jnp.square(jnp.maximum(hid, 0.0)).astype(BF16)
    acc_ref[...] += jnp.dot(hid, wd_ref[...], preferred_element_type=F32)

    @pl.when(j == pl.num_programs(1) - 1)
    def _():
        o_ref[...] = x_ref[...] + _rms(acc_ref[...], gpost_ref[...])


def _mlp(x, gpre, gpost, w_up, w_down, layer, *, tm, th):
    tok, d = x.shape
    hidden = w_up.shape[-1]
    return pl.pallas_call(
        _mlp_body,
        grid=(tok // tm, hidden // th),
        in_specs=[
            pl.BlockSpec((tm, d), lambda i, j: (i, 0)),
            pl.BlockSpec((1, d), lambda i, j: (0, 0)),
            pl.BlockSpec((1, d), lambda i, j: (0, 0)),
            pl.BlockSpec((None, d, th), lambda i, j: (layer, 0, j)),
            pl.BlockSpec((None, th, d), lambda i, j: (layer, j, 0)),
        ],
        out_specs=pl.BlockSpec((tm, d), lambda i, j: (i, 0)),
        out_shape=jax.ShapeDtypeStruct((tok, d), F32),
        scratch_shapes=[pltpu.VMEM((tm, d), BF16), pltpu.VMEM((tm, d), F32)],
        compiler_params=_params("parallel", "arbitrary"),
    )(x, gpre, gpost, w_up, w_down)


def _tile(n, pref):
    t = min(n, pref)
    assert n % t == 0, (n, pref)
    return t


@jax.jit
def _forward(x, norm_gains, w_in, pool_w, pool_scale, conv_w, rel_bias, w_branch,
             w_gate, b_gate, w_out, w_up, w_down):
    batch, seq, d = x.shape
    depth = w_in.shape[0]
    tok = batch * seq
    branch_dim = d // 2
    attn_dim = N_HEADS * HEAD_DIM
    kv_dim = N_KV_HEADS * HEAD_DIM
    qi_dim = IDX_HEADS * IDX_DIM
    assert branch_dim == attn_dim == len(POOL_WINDOWS) * POOL_GROUP_DIM
    assert seq % ATT_BLOCK == 0 and ATT_BLOCK >= MAX_DISTANCE

    main_w = 4 * branch_dim
    qkv_w = attn_dim + 2 * kv_dim
    idx_w = qi_dim + IDX_DIM + IDX_HEADS
    w_main = w_in[:, :, :main_w].astype(BF16)
    w_qkv = w_in[:, :, main_w:main_w + qkv_w].astype(BF16)
    w_idx = jnp.pad(w_in[:, :, main_w + qkv_w:], ((0, 0), (0, 0), (0, qi_dim + IDX_PAD - idx_w))).astype(BF16)
    w_gate_b = w_gate.astype(BF16)
    w_branch_b = w_branch.astype(BF16)
    w_out_b = w_out.astype(BF16)
    w_up_b = w_up.astype(BF16)
    w_down_b = w_down.astype(BF16)
    pool_w_b = pool_w.astype(BF16)

    ones = lambda n: jnp.ones((n,), F32)
    zeros = lambda n: jnp.zeros((n,), F32)
    post_main = jnp.stack([ones(main_w), zeros(main_w)])
    post_qkv = jnp.stack([jnp.concatenate([jnp.full((attn_dim,), HEAD_DIM ** -0.5, F32), ones(2 * kv_dim)]),
                          zeros(qkv_w)])
    post_idx = jnp.stack([ones(qi_dim + IDX_PAD), zeros(qi_dim + IDX_PAD)])

    bias_tiles = _bias_tiles(rel_bias, ATT_BLOCK)

    tm = _tile(tok, 1024)
    tm_s = _tile(seq, 512)
    xf = x.reshape(tok, d)
    for l in range(depth):
        g = norm_gains[l]
        g0, g1, g2, g3 = (g[n:n + 1] for n in range(4))
        main = _norm_matmul(xf, g0, w_main, l, post_main, tm=tm, tn=512, out_dtype=BF16)
        qkv = _norm_matmul(xf, g0, w_qkv, l, post_qkv, tm=tm, tn=512, out_dtype=BF16)
        idx = _norm_matmul(xf, g0, w_idx, l, post_idx, tm=tm, tn=qi_dim + IDX_PAD, out_dtype=F32)
        post_gate = jnp.stack([ones(3 * d), b_gate[l].reshape(3 * d)])
        gates = _norm_matmul(xf, g0, w_gate_b, l, post_gate, tm=tm, tn=512, out_dtype=BF16, act="sigmoid")
        y_attn = _attention(rel_bias, qkv, idx, bias_tiles, batch, seq)
        mixed = _mixer(main, y_attn, gates, w_branch_b, pool_w_b, pool_scale.reshape(depth, 1, branch_dim),
                       conv_w, l, seq, tm=tm_s, tn=512)
        xf = _proj_norm_res(mixed, w_out_b, l, xf, g1, tm=tm_s)
        xf = _mlp(xf, g2, g3, w_up_b, w_down_b, l, tm=tm_s, th=1024)
    return xf.reshape(batch, seq, d)


def kernel(x, norm_gains, w_in, pool_w, pool_scale, conv_w, rel_bias, w_branch, w_gate, b_gate, w_out, w_up, w_down):
    return _forward(x, norm_gains, w_in, pool_w, pool_scale, conv_w, rel_bias, w_branch,
                    w_gate, b_gate, w_out, w_up, w_down)
```

```python
import functools
import math

import jax
import jax.numpy as jnp
import numpy as np
from jax import lax
from jax.experimental import pallas as pl
from jax.experimental.pallas import tpu as pltpu

F32 = jnp.float32
BF16 = jnp.bfloat16
I32 = jnp.int32

NORM_EPS = 1e-6
POOL_WINDOWS = (2, 4, 8, 16)
POOL_GROUP_DIM = 256
POOL_HALO = 16
CONV_WIDTH = 3
N_HEADS = 8
N_KV_HEADS = 2
GROUP = N_HEADS // N_KV_HEADS
HEAD_DIM = 128
IDX_HEADS = 16
IDX_DIM = 64
TOPK_MAX = 256
NUM_BUCKETS = 32
MAX_EXACT = NUM_BUCKETS // 2
MAX_DISTANCE = 128
ATT_BLOCK = 256
IDX_PAD = 128
INT_MIN = -(2 ** 31)
INT_MAX = 2 ** 31 - 1
NEG = -1e30
LOG2E = math.log2(math.e)
ONES_ROWS = 16
VMEM_LIMIT = 56 * 1024 * 1024


def _bucket_thresholds():
    n = np.arange(MAX_EXACT, 8 * MAX_DISTANCE, dtype=np.int64)
    nf = n.astype(np.float32)
    large = MAX_EXACT + (np.log(nf / np.float32(MAX_EXACT)) / np.float32(math.log(MAX_DISTANCE / MAX_EXACT))
                         * np.float32(NUM_BUCKETS - MAX_EXACT)).astype(np.int32)
    large = np.minimum(large, NUM_BUCKETS - 1)
    thr = []
    for b in range(MAX_EXACT + 1, NUM_BUCKETS):
        thr.append(int(n[np.argmax(large >= b)]))
    return tuple(thr)


BUCKET_THRESHOLDS = _bucket_thresholds()


def _params(*sem):
    return pltpu.CompilerParams(dimension_semantics=sem, vmem_limit_bytes=VMEM_LIMIT)


def _rms(x, gain):
    ms = jnp.mean(x * x, axis=-1, keepdims=True)
    return x * lax.rsqrt(ms + NORM_EPS) * gain


def _dot_nt(a, b):
    return lax.dot_general(a, b, (((1,), (1,)), ((), ())), preferred_element_type=F32)


def _norm_matmul_body(x_ref, g_ref, w_ref, post_ref, o_ref, h_ref, *, act):
    @pl.when(pl.program_id(1) == 0)
    def _():
        h_ref[...] = _rms(x_ref[...], g_ref[...]).astype(BF16)

    y = jnp.dot(h_ref[...], w_ref[...], preferred_element_type=F32)
    y = y * post_ref[0:1, :] + post_ref[1:2, :]
    if act == "sigmoid":
        y = 1.0 / (1.0 + jnp.exp(-y))
    o_ref[...] = y.astype(o_ref.dtype)


def _norm_matmul(x, gain, w, layer, post, *, tm, tn, out_dtype, name, act=None):
    tok, d = x.shape
    n = w.shape[-1]
    return pl.pallas_call(
        functools.partial(_norm_matmul_body, act=act),
        grid=(tok // tm, n // tn),
        in_specs=[
            pl.BlockSpec((tm, d), lambda i, j: (i, 0)),
            pl.BlockSpec((1, d), lambda i, j: (0, 0)),
            pl.BlockSpec((None, d, tn), lambda i, j: (layer, 0, j)),
            pl.BlockSpec((2, tn), lambda i, j: (0, j)),
        ],
        out_specs=pl.BlockSpec((tm, tn), lambda i, j: (i, j)),
        out_shape=jax.ShapeDtypeStruct((tok, n), out_dtype),
        scratch_shapes=[pltpu.VMEM((tm, d), BF16)],
        compiler_params=_params("parallel", "arbitrary"),
        name=name,
    )(x, gain, w, post)


def _bias_tiles_body(rel_ref, o_ref, *, t):
    row = lax.broadcasted_iota(I32, (t, t), 0)
    col = lax.broadcasted_iota(I32, (t, t), 1)
    for off in range(2):
        n = jnp.maximum(col - row + off * t, 0)
        large = jnp.full((t, t), MAX_EXACT, I32)
        for thr in BUCKET_THRESHOLDS:
            large = large + jnp.where(n >= thr, 1, 0)
        bucket = jnp.where(n < MAX_EXACT, n, large)
        for h in range(N_HEADS):
            val = jnp.zeros((t, t), F32)
            for b in range(NUM_BUCKETS):
                val = jnp.where(bucket == b, rel_ref[b, h], val)
            o_ref[h, off] = (val - rel_ref[NUM_BUCKETS - 1, h]) * LOG2E


def _bias_tiles(rel_bias, t):
    return pl.pallas_call(
        functools.partial(_bias_tiles_body, t=t),
        in_specs=[pl.BlockSpec(memory_space=pltpu.SMEM)],
        out_specs=pl.BlockSpec(memory_space=pltpu.VMEM),
        out_shape=jax.ShapeDtypeStruct((N_HEADS, 2, t, t), F32),
        compiler_params=pltpu.CompilerParams(vmem_limit_bytes=VMEM_LIMIT),
        name="bias_tiles",
    )(rel_bias)


def _attn_body(q_ref, k_ref, v_ref, qi_ref, kiw_ref, bias_ref, o_ref,
               sc_ref, thr_ref, vt_ref, kib_ref, qih_ref, mask_ref, s_ref, p_ref, alpha_ref, m_ref, acc_ref,
               *, t, topk):
    i = pl.program_id(1)
    nkb = sc_ref.shape[0]
    q0 = pl.multiple_of(i * t, t)
    key_pos = lax.broadcasted_iota(I32, (t, 1), 0)
    qry_pos = lax.broadcasted_iota(I32, (1, t), 1)
    causal = key_pos <= qry_pos

    @pl.when(i == 0)
    def _():
        for kb in range(nkb):
            rows = slice(kb * t, (kb + 1) * t)
            vb = v_ref[rows, :].astype(F32)
            for g in range(N_KV_HEADS):
                vt_ref[kb, g, 0:HEAD_DIM, :] = vb[:, g * HEAD_DIM:(g + 1) * HEAD_DIM].T.astype(BF16)
                vt_ref[kb, g, HEAD_DIM:, :] = jnp.ones((vt_ref.shape[2] - HEAD_DIM, t), BF16)
            kib_ref[rows, :] = kiw_ref[rows, 0:IDX_DIM].astype(BF16)

    for h in range(IDX_HEADS):
        qih_ref[h] = qi_ref[:, h * IDX_DIM:(h + 1) * IDX_DIM].astype(BF16)
    w_t = kiw_ref[pl.ds(q0, t), :].T[IDX_DIM:IDX_DIM + IDX_HEADS, :] * (IDX_HEADS ** -0.5)

    def score_block(kb, diag):
        kblk = kib_ref[pl.ds(pl.multiple_of(kb * t, t), t), :]
        acc = jnp.zeros((t, t), F32)
        for h in range(IDX_HEADS):
            acc = acc + w_t[h:h + 1, :] * jnp.maximum(_dot_nt(kblk, qih_ref[h]), 0.0)
        if diag:
            acc = jnp.where(causal, acc, -jnp.inf)
        bits = pltpu.bitcast(acc, I32)
        sc_ref[kb] = bits ^ ((bits >> 31) & INT_MAX)

    def score_loop(kb, carry):
        score_block(kb, False)
        return carry

    lax.fori_loop(0, i, score_loop, 0)
    score_block(i, True)

    @pl.when(q0 + t > topk)
    def _():
        def bisect(_, carry):
            lo, hi = carry
            mid = (lo & hi) + ((lo ^ hi) >> 1)

            def count(kb, c):
                ge = jnp.where(sc_ref[kb] >= mid, 1, 0)
                return c + jnp.sum(ge.reshape(t // 8, 8, t), axis=0)

            c = lax.fori_loop(0, i + 1, count, jnp.zeros((8, t), I32))
            enough = jnp.sum(c, axis=0, keepdims=True) >= topk
            return jnp.where(enough, mid, lo), jnp.where(enough, hi, mid)

        lo, _ = lax.fori_loop(0, 32, bisect,
                              (jnp.full((1, t), INT_MIN, I32), jnp.full((1, t), INT_MAX, I32)))
        thr_ref[...] = jnp.where(q0 + qry_pos >= topk, lo, INT_MIN)

    @pl.when(q0 + t <= topk)
    def _():
        thr_ref[...] = jnp.full((1, t), INT_MIN, I32)

    m_ref[...] = jnp.full(m_ref.shape, NEG, F32)
    acc_ref[...] = jnp.zeros(acc_ref.shape, F32)
    thr = thr_ref[...]

    def attn_block(kb, mode):
        ks = pl.ds(pl.multiple_of(kb * t, t), t)
        mask = jnp.where(sc_ref[kb] >= thr, 0.0, NEG)
        if mode == 0:
            mask = jnp.where(causal, mask, NEG)
        mask_ref[...] = mask
        for h in range(N_HEADS):
            g = h // GROUP
            s_ref[h] = _dot_nt(k_ref[ks, g * HEAD_DIM:(g + 1) * HEAD_DIM], q_ref[:, h * HEAD_DIM:(h + 1) * HEAD_DIM])
        for h in range(N_HEADS):
            s = s_ref[h] + mask_ref[...]
            if mode != "far":
                s = s + bias_ref[h, mode]
            m_old = m_ref[h]
            m_new = jnp.maximum(m_old, jnp.max(s, axis=0, keepdims=True))
            p_ref[h] = jnp.exp2(s - m_new).astype(BF16)
            alpha_ref[h] = jnp.exp2(m_old - m_new)
            m_ref[h] = m_new
        for h in range(N_HEADS):
            pv = jnp.dot(vt_ref[kb, h // GROUP], p_ref[h], preferred_element_type=F32)
            acc_ref[h] = alpha_ref[h] * acc_ref[h] + pv

    def far_loop(kb, carry):
        attn_block(kb, "far")
        return carry

    lax.fori_loop(0, jnp.maximum(i - 1, 0), far_loop, 0)

    @pl.when(i >= 1)
    def _():
        attn_block(i - 1, 1)

    attn_block(i, 0)

    for h in range(N_HEADS):
        acc = acc_ref[h]
        out_t = acc[0:HEAD_DIM, :] / acc[HEAD_DIM:HEAD_DIM + 1, :]
        o_ref[:, h * HEAD_DIM:(h + 1) * HEAD_DIM] = out_t.T.astype(o_ref.dtype)


def _attention(qkv, idx, bias_tiles, batch, seq):
    t = ATT_BLOCK
    nq = seq // t
    topk = min(TOPK_MAX, seq // 4)
    attn_dim = N_HEADS * HEAD_DIM
    kv_dim = N_KV_HEADS * HEAD_DIM
    qi_dim = IDX_HEADS * IDX_DIM
    return pl.pallas_call(
        functools.partial(_attn_body, t=t, topk=topk),
        grid=(batch, nq),
        in_specs=[
            pl.BlockSpec((t, attn_dim), lambda b, i: (b * nq + i, 0)),
            pl.BlockSpec((seq, kv_dim), lambda b, i: (b, attn_dim // kv_dim)),
            pl.BlockSpec((seq, kv_dim), lambda b, i: (b, attn_dim // kv_dim + 1)),
            pl.BlockSpec((t, qi_dim), lambda b, i: (b * nq + i, 0)),
            pl.BlockSpec((seq, IDX_PAD), lambda b, i: (b, qi_dim // IDX_PAD)),
            pl.BlockSpec((N_HEADS, 2, t, t), lambda b, i: (0, 0, 0, 0)),
        ],
        out_specs=pl.BlockSpec((t, attn_dim), lambda b, i: (b * nq + i, 0)),
        out_shape=jax.ShapeDtypeStruct((batch * seq, attn_dim), BF16),
        scratch_shapes=[
            pltpu.VMEM((nq, t, t), I32),
            pltpu.VMEM((1, t), I32),
            pltpu.VMEM((nq, N_KV_HEADS, HEAD_DIM + ONES_ROWS, t), BF16),
            pltpu.VMEM((seq, IDX_DIM), BF16),
            pltpu.VMEM((IDX_HEADS, t, IDX_DIM), BF16),
            pltpu.VMEM((t, t), F32),
            pltpu.VMEM((N_HEADS, t, t), F32),
            pltpu.VMEM((N_HEADS, t, t), BF16),
            pltpu.VMEM((N_HEADS, 1, t), F32),
            pltpu.VMEM((N_HEADS, 1, t), F32),
            pltpu.VMEM((N_HEADS, HEAD_DIM + ONES_ROWS, t), F32),
        ],
        compiler_params=_params("parallel", "arbitrary"),
        name="sparse_attention",
    )(qkv, qkv, qkv, idx, idx, bias_tiles)


def _mixer_body(main_ref, halo_ref, ya_ref, g0_ref, g1_ref, g2_ref, wb_ref, pw_ref, ps_ref, cw_ref,
                o_ref, y_ref, *, tm, seq, pool_dim, conv_dim):
    i = pl.program_id(0)

    @pl.when(pl.program_id(1) == 0)
    def _():
        pos0 = (i * tm) % seq
        pos = lax.broadcasted_iota(I32, (tm, 1), 0) + pos0
        keep = jnp.where(pos0 == 0, 0.0, 1.0)

        def ext(c0, c1):
            halo = halo_ref[:, c0:c1].astype(F32) * keep
            return jnp.concatenate([halo, main_ref[:, c0:c1].astype(F32)], axis=0)

        for g, win in enumerate(POOL_WINDOWS):
            c0 = g * POOL_GROUP_DIM
            u = ext(c0, c0 + POOL_GROUP_DIM)
            s, sh = u, 1
            while sh < win:
                s = s + pltpu.roll(s, sh, axis=0)
                sh *= 2
            cnt = jnp.minimum(pos + 1, win).astype(F32)
            dlt = s[POOL_HALO:] / cnt - u[POOL_HALO:]
            yp = jnp.dot(dlt.astype(BF16), pw_ref[g], preferred_element_type=F32)
            y_ref[0, :, c0:c0 + POOL_GROUP_DIM] = (yp * ps_ref[:, c0:c0 + POOL_GROUP_DIM]).astype(BF16)

        z = ext(pool_dim, pool_dim + conv_dim) * ext(pool_dim + conv_dim, pool_dim + 2 * conv_dim)
        yc = z[POOL_HALO:] * cw_ref[CONV_WIDTH - 1:CONV_WIDTH, :]
        for tap in range(1, CONV_WIDTH):
            yc = yc + pltpu.roll(z, tap, axis=0)[POOL_HALO:] * cw_ref[CONV_WIDTH - 1 - tap:CONV_WIDTH - tap, :]
        gate_b = main_ref[:, pool_dim + 2 * conv_dim:pool_dim + 3 * conv_dim].astype(F32)
        y_ref[1] = (gate_b * yc).astype(BF16)

    up0 = jnp.dot(y_ref[0], wb_ref[0], preferred_element_type=F32)
    up1 = jnp.dot(y_ref[1], wb_ref[1], preferred_element_type=F32)
    up2 = jnp.dot(ya_ref[...], wb_ref[2], preferred_element_type=F32)
    mixed = g0_ref[...].astype(F32) * up0 + g1_ref[...].astype(F32) * up1 + g2_ref[...].astype(F32) * up2
    o_ref[...] = mixed.astype(o_ref.dtype)


def _mixer(main, y_attn, gates, w_branch, pool_w, pool_scale, conv_w, layer, seq, *, tm, tn):
    tok, width = main.shape
    branch_dim = y_attn.shape[-1]
    d = w_branch.shape[-1]
    nj = d // tn
    halo_blocks = tm // POOL_HALO
    return pl.pallas_call(
        functools.partial(_mixer_body, tm=tm, seq=seq, pool_dim=branch_dim, conv_dim=branch_dim),
        grid=(tok // tm, nj),
        in_specs=[
            pl.BlockSpec((tm, width), lambda i, j: (i, 0)),
            pl.BlockSpec((POOL_HALO, width), lambda i, j: (jnp.maximum(i * halo_blocks - 1, 0), 0)),
            pl.BlockSpec((tm, branch_dim), lambda i, j: (i, 0)),
            pl.BlockSpec((tm, tn), lambda i, j: (i, j)),
            pl.BlockSpec((tm, tn), lambda i, j: (i, nj + j)),
            pl.BlockSpec((tm, tn), lambda i, j: (i, 2 * nj + j)),
            pl.BlockSpec((None, 3, branch_dim, tn), lambda i, j: (layer, 0, 0, j)),
            pl.BlockSpec((None,) + pool_w.shape[1:], lambda i, j: (layer, 0, 0, 0)),
            pl.BlockSpec((None, 1, branch_dim), lambda i, j: (layer, 0, 0)),
            pl.BlockSpec((None, CONV_WIDTH, branch_dim), lambda i, j: (layer, 0, 0)),
        ],
        out_specs=pl.BlockSpec((tm, tn), lambda i, j: (i, j)),
        out_shape=jax.ShapeDtypeStruct((tok, d), BF16),
        scratch_shapes=[pltpu.VMEM((2, tm, branch_dim), BF16)],
        compiler_params=_params("parallel", "arbitrary"),
        name="mixer_branches",
    )(main, main, y_attn, gates, gates, gates, w_branch, pool_w, pool_scale, conv_w)


def _proj_norm_res_body(a_ref, w_ref, x_ref, g_ref, o_ref):
    m = jnp.dot(a_ref[...], w_ref[...], preferred_element_type=F32)
    o_ref[...] = x_ref[...] + _rms(m, g_ref[...])


def _proj_norm_res(a, w, layer, x, gain, *, tm):
    tok, d = x.shape
    k = a.shape[-1]
    return pl.pallas_call(
        _proj_norm_res_body,
        grid=(tok // tm,),
        in_specs=[
            pl.BlockSpec((tm, k), lambda i: (i, 0)),
            pl.BlockSpec((None, k, d), lambda i: (layer, 0, 0)),
            pl.BlockSpec((tm, d), lambda i: (i, 0)),
            pl.BlockSpec((1, d), lambda i: (0, 0)),
        ],
        out_specs=pl.BlockSpec((tm, d), lambda i: (i, 0)),
        out_shape=jax.ShapeDtypeStruct((tok, d), F32),
        compiler_params=_params("parallel"),
        name="out_proj",
    )(a, w, x, gain)


def _mlp_body(x_ref, gpre_ref, gpost_ref, wu_ref, wd_ref, o_ref, h_ref, acc_ref):
    j = pl.program_id(1)

    @pl.when(j == 0)
    def _():
        h_ref[...] = _rms(x_ref[...], gpre_ref[...]).astype(BF16)
        acc_ref[...] = jnp.zeros(acc_ref.shape, F32)

    hid = jnp.dot(h_ref[...], wu_ref[...], preferred_element_type=F32)
    hid = jnp.square(jnp.maximum(hid, 0.0)).astype(BF16)
    acc_ref[...] += jnp.dot(hid, wd_ref[...], preferred_element_type=F32)

    @pl.when(j == pl.num_programs(1) - 1)
    def _():
        o_ref[...] = x_ref[...] + _rms(acc_ref[...], gpost_ref[...])


def _mlp(x, gpre, gpost, w_up, w_down, layer, *, tm, th):
    tok, d = x.shape
    hidden = w_up.shape[-1]
    return pl.pallas_call(
        _mlp_body,
        grid=(tok // tm, hidden // th),
        in_specs=[
            pl.BlockSpec((tm, d), lambda i, j: (i, 0)),
            pl.BlockSpec((1, d), lambda i, j: (0, 0)),
            pl.BlockSpec((1, d), lambda i, j: (0, 0)),
            pl.BlockSpec((None, d, th), lambda i, j: (layer, 0, j)),
            pl.BlockSpec((None, th, d), lambda i, j: (layer, j, 0)),
        ],
        out_specs=pl.BlockSpec((tm, d), lambda i, j: (i, 0)),
        out_shape=jax.ShapeDtypeStruct((tok, d), F32),
        scratch_shapes=[pltpu.VMEM((tm, d), BF16), pltpu.VMEM((tm, d), F32)],
        compiler_params=_params("parallel", "arbitrary"),
        name="mlp",
    )(x, gpre, gpost, w_up, w_down)


def _tile(n, pref):
    t = min(n, pref)
    assert n % t == 0, (n, pref)
    return t


@jax.jit
def _forward(x, norm_gains, w_in, pool_w, pool_scale, conv_w, rel_bias, w_branch,
             w_gate, b_gate, w_out, w_up, w_down):
    batch, seq, d = x.shape
    depth = w_in.shape[0]
    tok = batch * seq
    branch_dim = d // 2
    attn_dim = N_HEADS * HEAD_DIM
    kv_dim = N_KV_HEADS * HEAD_DIM
    qi_dim = IDX_HEADS * IDX_DIM
    assert branch_dim == attn_dim == len(POOL_WINDOWS) * POOL_GROUP_DIM
    assert seq % ATT_BLOCK == 0 and ATT_BLOCK >= MAX_DISTANCE

    main_w = 4 * branch_dim
    qkv_w = attn_dim + 2 * kv_dim
    idx_w = qi_dim + IDX_DIM + IDX_HEADS
    w_main = w_in[:, :, :main_w].astype(BF16)
    w_qkv = w_in[:, :, main_w:main_w + qkv_w].astype(BF16)
    w_idx = jnp.pad(w_in[:, :, main_w + qkv_w:], ((0, 0), (0, 0), (0, qi_dim + IDX_PAD - idx_w))).astype(BF16)
    w_gate_b = w_gate.astype(BF16)
    w_branch_b = w_branch.astype(BF16)
    w_out_b = w_out.astype(BF16)
    w_up_b = w_up.astype(BF16)
    w_down_b = w_down.astype(BF16)
    pool_w_b = pool_w.astype(BF16)

    ones = lambda n: jnp.ones((n,), F32)
    zeros = lambda n: jnp.zeros((n,), F32)
    post_main = jnp.stack([ones(main_w), zeros(main_w)])
    post_qkv = jnp.stack([jnp.concatenate([jnp.full((attn_dim,), HEAD_DIM ** -0.5 * LOG2E, F32), ones(2 * kv_dim)]),
                          zeros(qkv_w)])
    post_idx = jnp.stack([ones(qi_dim + IDX_PAD), zeros(qi_dim + IDX_PAD)])

    bias_tiles = _bias_tiles(rel_bias, ATT_BLOCK)

    tm = _tile(tok, 1024)
    tm_s = _tile(seq, 512)
    xf = x.reshape(tok, d)
    for l in range(depth):
        g = norm_gains[l]
        g0, g1, g2, g3 = (g[n:n + 1] for n in range(4))
        main = _norm_matmul(xf, g0, w_main, l, post_main, tm=tm, tn=512, out_dtype=BF16, name="proj_main")
        qkv = _norm_matmul(xf, g0, w_qkv, l, post_qkv, tm=tm, tn=512, out_dtype=BF16, name="proj_qkv")
        idx = _norm_matmul(xf, g0, w_idx, l, post_idx, tm=tm, tn=qi_dim + IDX_PAD, out_dtype=F32, name="proj_idx")
        post_gate = jnp.stack([ones(3 * d), b_gate[l].reshape(3 * d)])
        gates = _norm_matmul(xf, g0, w_gate_b, l, post_gate, tm=tm, tn=512, out_dtype=BF16, act="sigmoid",
                             name="proj_gates")
        y_attn = _attention(qkv, idx, bias_tiles, batch, seq)
        mixed = _mixer(main, y_attn, gates, w_branch_b, pool_w_b, pool_scale.reshape(depth, 1, branch_dim),
                       conv_w, l, seq, tm=tm_s, tn=512)
        xf = _proj_norm_res(mixed, w_out_b, l, xf, g1, tm=tm_s)
        xf = _mlp(xf, g2, g3, w_up_b, w_down_b, l, tm=tm_s, th=1024)
    return xf.reshape(batch, seq, d)


def kernel(x, norm_gains, w_in, pool_w, pool_scale, conv_w, rel_bias, w_branch, w_gate, b_gate, w_out, w_up, w_down):
    return _forward(x, norm_gains, w_in, pool_w, pool_scale, conv_w, rel_bias, w_branch,
                    w_gate, b_gate, w_out, w_up, w_down)
```

```python
import functools
import math

import jax
import jax.numpy as jnp
import numpy as np
from jax import lax
from jax.experimental import pallas as pl
from jax.experimental.pallas import tpu as pltpu

F32 = jnp.float32
BF16 = jnp.bfloat16
I32 = jnp.int32

NORM_EPS = 1e-6
POOL_WINDOWS = (2, 4, 8, 16)
POOL_GROUP_DIM = 256
POOL_HALO = 16
CONV_WIDTH = 3
N_HEADS = 8
N_KV_HEADS = 2
GROUP = N_HEADS // N_KV_HEADS
HEAD_DIM = 128
IDX_HEADS = 16
IDX_DIM = 64
TOPK_MAX = 256
NUM_BUCKETS = 32
MAX_EXACT = NUM_BUCKETS // 2
MAX_DISTANCE = 128
ATT_BLOCK = 256
IDX_PAD = 128
INT_MIN = -(2 ** 31)
INT_MAX = 2 ** 31 - 1
NEG = -1e30
LOG2E = math.log2(math.e)
ONES_ROWS = 16
VMEM_LIMIT = 56 * 1024 * 1024


def _bucket_thresholds():
    n = np.arange(MAX_EXACT, 8 * MAX_DISTANCE, dtype=np.int64)
    nf = n.astype(np.float32)
    large = MAX_EXACT + (np.log(nf / np.float32(MAX_EXACT)) / np.float32(math.log(MAX_DISTANCE / MAX_EXACT))
                         * np.float32(NUM_BUCKETS - MAX_EXACT)).astype(np.int32)
    large = np.minimum(large, NUM_BUCKETS - 1)
    thr = []
    for b in range(MAX_EXACT + 1, NUM_BUCKETS):
        thr.append(int(n[np.argmax(large >= b)]))
    return tuple(thr)


BUCKET_THRESHOLDS = _bucket_thresholds()


def _params(*sem):
    return pltpu.CompilerParams(dimension_semantics=sem, vmem_limit_bytes=VMEM_LIMIT)


def _rms(x, gain):
    ms = jnp.mean(x * x, axis=-1, keepdims=True)
    return x * lax.rsqrt(ms + NORM_EPS) * gain


def _dot_nt(a, b):
    return lax.dot_general(a, b, (((1,), (1,)), ((), ())), preferred_element_type=F32)


def _proj_body(x_ref, g_ref, win_ref, wg_ref, wkw_ref, post_ref, o_ref, okw_ref, h_ref, *, gate_lo, gate_hi):
    j = pl.program_id(1)
    is_gate = jnp.logical_and(j >= gate_lo, j < gate_hi)

    @pl.when(j == 0)
    def _():
        h = _rms(x_ref[...], g_ref[...]).astype(BF16)
        h_ref[...] = h
        okw_ref[...] = jnp.dot(h, wkw_ref[...], preferred_element_type=F32)

    @pl.when(jnp.logical_not(is_gate))
    def _():
        y = jnp.dot(h_ref[...], win_ref[...], preferred_element_type=F32)
        o_ref[...] = (y * post_ref[0:1, :]).astype(o_ref.dtype)

    @pl.when(is_gate)
    def _():
        y = jnp.dot(h_ref[...], wg_ref[...], preferred_element_type=F32) + post_ref[1:2, :]
        o_ref[...] = (1.0 / (1.0 + jnp.exp(-y))).astype(o_ref.dtype)


def _projections(x, gain, w_in, w_gate, w_kw, layer, post, n_main, n_in, *, tm, tn):
    tok, d = x.shape
    n_gate = w_gate.shape[-1]
    assert n_main % tn == 0 and n_in % tn == 0 and n_gate % tn == 0
    gate_lo = n_main // tn
    gate_blocks = n_gate // tn
    gate_hi = gate_lo + gate_blocks
    kw = w_kw.shape[-1]
    return pl.pallas_call(
        functools.partial(_proj_body, gate_lo=gate_lo, gate_hi=gate_hi),
        grid=(tok // tm, (n_in + n_gate) // tn),
        in_specs=[
            pl.BlockSpec((tm, d), lambda i, j: (i, 0)),
            pl.BlockSpec((1, d), lambda i, j: (0, 0)),
            pl.BlockSpec((None, d, tn),
                         lambda i, j: (layer, 0, jnp.where(j < gate_lo, j, jnp.maximum(j - gate_blocks, gate_lo)))),
            pl.BlockSpec((None, d, tn), lambda i, j: (layer, 0, jnp.clip(j - gate_lo, 0, gate_blocks - 1))),
            pl.BlockSpec((None, d, kw), lambda i, j: (layer, 0, 0)),
            pl.BlockSpec((2, tn), lambda i, j: (0, j)),
        ],
        out_specs=[
            pl.BlockSpec((tm, tn), lambda i, j: (i, j)),
            pl.BlockSpec((tm, kw), lambda i, j: (i, 0)),
        ],
        out_shape=[
            jax.ShapeDtypeStruct((tok, n_in + n_gate), BF16),
            jax.ShapeDtypeStruct((tok, kw), F32),
        ],
        scratch_shapes=[pltpu.VMEM((tm, d), BF16)],
        compiler_params=_params("parallel", "arbitrary"),
        name="projections",
    )(x, gain, w_in, w_gate, w_kw, post)


def _bias_tiles_body(rel_ref, o_ref, *, t):
    row = lax.broadcasted_iota(I32, (t, t), 0)
    col = lax.broadcasted_iota(I32, (t, t), 1)
    for off in range(2):
        n = jnp.maximum(col - row + off * t, 0)
        large = jnp.full((t, t), MAX_EXACT, I32)
        for thr in BUCKET_THRESHOLDS:
            large = large + jnp.where(n >= thr, 1, 0)
        bucket = jnp.where(n < MAX_EXACT, n, large)
        for h in range(N_HEADS):
            val = jnp.zeros((t, t), F32)
            for b in range(NUM_BUCKETS):
                val = jnp.where(bucket == b, rel_ref[b, h], val)
            o_ref[h, off] = (val - rel_ref[NUM_BUCKETS - 1, h]) * LOG2E


def _bias_tiles(rel_bias, t):
    return pl.pallas_call(
        functools.partial(_bias_tiles_body, t=t),
        in_specs=[pl.BlockSpec(memory_space=pltpu.SMEM)],
        out_specs=pl.BlockSpec(memory_space=pltpu.VMEM),
        out_shape=jax.ShapeDtypeStruct((N_HEADS, 2, t, t), F32),
        compiler_params=pltpu.CompilerParams(vmem_limit_bytes=VMEM_LIMIT),
        name="bias_tiles",
    )(rel_bias)


def _attn_body(q_ref, k_ref, v_ref, qia_ref, qib_ref, kiw_ref, bias_ref, o_ref,
               sc_ref, thr_ref, vt_ref, kib_ref, qih_ref, mask_ref, s_ref, p_ref, alpha_ref, m_ref, acc_ref,
               *, t, topk):
    i = pl.program_id(1)
    nkb = sc_ref.shape[0]
    q0 = pl.multiple_of(i * t, t)
    key_pos = lax.broadcasted_iota(I32, (t, 1), 0)
    qry_pos = lax.broadcasted_iota(I32, (1, t), 1)
    causal = key_pos <= qry_pos

    @pl.when(i == 0)
    def _():
        for kb in range(nkb):
            rows = slice(kb * t, (kb + 1) * t)
            vb = v_ref[rows, :].astype(F32)
            for g in range(N_KV_HEADS):
                vt_ref[kb, g, 0:HEAD_DIM, :] = vb[:, g * HEAD_DIM:(g + 1) * HEAD_DIM].T.astype(BF16)
                vt_ref[kb, g, HEAD_DIM:, :] = jnp.ones((vt_ref.shape[2] - HEAD_DIM, t), BF16)
            kib_ref[rows, :] = kiw_ref[rows, 0:IDX_DIM].astype(BF16)

    half = IDX_HEADS // 2
    for h in range(IDX_HEADS):
        src = qia_ref if h < half else qib_ref
        qih_ref[h] = src[:, (h % half) * IDX_DIM:(h % half + 1) * IDX_DIM]
    w_t = kiw_ref[pl.ds(q0, t), :].T[IDX_DIM:IDX_DIM + IDX_HEADS, :] * (IDX_HEADS ** -0.5)

    def score_block(kb, diag):
        kblk = kib_ref[pl.ds(pl.multiple_of(kb * t, t), t), :]
        acc = jnp.zeros((t, t), F32)
        for h in range(IDX_HEADS):
            acc = acc + w_t[h:h + 1, :] * jnp.maximum(_dot_nt(kblk, qih_ref[h]), 0.0)
        if diag:
            acc = jnp.where(causal, acc, -jnp.inf)
        bits = pltpu.bitcast(acc, I32)
        sc_ref[kb] = bits ^ ((bits >> 31) & INT_MAX)

    def score_loop(kb, carry):
        score_block(kb, False)
        return carry

    lax.fori_loop(0, i, score_loop, 0)
    score_block(i, True)

    @pl.when(q0 + t > topk)
    def _():
        def bisect(_, carry):
            lo, hi = carry
            mid = (lo & hi) + ((lo ^ hi) >> 1)

            def count(kb, c):
                ge = jnp.where(sc_ref[kb] >= mid, 1, 0)
                return c + jnp.sum(ge.reshape(t // 8, 8, t), axis=0)

            c = lax.fori_loop(0, i + 1, count, jnp.zeros((8, t), I32))
            enough = jnp.sum(c, axis=0, keepdims=True) >= topk
            return jnp.where(enough, mid, lo), jnp.where(enough, hi, mid)

        lo, _ = lax.fori_loop(0, 32, bisect,
                              (jnp.full((1, t), INT_MIN, I32), jnp.full((1, t), INT_MAX, I32)))
        thr_ref[...] = jnp.where(q0 + qry_pos >= topk, lo, INT_MIN)

    @pl.when(q0 + t <= topk)
    def _():
        thr_ref[...] = jnp.full((1, t), INT_MIN, I32)

    m_ref[...] = jnp.full(m_ref.shape, NEG, F32)
    acc_ref[...] = jnp.zeros(acc_ref.shape, F32)
    thr = thr_ref[...]

    def attn_block(kb, mode):
        ks = pl.ds(pl.multiple_of(kb * t, t), t)
        mask = jnp.where(sc_ref[kb] >= thr, 0.0, NEG)
        if mode == 0:
            mask = jnp.where(causal, mask, NEG)
        mask_ref[...] = mask
        for h in range(N_HEADS):
            g = h // GROUP
            s_ref[h] = _dot_nt(k_ref[ks, g * HEAD_DIM:(g + 1) * HEAD_DIM], q_ref[:, h * HEAD_DIM:(h + 1) * HEAD_DIM])
        for h in range(N_HEADS):
            s = s_ref[h] + mask_ref[...]
            if mode != "far":
                s = s + bias_ref[h, mode]
            m_old = m_ref[h]
            m_new = jnp.maximum(m_old, jnp.max(s, axis=0, keepdims=True))
            p_ref[h] = jnp.exp2(s - m_new).astype(BF16)
            alpha_ref[h] = jnp.exp2(m_old - m_new)
            m_ref[h] = m_new
        for h in range(N_HEADS):
            pv = jnp.dot(vt_ref[kb, h // GROUP], p_ref[h], preferred_element_type=F32)
            acc_ref[h] = alpha_ref[h] * acc_ref[h] + pv

    def far_loop(kb, carry):
        attn_block(kb, "far")
        return carry

    lax.fori_loop(0, jnp.maximum(i - 1, 0), far_loop, 0)

    @pl.when(i >= 1)
    def _():
        attn_block(i - 1, 1)

    attn_block(i, 0)

    for h in range(N_HEADS):
        acc = acc_ref[h]
        out_t = acc[0:HEAD_DIM, :] / acc[HEAD_DIM:HEAD_DIM + 1, :]
        o_ref[:, h * HEAD_DIM:(h + 1) * HEAD_DIM] = out_t.T.astype(o_ref.dtype)


def _attention(proj, kiw, bias_tiles, batch, seq, q_col):
    t = ATT_BLOCK
    nq = seq // t
    topk = min(TOPK_MAX, seq // 4)
    attn_dim = N_HEADS * HEAD_DIM
    kv_dim = N_KV_HEADS * HEAD_DIM
    qi_half = IDX_HEADS * IDX_DIM // 2
    k_col = q_col + attn_dim
    qi_col = k_col + 2 * kv_dim
    assert q_col % attn_dim == 0 and k_col % kv_dim == 0 and qi_col % qi_half == 0
    return pl.pallas_call(
        functools.partial(_attn_body, t=t, topk=topk),
        grid=(batch, nq),
        in_specs=[
            pl.BlockSpec((t, attn_dim), lambda b, i: (b * nq + i, q_col // attn_dim)),
            pl.BlockSpec((seq, kv_dim), lambda b, i: (b, k_col // kv_dim)),
            pl.BlockSpec((seq, kv_dim), lambda b, i: (b, k_col // kv_dim + 1)),
            pl.BlockSpec((t, qi_half), lambda b, i: (b * nq + i, qi_col // qi_half)),
            pl.BlockSpec((t, qi_half), lambda b, i: (b * nq + i, qi_col // qi_half + 1)),
            pl.BlockSpec((seq, IDX_PAD), lambda b, i: (b, 0)),
            pl.BlockSpec((N_HEADS, 2, t, t), lambda b, i: (0, 0, 0, 0)),
        ],
        out_specs=pl.BlockSpec((t, attn_dim), lambda b, i: (b * nq + i, 0)),
        out_shape=jax.ShapeDtypeStruct((batch * seq, attn_dim), BF16),
        scratch_shapes=[
            pltpu.VMEM((nq, t, t), I32),
            pltpu.VMEM((1, t), I32),
            pltpu.VMEM((nq, N_KV_HEADS, HEAD_DIM + ONES_ROWS, t), BF16),
            pltpu.VMEM((seq, IDX_DIM), BF16),
            pltpu.VMEM((IDX_HEADS, t, IDX_DIM), BF16),
            pltpu.VMEM((t, t), F32),
            pltpu.VMEM((N_HEADS, t, t), F32),
            pltpu.VMEM((N_HEADS, t, t), BF16),
            pltpu.VMEM((N_HEADS, 1, t), F32),
            pltpu.VMEM((N_HEADS, 1, t), F32),
            pltpu.VMEM((N_HEADS, HEAD_DIM + ONES_ROWS, t), F32),
        ],
        compiler_params=_params("parallel", "arbitrary"),
        name="sparse_attention",
    )(proj, proj, proj, proj, proj, kiw, bias_tiles)


def _mixer_body(main_ref, halo_ref, ya_ref, g0_ref, g1_ref, g2_ref, wb_ref, pw_ref, ps_ref, cw_ref,
                o_ref, y_ref, acc_ref, *, tm, tn, seq, pool_dim, conv_dim):
    pos0 = (pl.program_id(0) * tm) % seq
    pos = lax.broadcasted_iota(I32, (tm, 1), 0) + pos0
    keep = jnp.where(pos0 == 0, 0.0, 1.0)
    n_out = o_ref.shape[-1]

    def branch(n, y, c0):
        return g_refs[n][:, c0:c0 + tn].astype(F32) * jnp.dot(y, wb_ref[n, :, c0:c0 + tn], preferred_element_type=F32)

    g_refs = (g0_ref, g1_ref, g2_ref)
    for c0 in range(0, n_out, tn):
        acc_ref[:, c0:c0 + tn] = branch(2, ya_ref[...], c0)

    def ext(c0, c1):
        halo = halo_ref[:, c0:c1].astype(F32) * keep
        return jnp.concatenate([halo, main_ref[:, c0:c1].astype(F32)], axis=0)

    for g, win in enumerate(POOL_WINDOWS):
        c0 = g * POOL_GROUP_DIM
        u = ext(c0, c0 + POOL_GROUP_DIM)
        s, sh = u, 1
        while sh < win:
            s = s + pltpu.roll(s, sh, axis=0)
            sh *= 2
        cnt = jnp.minimum(pos + 1, win).astype(F32)
        dlt = s[POOL_HALO:] / cnt - u[POOL_HALO:]
        yp = jnp.dot(dlt.astype(BF16), pw_ref[g], preferred_element_type=F32)
        y_ref[0, :, c0:c0 + POOL_GROUP_DIM] = (yp * ps_ref[:, c0:c0 + POOL_GROUP_DIM]).astype(BF16)

    z = ext(pool_dim, pool_dim + conv_dim) * ext(pool_dim + conv_dim, pool_dim + 2 * conv_dim)
    yc = z[POOL_HALO:] * cw_ref[CONV_WIDTH - 1:CONV_WIDTH, :]
    for tap in range(1, CONV_WIDTH):
        yc = yc + pltpu.roll(z, tap, axis=0)[POOL_HALO:] * cw_ref[CONV_WIDTH - 1 - tap:CONV_WIDTH - tap, :]
    gate_b = main_ref[:, pool_dim + 2 * conv_dim:pool_dim + 3 * conv_dim].astype(F32)
    y_ref[1] = (gate_b * yc).astype(BF16)

    for c0 in range(0, n_out, tn):
        mixed = acc_ref[:, c0:c0 + tn] + branch(0, y_ref[0], c0) + branch(1, y_ref[1], c0)
        o_ref[:, c0:c0 + tn] = mixed.astype(o_ref.dtype)


def _mixer(proj, gate_col, y_attn, w_branch, pool_w, pool_scale, conv_w, layer, seq, *, tm, tn):
    main = gates = proj
    tok = proj.shape[0]
    branch_dim = y_attn.shape[-1]
    width = 4 * branch_dim
    d = w_branch.shape[-1]
    assert gate_col % d == 0 and d % tn == 0
    gj = gate_col // d
    halo_blocks = tm // POOL_HALO
    resident = dict(pipeline_mode=pl.Buffered(1))
    return pl.pallas_call(
        functools.partial(_mixer_body, tm=tm, tn=tn, seq=seq, pool_dim=branch_dim, conv_dim=branch_dim),
        grid=(tok // tm,),
        in_specs=[
            pl.BlockSpec((tm, width), lambda i: (i, 0)),
            pl.BlockSpec((POOL_HALO, width), lambda i: (jnp.maximum(i * halo_blocks - 1, 0), 0)),
            pl.BlockSpec((tm, branch_dim), lambda i: (i, 0)),
            pl.BlockSpec((tm, d), lambda i: (i, gj)),
            pl.BlockSpec((tm, d), lambda i: (i, gj + 1)),
            pl.BlockSpec((tm, d), lambda i: (i, gj + 2)),
            pl.BlockSpec((None, 3, branch_dim, d), lambda i: (layer, 0, 0, 0), **resident),
            pl.BlockSpec((None,) + pool_w.shape[1:], lambda i: (layer, 0, 0, 0), **resident),
            pl.BlockSpec((None, 1, branch_dim), lambda i: (layer, 0, 0)),
            pl.BlockSpec((None, CONV_WIDTH, branch_dim), lambda i: (layer, 0, 0)),
        ],
        out_specs=pl.BlockSpec((tm, d), lambda i: (i, 0)),
        out_shape=jax.ShapeDtypeStruct((tok, d), BF16),
        scratch_shapes=[pltpu.VMEM((2, tm, branch_dim), BF16), pltpu.VMEM((tm, d), F32)],
        compiler_params=_params("parallel"),
        name="mixer_branches",
    )(main, main, y_attn, gates, gates, gates, w_branch, pool_w, pool_scale, conv_w)


def _proj_norm_res_body(a_ref, w_ref, x_ref, g_ref, o_ref):
    m = jnp.dot(a_ref[...], w_ref[...], preferred_element_type=F32)
    o_ref[...] = x_ref[...] + _rms(m, g_ref[...])


def _proj_norm_res(a, w, layer, x, gain, *, tm):
    tok, d = x.shape
    k = a.shape[-1]
    return pl.pallas_call(
        _proj_norm_res_body,
        grid=(tok // tm,),
        in_specs=[
            pl.BlockSpec((tm, k), lambda i: (i, 0)),
            pl.BlockSpec((None, k, d), lambda i: (layer, 0, 0)),
            pl.BlockSpec((tm, d), lambda i: (i, 0)),
            pl.BlockSpec((1, d), lambda i: (0, 0)),
        ],
        out_specs=pl.BlockSpec((tm, d), lambda i: (i, 0)),
        out_shape=jax.ShapeDtypeStruct((tok, d), F32),
        compiler_params=_params("parallel"),
        name="out_proj",
    )(a, w, x, gain)


def _mlp_body(x_ref, gpre_ref, gpost_ref, wu_ref, wd_ref, o_ref, h_ref, acc_ref):
    j = pl.program_id(1)

    @pl.when(j == 0)
    def _():
        h_ref[...] = _rms(x_ref[...], gpre_ref[...]).astype(BF16)
        acc_ref[...] = jnp.zeros(acc_ref.shape, F32)

    hid = jnp.dot(h_ref[...], wu_ref[...], preferred_element_type=F32)
    hid = jnp.square(jnp.maximum(hid, 0.0)).astype(BF16)
    acc_ref[...] += jnp.dot(hid, wd_ref[...], preferred_element_type=F32)

    @pl.when(j == pl.num_programs(1) - 1)
    def _():
        o_ref[...] = x_ref[...] + _rms(acc_ref[...], gpost_ref[...])


def _mlp(x, gpre, gpost, w_up, w_down, layer, *, tm, th):
    tok, d = x.shape
    hidden = w_up.shape[-1]
    return pl.pallas_call(
        _mlp_body,
        grid=(tok // tm, hidden // th),
        in_specs=[
            pl.BlockSpec((tm, d), lambda i, j: (i, 0)),
            pl.BlockSpec((1, d), lambda i, j: (0, 0)),
            pl.BlockSpec((1, d), lambda i, j: (0, 0)),
            pl.BlockSpec((None, d, th), lambda i, j: (layer, 0, j)),
            pl.BlockSpec((None, th, d), lambda i, j: (layer, j, 0)),
        ],
        out_specs=pl.BlockSpec((tm, d), lambda i, j: (i, 0)),
        out_shape=jax.ShapeDtypeStruct((tok, d), F32),
        scratch_shapes=[pltpu.VMEM((tm, d), BF16), pltpu.VMEM((tm, d), F32)],
        compiler_params=_params("parallel", "arbitrary"),
        name="mlp",
    )(x, gpre, gpost, w_up, w_down)


def _tile(n, pref):
    t = min(n, pref)
    assert n % t == 0, (n, pref)
    return t


@jax.jit
def _forward(x, norm_gains, w_in, pool_w, pool_scale, conv_w, rel_bias, w_branch,
             w_gate, b_gate, w_out, w_up, w_down):
    batch, seq, d = x.shape
    depth = w_in.shape[0]
    tok = batch * seq
    branch_dim = d // 2
    attn_dim = N_HEADS * HEAD_DIM
    kv_dim = N_KV_HEADS * HEAD_DIM
    qi_dim = IDX_HEADS * IDX_DIM
    assert branch_dim == attn_dim == len(POOL_WINDOWS) * POOL_GROUP_DIM
    assert seq % ATT_BLOCK == 0 and ATT_BLOCK >= MAX_DISTANCE

    main_w = 4 * branch_dim
    n_in = main_w + attn_dim + 2 * kv_dim + qi_dim
    kw_w = IDX_DIM + IDX_HEADS
    assert w_in.shape[-1] == n_in + kw_w
    w_in_b = w_in.astype(BF16)
    w_kw = jnp.pad(w_in[:, :, n_in:], ((0, 0), (0, 0), (0, IDX_PAD - kw_w))).astype(BF16)
    w_gate_b = w_gate.astype(BF16)
    w_branch_b = w_branch.astype(BF16)
    w_out_b = w_out.astype(BF16)
    w_up_b = w_up.astype(BF16)
    w_down_b = w_down.astype(BF16)
    pool_w_b = pool_w.astype(BF16)

    gate_col = main_w
    q_col = main_w + 3 * d
    scale = jnp.concatenate([jnp.ones((q_col,), F32), jnp.full((attn_dim,), HEAD_DIM ** -0.5 * LOG2E, F32),
                             jnp.ones((n_in - main_w - attn_dim,), F32)])

    bias_tiles = _bias_tiles(rel_bias, ATT_BLOCK)

    tm = _tile(tok, 1024)
    tm_s = _tile(seq, 512)
    xf = x.reshape(tok, d)
    for l in range(depth):
        g = norm_gains[l]
        g0, g1, g2, g3 = (g[n:n + 1] for n in range(4))
        bias = jnp.concatenate([jnp.zeros((main_w,), F32), b_gate[l].reshape(3 * d), jnp.zeros((n_in - main_w,), F32)])
        proj, kiw = _projections(xf, g0, w_in_b, w_gate_b, w_kw, l, jnp.stack([scale, bias]), main_w, n_in,
                                 tm=tm, tn=512)
        y_attn = _attention(proj, kiw, bias_tiles, batch, seq, q_col)
        mixed = _mixer(proj, gate_col, y_attn, w_branch_b, pool_w_b, pool_scale.reshape(depth, 1, branch_dim),
                       conv_w, l, seq, tm=tm_s, tn=1024)
        xf = _proj_norm_res(mixed, w_out_b, l, xf, g1, tm=tm_s)
        xf = _mlp(xf, g2, g3, w_up_b, w_down_b, l, tm=tm_s, th=1024)
    return xf.reshape(batch, seq, d)


def kernel(x, norm_gains, w_in, pool_w, pool_scale, conv_w, rel_bias, w_branch, w_gate, b_gate, w_out, w_up, w_down):
    return _forward(x, norm_gains, w_in, pool_w, pool_scale, conv_w, rel_bias, w_branch,
                    w_gate, b_gate, w_out, w_up, w_down)
```

```python
import functools
import math

import jax
import jax.numpy as jnp
import numpy as np
from jax import lax
from jax.experimental import pallas as pl
from jax.experimental.pallas import tpu as pltpu

F32 = jnp.float32
BF16 = jnp.bfloat16
I32 = jnp.int32

NORM_EPS = 1e-6
POOL_WINDOWS = (2, 4, 8, 16)
POOL_GROUP_DIM = 256
POOL_HALO = 16
CONV_WIDTH = 3
N_HEADS = 8
N_KV_HEADS = 2
GROUP = N_HEADS // N_KV_HEADS
HEAD_DIM = 128
IDX_HEADS = 16
IDX_DIM = 64
TOPK_MAX = 256
NUM_BUCKETS = 32
MAX_EXACT = NUM_BUCKETS // 2
MAX_DISTANCE = 128
ATT_BLOCK = 256
IDX_PAD = 128
KEY_BITS = 32
INT_MIN = -(2 ** 31)
INT_MAX = 2 ** 31 - 1
NEG = -1e30
LOG2E = math.log2(math.e)
ONES_ROWS = 16
VMEM_LIMIT = 56 * 1024 * 1024
MXU_WIDTH = 256


def _bucket_thresholds():
    n = np.arange(MAX_EXACT, 8 * MAX_DISTANCE, dtype=np.int64)
    nf = n.astype(np.float32)
    large = MAX_EXACT + (np.log(nf / np.float32(MAX_EXACT)) / np.float32(math.log(MAX_DISTANCE / MAX_EXACT))
                         * np.float32(NUM_BUCKETS - MAX_EXACT)).astype(np.int32)
    large = np.minimum(large, NUM_BUCKETS - 1)
    thr = []
    for b in range(MAX_EXACT + 1, NUM_BUCKETS):
        thr.append(int(n[np.argmax(large >= b)]))
    return tuple(thr)


BUCKET_THRESHOLDS = _bucket_thresholds()


def _params(*sem):
    return pltpu.CompilerParams(dimension_semantics=sem, vmem_limit_bytes=VMEM_LIMIT)


def _rms(x, gain):
    ms = jnp.mean(x * x, axis=-1, keepdims=True)
    return x * lax.rsqrt(ms + NORM_EPS) * gain


def _dot_nt(a, b):
    return lax.dot_general(a, b, (((1,), (1,)), ((), ())), preferred_element_type=F32)


def _proj_body(x_ref, g_ref, win_ref, wg_ref, wkw_ref, post_ref, o_ref, okw_ref, h_ref, *, gate_lo, gate_hi):
    j = pl.program_id(1)
    is_gate = jnp.logical_and(j >= gate_lo, j < gate_hi)

    @pl.when(j == 0)
    def _():
        h = _rms(x_ref[...], g_ref[...]).astype(BF16)
        h_ref[...] = h
        okw_ref[...] = jnp.dot(h, wkw_ref[...], preferred_element_type=F32)

    chunks = [slice(c, c + MXU_WIDTH) for c in range(0, o_ref.shape[-1], MXU_WIDTH)]

    @pl.when(jnp.logical_not(is_gate))
    def _():
        for cs in chunks:
            y = jnp.dot(h_ref[...], win_ref[:, cs].astype(BF16), preferred_element_type=F32)
            o_ref[:, cs] = (y * post_ref[0:1, cs]).astype(o_ref.dtype)

    @pl.when(is_gate)
    def _():
        for cs in chunks:
            y = jnp.dot(h_ref[...], wg_ref[:, cs].astype(BF16), preferred_element_type=F32) + post_ref[1:2, cs]
            o_ref[:, cs] = (0.5 * jnp.tanh(0.5 * y) + 0.5).astype(o_ref.dtype)


def _projections(x, gain, w_in, w_gate, w_kw, layer, post, n_main, n_in, *, tm, tn):
    tok, d = x.shape
    n_gate = w_gate.shape[-1]
    assert n_main % tn == 0 and n_in % tn == 0 and n_gate % tn == 0
    gate_lo = n_main // tn
    gate_blocks = n_gate // tn
    gate_hi = gate_lo + gate_blocks
    kw = w_kw.shape[-1]
    return pl.pallas_call(
        functools.partial(_proj_body, gate_lo=gate_lo, gate_hi=gate_hi),
        grid=(tok // tm, (n_in + n_gate) // tn),
        in_specs=[
            pl.BlockSpec((tm, d), lambda i, j: (i, 0)),
            pl.BlockSpec((1, d), lambda i, j: (0, 0)),
            pl.BlockSpec((None, d, tn),
                         lambda i, j: (layer, 0, jnp.where(j < gate_lo, j, jnp.maximum(j - gate_blocks, gate_lo)))),
            pl.BlockSpec((None, d, tn), lambda i, j: (layer, 0, jnp.clip(j - gate_lo, 0, gate_blocks - 1))),
            pl.BlockSpec((None, d, kw), lambda i, j: (layer, 0, 0)),
            pl.BlockSpec((2, tn), lambda i, j: (0, j)),
        ],
        out_specs=[
            pl.BlockSpec((tm, tn), lambda i, j: (i, j)),
            pl.BlockSpec((tm, kw), lambda i, j: (i, 0)),
        ],
        out_shape=[
            jax.ShapeDtypeStruct((tok, n_in + n_gate), BF16),
            jax.ShapeDtypeStruct((tok, kw), F32),
        ],
        scratch_shapes=[pltpu.VMEM((tm, d), BF16)],
        compiler_params=_params("parallel", "arbitrary"),
        name="projections",
    )(x, gain, w_in, w_gate, w_kw, post)


def _bias_tiles_body(rel_ref, o_ref, *, t):
    row = lax.broadcasted_iota(I32, (t, t), 0)
    col = lax.broadcasted_iota(I32, (t, t), 1)
    for off in range(2):
        n = jnp.maximum(col - row + off * t, 0)
        large = jnp.full((t, t), MAX_EXACT, I32)
        for thr in BUCKET_THRESHOLDS:
            large = large + jnp.where(n >= thr, 1, 0)
        bucket = jnp.where(n < MAX_EXACT, n, large)
        for h in range(N_HEADS):
            val = jnp.zeros((t, t), F32)
            for b in range(NUM_BUCKETS):
                val = jnp.where(bucket == b, rel_ref[b, h], val)
            o_ref[h, off] = (val - rel_ref[NUM_BUCKETS - 1, h]) * LOG2E


def _bias_tiles(rel_bias, t):
    return pl.pallas_call(
        functools.partial(_bias_tiles_body, t=t),
        in_specs=[pl.BlockSpec(memory_space=pltpu.SMEM)],
        out_specs=pl.BlockSpec(memory_space=pltpu.VMEM),
        out_shape=jax.ShapeDtypeStruct((N_HEADS, 2, t, t), F32),
        compiler_params=pltpu.CompilerParams(vmem_limit_bytes=VMEM_LIMIT),
        name="bias_tiles",
    )(rel_bias)


def _attn_body(q_ref, k_ref, v_ref, qia_ref, qib_ref, kiw_ref, bias_ref, o_ref,
               sc_ref, thr_ref, vt_ref, kib_ref, qih_ref, mask_ref, alpha_ref, m_ref, acc_ref, *head_refs,
               t, topk):
    s_ref, p_ref = head_refs[:N_HEADS], head_refs[N_HEADS:]
    i = pl.program_id(1)
    nkb = sc_ref.shape[0]
    q0 = pl.multiple_of(i * t, t)
    key_pos = lax.broadcasted_iota(I32, (t, 1), 0)
    qry_pos = lax.broadcasted_iota(I32, (1, t), 1)
    causal = key_pos <= qry_pos

    @pl.when(i == 0)
    def _():
        for kb in range(nkb):
            rows = slice(kb * t, (kb + 1) * t)
            vb = v_ref[rows, :].astype(F32)
            for g in range(N_KV_HEADS):
                vt_ref[kb, g, 0:HEAD_DIM, :] = vb[:, g * HEAD_DIM:(g + 1) * HEAD_DIM].T.astype(BF16)
                vt_ref[kb, g, HEAD_DIM:, :] = jnp.ones((vt_ref.shape[2] - HEAD_DIM, t), BF16)
            kib_ref[rows, :] = kiw_ref[rows, 0:IDX_DIM].astype(BF16)

    half = IDX_HEADS // 2
    for h in range(IDX_HEADS):
        src = qia_ref if h < half else qib_ref
        qih_ref[h] = src[:, (h % half) * IDX_DIM:(h % half + 1) * IDX_DIM]
    w_t = kiw_ref[pl.ds(q0, t), :].T[IDX_DIM:IDX_DIM + IDX_HEADS, :] * (IDX_HEADS ** -0.5)

    def score_block(kb, diag):
        kblk = kib_ref[pl.ds(pl.multiple_of(kb * t, t), t), :]
        acc = jnp.zeros((t, t), F32)
        for h in range(IDX_HEADS):
            acc = acc + w_t[h:h + 1, :] * jnp.maximum(_dot_nt(kblk, qih_ref[h]), 0.0)
        if diag:
            acc = jnp.where(causal, acc, -jnp.inf)
        bits = pltpu.bitcast(acc, I32)
        sc_ref[kb] = bits ^ ((bits >> 31) & INT_MAX)

    def score_loop(kb, carry):
        score_block(kb, False)
        return carry

    lax.fori_loop(0, i, score_loop, 0)
    score_block(i, True)

    @pl.when(q0 + t > topk)
    def _():
        needs = q0 + qry_pos >= topk

        def bisect(_, state):
            lo, hi, cnt_lo, cnt_hi = state
            mid = (lo & hi) + ((lo ^ hi) >> 1)

            def count(kb, c):
                ge = jnp.where(sc_ref[kb] >= mid, 1, 0)
                return c + jnp.sum(ge.reshape(t // 8, 8, t), axis=0)

            c = jnp.sum(lax.fori_loop(0, i + 1, count, jnp.zeros((8, t), I32)), axis=0, keepdims=True)
            enough = c >= topk
            return (jnp.where(enough, mid, lo), jnp.where(enough, hi, mid),
                    jnp.where(enough, c, cnt_lo), jnp.where(enough, cnt_hi, c))

        lo, _, cnt_lo, cnt_hi = lax.fori_loop(
            0, KEY_BITS, bisect,
            (jnp.full((1, t), INT_MIN, I32), jnp.full((1, t), INT_MAX, I32),
             jnp.full((1, t), topk + 1, I32), jnp.zeros((1, t), I32)))
        thr_ref[...] = jnp.where(needs, lo, INT_MIN)

        tied = jnp.where(jnp.logical_and(needs, cnt_lo > topk), 1, 0)

        @pl.when(jnp.max(tied) > 0)
        def _():
            wanted = (topk - cnt_hi).astype(F32)
            tied_f = tied.astype(F32)
            earlier = jnp.where(lax.broadcasted_iota(I32, (t, t), 1) < lax.broadcasted_iota(I32, (t, t), 0),
                                1.0, 0.0).astype(BF16)

            def demote(kb, seen):
                key = sc_ref[kb]
                is_tie = jnp.where(key == lo, tied_f, 0.0)
                rank = seen + jnp.dot(earlier, is_tie.astype(BF16), preferred_element_type=F32)
                drop = is_tie * jnp.where(rank >= wanted, 1.0, 0.0)
                sc_ref[kb] = jnp.where(drop > 0.0, lo - 1, key)
                return seen + jnp.sum(is_tie, axis=0, keepdims=True)

            lax.fori_loop(0, i + 1, demote, jnp.zeros((1, t), F32))

    @pl.when(q0 + t <= topk)
    def _():
        thr_ref[...] = jnp.full((1, t), INT_MIN, I32)

    m_ref[...] = jnp.full(m_ref.shape, NEG, F32)
    acc_ref[...] = jnp.zeros(acc_ref.shape, F32)
    thr = thr_ref[...]

    def attn_block(kb, mode):
        ks = pl.ds(pl.multiple_of(kb * t, t), t)
        mask = jnp.where(sc_ref[kb] >= thr, 0.0, NEG)
        if mode == 0:
            mask = jnp.where(causal, mask, NEG)
        mask_ref[...] = mask

        def logits(h):
            g = h // GROUP
            s_ref[h][...] = _dot_nt(k_ref[ks, g * HEAD_DIM:(g + 1) * HEAD_DIM],
                                    q_ref[:, h * HEAD_DIM:(h + 1) * HEAD_DIM])

        def numerators(h):
            s = s_ref[h][...] + mask_ref[...]
            if mode != "far":
                s = s + bias_ref[h, mode]
            m_old = m_ref[h]
            m_new = jnp.maximum(m_old, jnp.max(s, axis=0, keepdims=True))
            p_ref[h][...] = jnp.exp2(s - m_new).astype(BF16)
            alpha_ref[h] = jnp.exp2(m_old - m_new)
            m_ref[h] = m_new

        def weighted_values(h):
            pv = jnp.dot(vt_ref[kb, h // GROUP], p_ref[h][...], preferred_element_type=F32)
            acc_ref[h] = alpha_ref[h] * acc_ref[h] + pv

        for stage in (logits, numerators, weighted_values):
            for h in range(N_HEADS):
                stage(h)

    def far_loop(kb, carry):
        attn_block(kb, "far")
        return carry

    lax.fori_loop(0, jnp.maximum(i - 1, 0), far_loop, 0)

    @pl.when(i >= 1)
    def _():
        attn_block(i - 1, 1)

    attn_block(i, 0)

    for h in range(N_HEADS):
        acc = acc_ref[h]
        out_t = acc[0:HEAD_DIM, :] / acc[HEAD_DIM:HEAD_DIM + 1, :]
        o_ref[:, h * HEAD_DIM:(h + 1) * HEAD_DIM] = out_t.T.astype(o_ref.dtype)


def _attention(proj, kiw, bias_tiles, batch, seq, q_col):
    t = ATT_BLOCK
    nq = seq // t
    topk = min(TOPK_MAX, seq // 4)
    attn_dim = N_HEADS * HEAD_DIM
    kv_dim = N_KV_HEADS * HEAD_DIM
    qi_half = IDX_HEADS * IDX_DIM // 2
    k_col = q_col + attn_dim
    qi_col = k_col + 2 * kv_dim
    assert q_col % attn_dim == 0 and k_col % kv_dim == 0 and qi_col % qi_half == 0
    return pl.pallas_call(
        functools.partial(_attn_body, t=t, topk=topk),
        grid=(batch, nq),
        in_specs=[
            pl.BlockSpec((t, attn_dim), lambda b, i: (b * nq + i, q_col // attn_dim)),
            pl.BlockSpec((seq, kv_dim), lambda b, i: (b, k_col // kv_dim)),
            pl.BlockSpec((seq, kv_dim), lambda b, i: (b, k_col // kv_dim + 1)),
            pl.BlockSpec((t, qi_half), lambda b, i: (b * nq + i, qi_col // qi_half)),
            pl.BlockSpec((t, qi_half), lambda b, i: (b * nq + i, qi_col // qi_half + 1)),
            pl.BlockSpec((seq, IDX_PAD), lambda b, i: (b, 0)),
            pl.BlockSpec((N_HEADS, 2, t, t), lambda b, i: (0, 0, 0, 0)),
        ],
        out_specs=pl.BlockSpec((t, attn_dim), lambda b, i: (b * nq + i, 0)),
        out_shape=jax.ShapeDtypeStruct((batch * seq, attn_dim), BF16),
        scratch_shapes=[
            pltpu.VMEM((nq, t, t), I32),
            pltpu.VMEM((1, t), I32),
            pltpu.VMEM((nq, N_KV_HEADS, HEAD_DIM + ONES_ROWS, t), BF16),
            pltpu.VMEM((seq, IDX_DIM), BF16),
            pltpu.VMEM((IDX_HEADS, t, IDX_DIM), BF16),
            pltpu.VMEM((t, t), F32),
            pltpu.VMEM((N_HEADS, 1, t), F32),
            pltpu.VMEM((N_HEADS, 1, t), F32),
            pltpu.VMEM((N_HEADS, HEAD_DIM + ONES_ROWS, t), F32),
        ] + [pltpu.VMEM((t, t), F32)] * N_HEADS + [pltpu.VMEM((t, t), BF16)] * N_HEADS,
        compiler_params=_params("parallel", "arbitrary"),
        name="sparse_attention",
    )(proj, proj, proj, proj, proj, kiw, bias_tiles)


def _mixer_body(main_ref, halo_ref, ya_ref, g0_ref, g1_ref, g2_ref, wb_ref, pw_ref, ps_ref, cw_ref,
                o_ref, y_ref, acc_ref, *, tm, tn, seq, pool_dim, conv_dim):
    pos0 = (pl.program_id(0) * tm) % seq
    pos = lax.broadcasted_iota(I32, (tm, 1), 0) + pos0
    keep = jnp.where(pos0 == 0, 0.0, 1.0)
    n_out = o_ref.shape[-1]

    def branch(n, y, c0):
        return g_refs[n][:, c0:c0 + tn].astype(F32) * jnp.dot(y, wb_ref[n, :, c0:c0 + tn], preferred_element_type=F32)

    g_refs = (g0_ref, g1_ref, g2_ref)
    for c0 in range(0, n_out, tn):
        acc_ref[:, c0:c0 + tn] = branch(2, ya_ref[...], c0)

    def ext(c0, c1):
        halo = halo_ref[:, c0:c1].astype(F32) * keep
        return jnp.concatenate([halo, main_ref[:, c0:c1].astype(F32)], axis=0)

    for g, win in enumerate(POOL_WINDOWS):
        c0 = g * POOL_GROUP_DIM
        u = ext(c0, c0 + POOL_GROUP_DIM)
        s, sh = u, 1
        while sh < win:
            s = s + pltpu.roll(s, sh, axis=0)
            sh *= 2
        cnt = jnp.minimum(pos + 1, win).astype(F32)
        dlt = s[POOL_HALO:] / cnt - u[POOL_HALO:]
        yp = jnp.dot(dlt.astype(BF16), pw_ref[g], preferred_element_type=F32)
        y_ref[0, :, c0:c0 + POOL_GROUP_DIM] = (yp * ps_ref[:, c0:c0 + POOL_GROUP_DIM]).astype(BF16)

    z = ext(pool_dim, pool_dim + conv_dim) * ext(pool_dim + conv_dim, pool_dim + 2 * conv_dim)
    yc = z[POOL_HALO:] * cw_ref[CONV_WIDTH - 1:CONV_WIDTH, :]
    for tap in range(1, CONV_WIDTH):
        yc = yc + pltpu.roll(z, tap, axis=0)[POOL_HALO:] * cw_ref[CONV_WIDTH - 1 - tap:CONV_WIDTH - tap, :]
    gate_b = main_ref[:, pool_dim + 2 * conv_dim:pool_dim + 3 * conv_dim].astype(F32)
    y_ref[1] = (gate_b * yc).astype(BF16)

    for c0 in range(0, n_out, tn):
        mixed = acc_ref[:, c0:c0 + tn] + branch(0, y_ref[0], c0) + branch(1, y_ref[1], c0)
        o_ref[:, c0:c0 + tn] = mixed.astype(o_ref.dtype)


def _mixer(proj, gate_col, y_attn, w_branch, pool_w, pool_scale, conv_w, layer, seq, *, tm, tn):
    main = gates = proj
    tok = proj.shape[0]
    branch_dim = y_attn.shape[-1]
    width = 4 * branch_dim
    d = w_branch.shape[-1]
    assert gate_col % d == 0 and d % tn == 0
    gj = gate_col // d
    halo_blocks = tm // POOL_HALO
    resident = dict(pipeline_mode=pl.Buffered(1))
    return pl.pallas_call(
        functools.partial(_mixer_body, tm=tm, tn=tn, seq=seq, pool_dim=branch_dim, conv_dim=branch_dim),
        grid=(tok // tm,),
        in_specs=[
            pl.BlockSpec((tm, width), lambda i: (i, 0)),
            pl.BlockSpec((POOL_HALO, width), lambda i: (jnp.maximum(i * halo_blocks - 1, 0), 0)),
            pl.BlockSpec((tm, branch_dim), lambda i: (i, 0)),
            pl.BlockSpec((tm, d), lambda i: (i, gj)),
            pl.BlockSpec((tm, d), lambda i: (i, gj + 1)),
            pl.BlockSpec((tm, d), lambda i: (i, gj + 2)),
            pl.BlockSpec((None, 3, branch_dim, d), lambda i: (layer, 0, 0, 0), **resident),
            pl.BlockSpec((None,) + pool_w.shape[1:], lambda i: (layer, 0, 0, 0), **resident),
            pl.BlockSpec((None, 1, branch_dim), lambda i: (layer, 0, 0)),
            pl.BlockSpec((None, CONV_WIDTH, branch_dim), lambda i: (layer, 0, 0)),
        ],
        out_specs=pl.BlockSpec((tm, d), lambda i: (i, 0)),
        out_shape=jax.ShapeDtypeStruct((tok, d), BF16),
        scratch_shapes=[pltpu.VMEM((2, tm, branch_dim), BF16), pltpu.VMEM((tm, d), F32)],
        compiler_params=_params("parallel"),
        name="mixer_branches",
    )(main, main, y_attn, gates, gates, gates, w_branch, pool_w, pool_scale, conv_w)


def _proj_norm_res_body(a_ref, w_ref, x_ref, g_ref, o_ref):
    half = a_ref.shape[0] // 2
    for rows in (slice(0, half), slice(half, 2 * half)):
        m = jnp.dot(a_ref[rows, :], w_ref[...], preferred_element_type=F32)
        o_ref[rows, :] = x_ref[rows, :] + _rms(m, g_ref[...])


def _proj_norm_res(a, w, layer, x, gain, *, tm):
    tok, d = x.shape
    k = a.shape[-1]
    return pl.pallas_call(
        _proj_norm_res_body,
        grid=(tok // tm,),
        in_specs=[
            pl.BlockSpec((tm, k), lambda i: (i, 0)),
            pl.BlockSpec((None, k, d), lambda i: (layer, 0, 0)),
            pl.BlockSpec((tm, d), lambda i: (i, 0)),
            pl.BlockSpec((1, d), lambda i: (0, 0)),
        ],
        out_specs=pl.BlockSpec((tm, d), lambda i: (i, 0)),
        out_shape=jax.ShapeDtypeStruct((tok, d), F32),
        compiler_params=_params("parallel"),
        name="out_proj",
    )(a, w, x, gain)


def _mlp_body(x_ref, gpre_ref, gpost_ref, wu_ref, wd_ref, o_ref, h_ref):
    j = pl.program_id(1)

    @pl.when(j == 0)
    def _():
        h_ref[...] = _rms(x_ref[...], gpre_ref[...]).astype(BF16)
        o_ref[...] = jnp.zeros(o_ref.shape, F32)

    hid = jnp.dot(h_ref[...], wu_ref[...], preferred_element_type=F32)
    hid = jnp.square(jnp.maximum(hid, 0.0)).astype(BF16)
    o_ref[...] += jnp.dot(hid, wd_ref[...], preferred_element_type=F32)

    @pl.when(j == pl.num_programs(1) - 1)
    def _():
        o_ref[...] = x_ref[...] + _rms(o_ref[...], gpost_ref[...])


def _mlp(x, gpre, gpost, w_up, w_down, layer, *, tm, th):
    tok, d = x.shape
    hidden = w_up.shape[-1]
    return pl.pallas_call(
        _mlp_body,
        grid=(tok // tm, hidden // th),
        in_specs=[
            pl.BlockSpec((tm, d), lambda i, j: (i, 0), pipeline_mode=pl.Buffered(1)),
            pl.BlockSpec((1, d), lambda i, j: (0, 0)),
            pl.BlockSpec((1, d), lambda i, j: (0, 0)),
            pl.BlockSpec((None, d, th), lambda i, j: (layer, 0, j)),
            pl.BlockSpec((None, th, d), lambda i, j: (layer, j, 0)),
        ],
        out_specs=pl.BlockSpec((tm, d), lambda i, j: (i, 0)),
        out_shape=jax.ShapeDtypeStruct((tok, d), F32),
        scratch_shapes=[pltpu.VMEM((tm, d), BF16)],
        compiler_params=_params("parallel", "arbitrary"),
        name="mlp",
    )(x, gpre, gpost, w_up, w_down)


def _tile(n, pref):
    t = min(n, pref)
    assert n % t == 0, (n, pref)
    return t


@jax.jit
def _forward(x, norm_gains, w_in, pool_w, pool_scale, conv_w, rel_bias, w_branch,
             w_gate, b_gate, w_out, w_up, w_down):
    batch, seq, d = x.shape
    depth = w_in.shape[0]
    tok = batch * seq
    branch_dim = d // 2
    attn_dim = N_HEADS * HEAD_DIM
    kv_dim = N_KV_HEADS * HEAD_DIM
    qi_dim = IDX_HEADS * IDX_DIM
    assert branch_dim == attn_dim == len(POOL_WINDOWS) * POOL_GROUP_DIM
    assert seq % ATT_BLOCK == 0 and ATT_BLOCK >= MAX_DISTANCE

    main_w = 4 * branch_dim
    n_in = main_w + attn_dim + 2 * kv_dim + qi_dim
    kw_w = IDX_DIM + IDX_HEADS
    assert w_in.shape[-1] == n_in + kw_w
    w_kw = jnp.pad(w_in[:, :, n_in:], ((0, 0), (0, 0), (0, IDX_PAD - kw_w))).astype(BF16)
    w_branch_b = w_branch.astype(BF16)
    w_out_b = w_out.astype(BF16)
    w_up_b = w_up.astype(BF16)
    w_down_b = w_down.astype(BF16)
    pool_w_b = pool_w.astype(BF16)

    gate_col = main_w
    q_col = main_w + 3 * d
    scale = jnp.concatenate([jnp.ones((q_col,), F32), jnp.full((attn_dim,), HEAD_DIM ** -0.5 * LOG2E, F32),
                             jnp.ones((n_in - main_w - attn_dim,), F32)])

    bias_tiles = _bias_tiles(rel_bias, ATT_BLOCK)

    tm = _tile(tok, 1024)
    tm_s = _tile(seq, 512)
    xf = x.reshape(tok, d)
    for l in range(depth):
        g = norm_gains[l]
        g0, g1, g2, g3 = (g[n:n + 1] for n in range(4))
        bias = jnp.concatenate([jnp.zeros((main_w,), F32), b_gate[l].reshape(3 * d), jnp.zeros((n_in - main_w,), F32)])
        proj, kiw = _projections(xf, g0, w_in, w_gate, w_kw, l, jnp.stack([scale, bias]), main_w, n_in,
                                 tm=tm, tn=512)
        y_attn = _attention(proj, kiw, bias_tiles, batch, seq, q_col)
        mixed = _mixer(proj, gate_col, y_attn, w_branch_b, pool_w_b, pool_scale.reshape(depth, 1, branch_dim),
                       conv_w, l, seq, tm=tm_s, tn=1024)
        xf = _proj_norm_res(mixed, w_out_b, l, xf, g1, tm=tm_s)
        xf = _mlp(xf, g2, g3, w_up_b, w_down_b, l, tm=tm, th=512)
    return xf.reshape(batch, seq, d)


def kernel(x, norm_gains, w_in, pool_w, pool_scale, conv_w, rel_bias, w_branch, w_gate, b_gate, w_out, w_up, w_down):
    return _forward(x, norm_gains, w_in, pool_w, pool_scale, conv_w, rel_bias, w_branch,
                    w_gate, b_gate, w_out, w_up, w_down)
```

```python
import functools
import math

import jax
import jax.numpy as jnp
import numpy as np
from jax import lax
from jax.experimental import pallas as pl
from jax.experimental.pallas import tpu as pltpu

F32 = jnp.float32
BF16 = jnp.bfloat16
I32 = jnp.int32

NORM_EPS = 1e-6
POOL_WINDOWS = (2, 4, 8, 16)
POOL_GROUP_DIM = 256
POOL_HALO = 16
CONV_WIDTH = 3
N_HEADS = 8
N_KV_HEADS = 2
GROUP = N_HEADS // N_KV_HEADS
HEAD_DIM = 128
IDX_HEADS = 16
IDX_DIM = 64
TOPK_MAX = 256
NUM_BUCKETS = 32
MAX_EXACT = NUM_BUCKETS // 2
MAX_DISTANCE = 128
ATT_BLOCK = 256
IDX_PAD = 128
KEY_BITS = 32
INT_MIN = -(2 ** 31)
INT_MAX = 2 ** 31 - 1
NEG = -1e30
LOG2E = math.log2(math.e)
ONES_ROWS = 16
VMEM_LIMIT = 56 * 1024 * 1024
MXU_WIDTH = 256
NORM_ROWS = 256


def _bucket_thresholds():
    n = np.arange(MAX_EXACT, 8 * MAX_DISTANCE, dtype=np.int64)
    nf = n.astype(np.float32)
    large = MAX_EXACT + (np.log(nf / np.float32(MAX_EXACT)) / np.float32(math.log(MAX_DISTANCE / MAX_EXACT))
                         * np.float32(NUM_BUCKETS - MAX_EXACT)).astype(np.int32)
    large = np.minimum(large, NUM_BUCKETS - 1)
    thr = []
    for b in range(MAX_EXACT + 1, NUM_BUCKETS):
        thr.append(int(n[np.argmax(large >= b)]))
    return tuple(thr)


BUCKET_THRESHOLDS = _bucket_thresholds()


def _params(*sem):
    return pltpu.CompilerParams(dimension_semantics=sem, vmem_limit_bytes=VMEM_LIMIT)


def _rms(x, gain):
    ms = jnp.mean(x * x, axis=-1, keepdims=True)
    return x * lax.rsqrt(ms + NORM_EPS) * gain


def _dot_nt(a, b):
    return lax.dot_general(a, b, (((1,), (1,)), ((), ())), preferred_element_type=F32)


def _proj_body(x_ref, g_ref, wint_ref, wg_ref, wkwt_ref, post_ref, o_ref, okw_ref, h_ref, *, gate_lo, gate_hi):
    j = pl.program_id(1)
    is_gate = jnp.logical_and(j >= gate_lo, j < gate_hi)

    @pl.when(j == 0)
    def _():
        def norm_rows(r, carry):
            rows = pl.ds(pl.multiple_of(r * NORM_ROWS, NORM_ROWS), NORM_ROWS)
            h = _rms(x_ref[rows, :], g_ref[...]).astype(BF16)
            h_ref[rows, :] = h
            okw_ref[rows, :] = _dot_nt(h, wkwt_ref[...])
            return carry

        lax.fori_loop(0, x_ref.shape[0] // NORM_ROWS, norm_rows, 0)

    chunks = [slice(c, c + MXU_WIDTH) for c in range(0, o_ref.shape[-1], MXU_WIDTH)]

    @pl.when(jnp.logical_not(is_gate))
    def _():
        for cs in chunks:
            y = _dot_nt(h_ref[...], wint_ref[cs, :].astype(BF16))
            o_ref[:, cs] = (y * post_ref[0:1, cs]).astype(o_ref.dtype)

    @pl.when(is_gate)
    def _():
        for cs in chunks:
            y = jnp.dot(h_ref[...], wg_ref[:, cs].astype(BF16), preferred_element_type=F32) + post_ref[1:2, cs]
            o_ref[:, cs] = (0.5 * jnp.tanh(0.5 * y) + 0.5).astype(o_ref.dtype)


def _projections(x, gain, w_in_t, w_gate, w_kw_t, layer, post, n_main, n_in, *, tm, tn):
    tok, d = x.shape
    n_gate = w_gate.shape[-1]
    assert n_main % tn == 0 and n_in % tn == 0 and n_gate % tn == 0 and tm % NORM_ROWS == 0
    gate_lo = n_main // tn
    gate_blocks = n_gate // tn
    gate_hi = gate_lo + gate_blocks
    kw = w_kw_t.shape[1]
    return pl.pallas_call(
        functools.partial(_proj_body, gate_lo=gate_lo, gate_hi=gate_hi),
        grid=(tok // tm, (n_in + n_gate) // tn),
        in_specs=[
            pl.BlockSpec((tm, d), lambda i, j: (i, 0), pipeline_mode=pl.Buffered(1)),
            pl.BlockSpec((1, d), lambda i, j: (0, 0)),
            pl.BlockSpec((None, tn, d),
                         lambda i, j: (layer, jnp.where(j < gate_lo, j, jnp.maximum(j - gate_blocks, gate_lo)), 0)),
            pl.BlockSpec((None, d, tn), lambda i, j: (layer, 0, jnp.clip(j - gate_lo, 0, gate_blocks - 1))),
            pl.BlockSpec((None, kw, d), lambda i, j: (layer, 0, 0)),
            pl.BlockSpec((2, tn), lambda i, j: (0, j)),
        ],
        out_specs=[
            pl.BlockSpec((tm, tn), lambda i, j: (i, j)),
            pl.BlockSpec((tm, kw), lambda i, j: (i, 0)),
        ],
        out_shape=[
            jax.ShapeDtypeStruct((tok, n_in + n_gate), BF16),
            jax.ShapeDtypeStruct((tok, kw), F32),
        ],
        scratch_shapes=[pltpu.VMEM((tm, d), BF16)],
        compiler_params=_params("parallel", "arbitrary"),
        name="projections",
    )(x, gain, w_in_t, w_gate, w_kw_t, post)


def _bias_tiles_body(rel_ref, o_ref, *, t):
    row = lax.broadcasted_iota(I32, (t, t), 0)
    col = lax.broadcasted_iota(I32, (t, t), 1)
    for off in range(2):
        n = jnp.maximum(col - row + off * t, 0)
        large = jnp.full((t, t), MAX_EXACT, I32)
        for thr in BUCKET_THRESHOLDS:
            large = large + jnp.where(n >= thr, 1, 0)
        bucket = jnp.where(n < MAX_EXACT, n, large)
        for h in range(N_HEADS):
            val = jnp.zeros((t, t), F32)
            for b in range(NUM_BUCKETS):
                val = jnp.where(bucket == b, rel_ref[b, h], val)
            o_ref[h, off] = (val - rel_ref[NUM_BUCKETS - 1, h]) * LOG2E


def _bias_tiles(rel_bias, t):
    return pl.pallas_call(
        functools.partial(_bias_tiles_body, t=t),
        in_specs=[pl.BlockSpec(memory_space=pltpu.SMEM)],
        out_specs=pl.BlockSpec(memory_space=pltpu.VMEM),
        out_shape=jax.ShapeDtypeStruct((N_HEADS, 2, t, t), F32),
        compiler_params=pltpu.CompilerParams(vmem_limit_bytes=VMEM_LIMIT),
        name="bias_tiles",
    )(rel_bias)


def _attn_body(q_ref, k_ref, v_ref, qia_ref, qib_ref, kiw_ref, bias_ref, o_ref,
               sc_ref, thr_ref, vt_ref, kib_ref, qih_ref, mask_ref, alpha_ref, m_ref, acc_ref, *head_refs,
               t, topk):
    s_ref, p_ref = head_refs[:N_HEADS], head_refs[N_HEADS:]
    i = pl.program_id(1)
    nkb = sc_ref.shape[0]
    q0 = pl.multiple_of(i * t, t)
    key_pos = lax.broadcasted_iota(I32, (t, 1), 0)
    qry_pos = lax.broadcasted_iota(I32, (1, t), 1)
    causal = key_pos <= qry_pos

    @pl.when(i == 0)
    def _():
        for kb in range(nkb):
            rows = slice(kb * t, (kb + 1) * t)
            vb = v_ref[rows, :].astype(F32)
            for g in range(N_KV_HEADS):
                vt_ref[kb, g, 0:HEAD_DIM, :] = vb[:, g * HEAD_DIM:(g + 1) * HEAD_DIM].T.astype(BF16)
                vt_ref[kb, g, HEAD_DIM:, :] = jnp.ones((vt_ref.shape[2] - HEAD_DIM, t), BF16)
            kib_ref[rows, :] = kiw_ref[rows, 0:IDX_DIM].astype(BF16)

    half = IDX_HEADS // 2
    for h in range(IDX_HEADS):
        src = qia_ref if h < half else qib_ref
        qih_ref[h] = src[:, (h % half) * IDX_DIM:(h % half + 1) * IDX_DIM]
    w_t = kiw_ref[pl.ds(q0, t), :].T[IDX_DIM:IDX_DIM + IDX_HEADS, :] * (IDX_HEADS ** -0.5)

    def score_block(kb, diag):
        kblk = kib_ref[pl.ds(pl.multiple_of(kb * t, t), t), :]
        acc = jnp.zeros((t, t), F32)
        for h in range(IDX_HEADS):
            acc = acc + w_t[h:h + 1, :] * jnp.maximum(_dot_nt(kblk, qih_ref[h]), 0.0)
        if diag:
            acc = jnp.where(causal, acc, -jnp.inf)
        bits = pltpu.bitcast(acc, I32)
        sc_ref[kb] = bits ^ ((bits >> 31) & INT_MAX)

    def score_loop(kb, carry):
        score_block(kb, False)
        return carry

    lax.fori_loop(0, i, score_loop, 0)
    score_block(i, True)

    @pl.when(q0 + t > topk)
    def _():
        needs = q0 + qry_pos >= topk

        def bisect(_, state):
            lo, hi, cnt_lo, cnt_hi = state
            mid = (lo & hi) + ((lo ^ hi) >> 1)

            def count(kb, c):
                ge = jnp.where(sc_ref[kb] >= mid, 1, 0)
                return c + jnp.sum(ge.reshape(t // 8, 8, t), axis=0)

            c = jnp.sum(lax.fori_loop(0, i + 1, count, jnp.zeros((8, t), I32)), axis=0, keepdims=True)
            enough = c >= topk
            return (jnp.where(enough, mid, lo), jnp.where(enough, hi, mid),
                    jnp.where(enough, c, cnt_lo), jnp.where(enough, cnt_hi, c))

        lo, _, cnt_lo, cnt_hi = lax.fori_loop(
            0, KEY_BITS, bisect,
            (jnp.full((1, t), INT_MIN, I32), jnp.full((1, t), INT_MAX, I32),
             jnp.full((1, t), topk + 1, I32), jnp.zeros((1, t), I32)))
        thr_ref[...] = jnp.where(needs, lo, INT_MIN)

        tied = jnp.where(jnp.logical_and(needs, cnt_lo > topk), 1, 0)

        @pl.when(jnp.max(tied) > 0)
        def _():
            wanted = (topk - cnt_hi).astype(F32)
            tied_f = tied.astype(F32)
            earlier = jnp.where(lax.broadcasted_iota(I32, (t, t), 1) < lax.broadcasted_iota(I32, (t, t), 0),
                                1.0, 0.0).astype(BF16)

            def demote(kb, seen):
                key = sc_ref[kb]
                is_tie = jnp.where(key == lo, tied_f, 0.0)
                rank = seen + jnp.dot(earlier, is_tie.astype(BF16), preferred_element_type=F32)
                drop = is_tie * jnp.where(rank >= wanted, 1.0, 0.0)
                sc_ref[kb] = jnp.where(drop > 0.0, lo - 1, key)
                return seen + jnp.sum(is_tie, axis=0, keepdims=True)

            lax.fori_loop(0, i + 1, demote, jnp.zeros((1, t), F32))

    @pl.when(q0 + t <= topk)
    def _():
        thr_ref[...] = jnp.full((1, t), INT_MIN, I32)

    m_ref[...] = jnp.full(m_ref.shape, NEG, F32)
    acc_ref[...] = jnp.zeros(acc_ref.shape, F32)
    thr = thr_ref[...]

    def attn_block(kb, mode):
        ks = pl.ds(pl.multiple_of(kb * t, t), t)
        mask = jnp.where(sc_ref[kb] >= thr, 0.0, NEG)
        if mode == 0:
            mask = jnp.where(causal, mask, NEG)
        mask_ref[...] = mask

        def logits(h):
            g = h // GROUP
            s_ref[h][...] = _dot_nt(k_ref[ks, g * HEAD_DIM:(g + 1) * HEAD_DIM],
                                    q_ref[:, h * HEAD_DIM:(h + 1) * HEAD_DIM])

        def numerators(h):
            s = s_ref[h][...] + mask_ref[...]
            if mode != "far":
                s = s + bias_ref[h, mode]
            m_old = m_ref[h]
            m_new = jnp.maximum(m_old, jnp.max(s, axis=0, keepdims=True))
            p_ref[h][...] = jnp.exp2(s - m_new).astype(BF16)
            alpha_ref[h] = jnp.exp2(m_old - m_new)
            m_ref[h] = m_new

        def weighted_values(h):
            pv = jnp.dot(vt_ref[kb, h // GROUP], p_ref[h][...], preferred_element_type=F32)
            acc_ref[h] = alpha_ref[h] * acc_ref[h] + pv

        for stage in (logits, numerators, weighted_values):
            for h in range(N_HEADS):
                stage(h)

    def far_loop(kb, carry):
        attn_block(kb, "far")
        return carry

    lax.fori_loop(0, jnp.maximum(i - 1, 0), far_loop, 0)

    @pl.when(i >= 1)
    def _():
        attn_block(i - 1, 1)

    attn_block(i, 0)

    for h in range(N_HEADS):
        acc = acc_ref[h]
        out_t = acc[0:HEAD_DIM, :] / acc[HEAD_DIM:HEAD_DIM + 1, :]
        o_ref[:, h * HEAD_DIM:(h + 1) * HEAD_DIM] = out_t.T.astype(o_ref.dtype)


def _attention(proj, kiw, bias_tiles, batch, seq, q_col):
    t = ATT_BLOCK
    nq = seq // t
    topk = min(TOPK_MAX, seq // 4)
    attn_dim = N_HEADS * HEAD_DIM
    kv_dim = N_KV_HEADS * HEAD_DIM
    qi_half = IDX_HEADS * IDX_DIM // 2
    k_col = q_col + attn_dim
    qi_col = k_col + 2 * kv_dim
    assert q_col % attn_dim == 0 and k_col % kv_dim == 0 and qi_col % qi_half == 0
    return pl.pallas_call(
        functools.partial(_attn_body, t=t, topk=topk),
        grid=(batch, nq),
        in_specs=[
            pl.BlockSpec((t, attn_dim), lambda b, i: (b * nq + i, q_col // attn_dim)),
            pl.BlockSpec((seq, kv_dim), lambda b, i: (b, k_col // kv_dim)),
            pl.BlockSpec((seq, kv_dim), lambda b, i: (b, k_col // kv_dim + 1)),
            pl.BlockSpec((t, qi_half), lambda b, i: (b * nq + i, qi_col // qi_half)),
            pl.BlockSpec((t, qi_half), lambda b, i: (b * nq + i, qi_col // qi_half + 1)),
            pl.BlockSpec((seq, IDX_PAD), lambda b, i: (b, 0)),
            pl.BlockSpec((N_HEADS, 2, t, t), lambda b, i: (0, 0, 0, 0)),
        ],
        out_specs=pl.BlockSpec((t, attn_dim), lambda b, i: (b * nq + i, 0)),
        out_shape=jax.ShapeDtypeStruct((batch * seq, attn_dim), BF16),
        scratch_shapes=[
            pltpu.VMEM((nq, t, t), I32),
            pltpu.VMEM((1, t), I32),
            pltpu.VMEM((nq, N_KV_HEADS, HEAD_DIM + ONES_ROWS, t), BF16),
            pltpu.VMEM((seq, IDX_DIM), BF16),
            pltpu.VMEM((IDX_HEADS, t, IDX_DIM), BF16),
            pltpu.VMEM((t, t), F32),
            pltpu.VMEM((N_HEADS, 1, t), F32),
            pltpu.VMEM((N_HEADS, 1, t), F32),
            pltpu.VMEM((N_HEADS, HEAD_DIM + ONES_ROWS, t), F32),
        ] + [pltpu.VMEM((t, t), F32)] * N_HEADS + [pltpu.VMEM((t, t), BF16)] * N_HEADS,
        compiler_params=_params("parallel", "arbitrary"),
        name="sparse_attention",
    )(proj, proj, proj, proj, proj, kiw, bias_tiles)


def _mixer_body(main_ref, halo_ref, ya_ref, g0_ref, g1_ref, g2_ref, wb_ref, pw_ref, ps_ref, cw_ref,
                o_ref, y_ref, acc_ref, *, tm, tn, seq, pool_dim, conv_dim):
    pos0 = (pl.program_id(0) * tm) % seq
    pos = lax.broadcasted_iota(I32, (tm, 1), 0) + pos0
    keep = jnp.where(pos0 == 0, 0.0, 1.0)
    n_out = o_ref.shape[-1]

    def branch(n, y, c0):
        return g_refs[n][:, c0:c0 + tn].astype(F32) * jnp.dot(y, wb_ref[n, :, c0:c0 + tn], preferred_element_type=F32)

    g_refs = (g0_ref, g1_ref, g2_ref)
    for c0 in range(0, n_out, tn):
        acc_ref[:, c0:c0 + tn] = branch(2, ya_ref[...], c0)

    def ext(c0, c1):
        halo = halo_ref[:, c0:c1].astype(F32) * keep
        return jnp.concatenate([halo, main_ref[:, c0:c1].astype(F32)], axis=0)

    for g, win in enumerate(POOL_WINDOWS):
        c0 = g * POOL_GROUP_DIM
        u = ext(c0, c0 + POOL_GROUP_DIM)
        s, sh = u, 1
        while sh < win:
            s = s + pltpu.roll(s, sh, axis=0)
            sh *= 2
        cnt = jnp.minimum(pos + 1, win).astype(F32)
        dlt = s[POOL_HALO:] / cnt - u[POOL_HALO:]
        yp = jnp.dot(dlt.astype(BF16), pw_ref[g], preferred_element_type=F32)
        y_ref[0, :, c0:c0 + POOL_GROUP_DIM] = (yp * ps_ref[:, c0:c0 + POOL_GROUP_DIM]).astype(BF16)

    z = ext(pool_dim, pool_dim + conv_dim) * ext(pool_dim + conv_dim, pool_dim + 2 * conv_dim)
    yc = z[POOL_HALO:] * cw_ref[CONV_WIDTH - 1:CONV_WIDTH, :]
    for tap in range(1, CONV_WIDTH):
        yc = yc + pltpu.roll(z, tap, axis=0)[POOL_HALO:] * cw_ref[CONV_WIDTH - 1 - tap:CONV_WIDTH - tap, :]
    gate_b = main_ref[:, pool_dim + 2 * conv_dim:pool_dim + 3 * conv_dim].astype(F32)
    y_ref[1] = (gate_b * yc).astype(BF16)

    for c0 in range(0, n_out, tn):
        mixed = acc_ref[:, c0:c0 + tn] + branch(0, y_ref[0], c0) + branch(1, y_ref[1], c0)
        o_ref[:, c0:c0 + tn] = mixed.astype(o_ref.dtype)


def _mixer(proj, gate_col, y_attn, w_branch, pool_w, pool_scale, conv_w, layer, seq, *, tm, tn):
    main = gates = proj
    tok = proj.shape[0]
    branch_dim = y_attn.shape[-1]
    width = 4 * branch_dim
    d = w_branch.shape[-1]
    assert gate_col % d == 0 and d % tn == 0
    gj = gate_col // d
    halo_blocks = tm // POOL_HALO
    resident = dict(pipeline_mode=pl.Buffered(1))
    return pl.pallas_call(
        functools.partial(_mixer_body, tm=tm, tn=tn, seq=seq, pool_dim=branch_dim, conv_dim=branch_dim),
        grid=(tok // tm,),
        in_specs=[
            pl.BlockSpec((tm, width), lambda i: (i, 0)),
            pl.BlockSpec((POOL_HALO, width), lambda i: (jnp.maximum(i * halo_blocks - 1, 0), 0)),
            pl.BlockSpec((tm, branch_dim), lambda i: (i, 0)),
            pl.BlockSpec((tm, d), lambda i: (i, gj)),
            pl.BlockSpec((tm, d), lambda i: (i, gj + 1)),
            pl.BlockSpec((tm, d), lambda i: (i, gj + 2)),
            pl.BlockSpec((None, 3, branch_dim, d), lambda i: (layer, 0, 0, 0), **resident),
            pl.BlockSpec((None,) + pool_w.shape[1:], lambda i: (layer, 0, 0, 0), **resident),
            pl.BlockSpec((None, 1, branch_dim), lambda i: (layer, 0, 0)),
            pl.BlockSpec((None, CONV_WIDTH, branch_dim), lambda i: (layer, 0, 0)),
        ],
        out_specs=pl.BlockSpec((tm, d), lambda i: (i, 0)),
        out_shape=jax.ShapeDtypeStruct((tok, d), BF16),
        scratch_shapes=[pltpu.VMEM((2, tm, branch_dim), BF16), pltpu.VMEM((tm, d), F32)],
        compiler_params=_params("parallel"),
        name="mixer_branches",
    )(main, main, y_attn, gates, gates, gates, w_branch, pool_w, pool_scale, conv_w)


def _proj_norm_res_body(a_ref, w_ref, x_ref, g_ref, o_ref):
    half = a_ref.shape[0] // 2
    for rows in (slice(0, half), slice(half, 2 * half)):
        m = jnp.dot(a_ref[rows, :], w_ref[...], preferred_element_type=F32)
        o_ref[rows, :] = x_ref[rows, :] + _rms(m, g_ref[...])


def _proj_norm_res(a, w, layer, x, gain, *, tm):
    tok, d = x.shape
    k = a.shape[-1]
    return pl.pallas_call(
        _proj_norm_res_body,
        grid=(tok // tm,),
        in_specs=[
            pl.BlockSpec((tm, k), lambda i: (i, 0)),
            pl.BlockSpec((None, k, d), lambda i: (layer, 0, 0)),
            pl.BlockSpec((tm, d), lambda i: (i, 0)),
            pl.BlockSpec((1, d), lambda i: (0, 0)),
        ],
        out_specs=pl.BlockSpec((tm, d), lambda i: (i, 0)),
        out_shape=jax.ShapeDtypeStruct((tok, d), F32),
        compiler_params=_params("parallel"),
        name="out_proj",
    )(a, w, x, gain)


def _mlp_body(x_ref, gpre_ref, gpost_ref, wu_ref, wd_ref, o_ref, h_ref, acc_ref):
    j = pl.program_id(1)

    @pl.when(j == 0)
    def _():
        h_ref[...] = _rms(x_ref[...], gpre_ref[...]).astype(BF16)
        acc_ref[...] = jnp.zeros(acc_ref.shape, F32)

    hid = jnp.dot(h_ref[...], wu_ref[...], preferred_element_type=F32)
    hid = jnp.square(jnp.maximum(hid, 0.0)).astype(BF16)
    acc_ref[...] += jnp.dot(hid, wd_ref[...], preferred_element_type=F32)

    @pl.when(j == pl.num_programs(1) - 1)
    def _():
        o_ref[...] = x_ref[...] + _rms(acc_ref[...], gpost_ref[...])


def _mlp(x, gpre, gpost, w_up, w_down, layer, *, tm, th):
    tok, d = x.shape
    hidden = w_up.shape[-1]
    return pl.pallas_call(
        _mlp_body,
        grid=(tok // tm, hidden // th),
        in_specs=[
            pl.BlockSpec((tm, d), lambda i, j: (i, 0)),
            pl.BlockSpec((1, d), lambda i, j: (0, 0)),
            pl.BlockSpec((1, d), lambda i, j: (0, 0)),
            pl.BlockSpec((None, d, th), lambda i, j: (layer, 0, j)),
            pl.BlockSpec((None, th, d), lambda i, j: (layer, j, 0)),
        ],
        out_specs=pl.BlockSpec((tm, d), lambda i, j: (i, 0)),
        out_shape=jax.ShapeDtypeStruct((tok, d), F32),
        scratch_shapes=[pltpu.VMEM((tm, d), BF16), pltpu.VMEM((tm, d), F32)],
        compiler_params=_params("parallel", "arbitrary"),
        name="mlp",
    )(x, gpre, gpost, w_up, w_down)


def _tile(n, pref):
    t = min(n, pref)
    assert n % t == 0, (n, pref)
    return t


@jax.jit
def _forward(x, norm_gains, w_in, pool_w, pool_scale, conv_w, rel_bias, w_branch,
             w_gate, b_gate, w_out, w_up, w_down):
    batch, seq, d = x.shape
    depth = w_in.shape[0]
    tok = batch * seq
    branch_dim = d // 2
    attn_dim = N_HEADS * HEAD_DIM
    kv_dim = N_KV_HEADS * HEAD_DIM
    qi_dim = IDX_HEADS * IDX_DIM
    assert branch_dim == attn_dim == len(POOL_WINDOWS) * POOL_GROUP_DIM
    assert seq % ATT_BLOCK == 0 and ATT_BLOCK >= MAX_DISTANCE

    main_w = 4 * branch_dim
    n_in = main_w + attn_dim + 2 * kv_dim + qi_dim
    kw_w = IDX_DIM + IDX_HEADS
    assert w_in.shape[-1] == n_in + kw_w
    w_in_t = jnp.swapaxes(w_in, 1, 2)
    w_kw_t = jnp.pad(w_in_t[:, n_in:, :], ((0, 0), (0, IDX_PAD - kw_w), (0, 0))).astype(BF16)
    w_branch_b = w_branch.astype(BF16)
    w_out_b = w_out.astype(BF16)
    w_up_b = w_up.astype(BF16)
    w_down_b = w_down.astype(BF16)
    pool_w_b = pool_w.astype(BF16)

    gate_col = main_w
    q_col = main_w + 3 * d
    scale = jnp.concatenate([jnp.ones((q_col,), F32), jnp.full((attn_dim,), HEAD_DIM ** -0.5 * LOG2E, F32),
                             jnp.ones((n_in - main_w - attn_dim,), F32)])

    bias_tiles = _bias_tiles(rel_bias, ATT_BLOCK)

    tm = _tile(tok, 1024)
    tm_s = _tile(seq, 512)
    xf = x.reshape(tok, d)
    for l in range(depth):
        g = norm_gains[l]
        g0, g1, g2, g3 = (g[n:n + 1] for n in range(4))
        bias = jnp.concatenate([jnp.zeros((main_w,), F32), b_gate[l].reshape(3 * d), jnp.zeros((n_in - main_w,), F32)])
        proj, kiw = _projections(xf, g0, w_in_t, w_gate, w_kw_t, l, jnp.stack([scale, bias]), main_w, n_in,
                                 tm=_tile(tok, 2048), tn=512)
        y_attn = _attention(proj, kiw, bias_tiles, batch, seq, q_col)
        mixed = _mixer(proj, gate_col, y_attn, w_branch_b, pool_w_b, pool_scale.reshape(depth, 1, branch_dim),
                       conv_w, l, seq, tm=tm_s, tn=1024)
        xf = _proj_norm_res(mixed, w_out_b, l, xf, g1, tm=tm_s)
        xf = _mlp(xf, g2, g3, w_up_b, w_down_b, l, tm=tm_s, th=1024)
    return xf.reshape(batch, seq, d)


def kernel(x, norm_gains, w_in, pool_w, pool_scale, conv_w, rel_bias, w_branch, w_gate, b_gate, w_out, w_up, w_down):
    return _forward(x, norm_gains, w_in, pool_w, pool_scale, conv_w, rel_bias, w_branch,
                    w_gate, b_gate, w_out, w_up, w_down)
```

```python
import functools
import math

import jax
import jax.numpy as jnp
import numpy as np
from jax import lax
from jax.experimental import pallas as pl
from jax.experimental.pallas import tpu as pltpu

F32 = jnp.float32
BF16 = jnp.bfloat16
I32 = jnp.int32

NORM_EPS = 1e-6
POOL_WINDOWS = (2, 4, 8, 16)
POOL_GROUP_DIM = 256
POOL_HALO = 16
CONV_WIDTH = 3
N_HEADS = 8
N_KV_HEADS = 2
GROUP = N_HEADS // N_KV_HEADS
HEAD_DIM = 128
IDX_HEADS = 16
IDX_DIM = 64
TOPK_MAX = 256
NUM_BUCKETS = 32
MAX_EXACT = NUM_BUCKETS // 2
MAX_DISTANCE = 128
ATT_BLOCK = 256
IDX_PAD = 128
KEY_BITS = 32
INT_MIN = -(2 ** 31)
INT_MAX = 2 ** 31 - 1
NEG = -1e30
LOG2E = math.log2(math.e)
ONES_ROWS = 16
VMEM_LIMIT = 56 * 1024 * 1024
MXU_WIDTH = 256
BF16_SUBLANES = 16
NORM_ROWS = 256


def _bucket_thresholds():
    n = np.arange(MAX_EXACT, 8 * MAX_DISTANCE, dtype=np.int64)
    nf = n.astype(np.float32)
    large = MAX_EXACT + (np.log(nf / np.float32(MAX_EXACT)) / np.float32(math.log(MAX_DISTANCE / MAX_EXACT))
                         * np.float32(NUM_BUCKETS - MAX_EXACT)).astype(np.int32)
    large = np.minimum(large, NUM_BUCKETS - 1)
    thr = []
    for b in range(MAX_EXACT + 1, NUM_BUCKETS):
        thr.append(int(n[np.argmax(large >= b)]))
    return tuple(thr)


BUCKET_THRESHOLDS = _bucket_thresholds()


def _params(*sem):
    return pltpu.CompilerParams(dimension_semantics=sem, vmem_limit_bytes=VMEM_LIMIT)


def _rms(x, gain):
    ms = jnp.mean(x * x, axis=-1, keepdims=True)
    return x * lax.rsqrt(ms + NORM_EPS) * gain


def _dot_nt(a, b):
    return lax.dot_general(a, b, (((1,), (1,)), ((), ())), preferred_element_type=F32)


def _proj_body(x_ref, g_ref, wint_ref, wg_ref, wkwt_ref, post_ref, o_ref, okw_ref, h_ref, *, gate_lo, gate_hi):
    j = pl.program_id(1)
    is_gate = jnp.logical_and(j >= gate_lo, j < gate_hi)

    @pl.when(j == 0)
    def _():
        def norm_rows(r, carry):
            rows = pl.ds(pl.multiple_of(r * NORM_ROWS, NORM_ROWS), NORM_ROWS)
            h = _rms(x_ref[rows, :], g_ref[...]).astype(BF16)
            h_ref[rows, :] = h
            okw_ref[rows, :] = _dot_nt(h, wkwt_ref[...])
            return carry

        lax.fori_loop(0, x_ref.shape[0] // NORM_ROWS, norm_rows, 0)

    chunks = [slice(c, c + MXU_WIDTH) for c in range(0, o_ref.shape[-1], MXU_WIDTH)]

    @pl.when(jnp.logical_not(is_gate))
    def _():
        for cs in chunks:
            y = _dot_nt(h_ref[...], wint_ref[cs, :].astype(BF16))
            o_ref[:, cs] = (y * post_ref[0:1, cs]).astype(o_ref.dtype)

    @pl.when(is_gate)
    def _():
        for cs in chunks:
            y = jnp.dot(h_ref[...], wg_ref[:, cs].astype(BF16), preferred_element_type=F32) + post_ref[1:2, cs]
            o_ref[:, cs] = (0.5 * jnp.tanh(0.5 * y) + 0.5).astype(o_ref.dtype)


def _projections(x, gain, w_in_t, w_gate, w_kw_t, layer, post, n_main, n_in, *, tm, tn):
    tok, d = x.shape
    n_gate = w_gate.shape[-1]
    assert n_main % tn == 0 and n_in % tn == 0 and n_gate % tn == 0 and tm % NORM_ROWS == 0
    gate_lo = n_main // tn
    gate_blocks = n_gate // tn
    gate_hi = gate_lo + gate_blocks
    kw = w_kw_t.shape[1]
    return pl.pallas_call(
        functools.partial(_proj_body, gate_lo=gate_lo, gate_hi=gate_hi),
        grid=(tok // tm, (n_in + n_gate) // tn),
        in_specs=[
            pl.BlockSpec((tm, d), lambda i, j: (i, 0), pipeline_mode=pl.Buffered(1)),
            pl.BlockSpec((1, d), lambda i, j: (0, 0)),
            pl.BlockSpec((None, tn, d),
                         lambda i, j: (layer, jnp.where(j < gate_lo, j, jnp.maximum(j - gate_blocks, gate_lo)), 0)),
            pl.BlockSpec((None, d, tn), lambda i, j: (layer, 0, jnp.clip(j - gate_lo, 0, gate_blocks - 1))),
            pl.BlockSpec((None, kw, d), lambda i, j: (layer, 0, 0)),
            pl.BlockSpec((2, tn), lambda i, j: (0, j)),
        ],
        out_specs=[
            pl.BlockSpec((tm, tn), lambda i, j: (i, j)),
            pl.BlockSpec((tm, kw), lambda i, j: (i, 0)),
        ],
        out_shape=[
            jax.ShapeDtypeStruct((tok, n_in + n_gate), BF16),
            jax.ShapeDtypeStruct((tok, kw), F32),
        ],
        scratch_shapes=[pltpu.VMEM((tm, d), BF16)],
        compiler_params=_params("parallel", "arbitrary"),
        name="projections",
    )(x, gain, w_in_t, w_gate, w_kw_t, post)


def _bias_tiles_body(rel_ref, o_ref, *, t):
    row = lax.broadcasted_iota(I32, (t, t), 0)
    col = lax.broadcasted_iota(I32, (t, t), 1)
    for off in range(2):
        n = jnp.maximum(col - row + off * t, 0)
        large = jnp.full((t, t), MAX_EXACT, I32)
        for thr in BUCKET_THRESHOLDS:
            large = large + jnp.where(n >= thr, 1, 0)
        bucket = jnp.where(n < MAX_EXACT, n, large)
        for h in range(N_HEADS):
            val = jnp.zeros((t, t), F32)
            for b in range(NUM_BUCKETS):
                val = jnp.where(bucket == b, rel_ref[b, h], val)
            o_ref[h, off] = (val - rel_ref[NUM_BUCKETS - 1, h]) * LOG2E


def _bias_tiles(rel_bias, t):
    return pl.pallas_call(
        functools.partial(_bias_tiles_body, t=t),
        in_specs=[pl.BlockSpec(memory_space=pltpu.SMEM)],
        out_specs=pl.BlockSpec(memory_space=pltpu.VMEM),
        out_shape=jax.ShapeDtypeStruct((N_HEADS, 2, t, t), F32),
        compiler_params=pltpu.CompilerParams(vmem_limit_bytes=VMEM_LIMIT),
        name="bias_tiles",
    )(rel_bias)


def _attn_body(*refs, t, topk, n_casts):
    (q_ref, k_ref, v_ref, qia_ref, qib_ref, kiw_ref, bias_ref), refs = refs[:7], refs[7:]
    cast_src, refs = refs[:n_casts], refs[n_casts:]
    o_ref, cast_dst, refs = refs[0], refs[1:1 + n_casts], refs[1 + n_casts:]
    (sc_ref, thr_ref, vt_ref, kib_ref, qih_ref, mask_ref, alpha_ref, m_ref, acc_ref), refs = refs[:9], refs[9:]
    s_ref, p_ref = refs[:N_HEADS], refs[N_HEADS:]

    for src, dst in zip(cast_src, cast_dst):
        dst[...] = src[...].astype(BF16)

    i = pl.program_id(1)
    nkb = sc_ref.shape[0]
    q0 = pl.multiple_of(i * t, t)
    key_pos = lax.broadcasted_iota(I32, (t, 1), 0)
    qry_pos = lax.broadcasted_iota(I32, (1, t), 1)
    causal = key_pos <= qry_pos

    @pl.when(i == 0)
    def _():
        for kb in range(nkb):
            rows = slice(kb * t, (kb + 1) * t)
            vb = v_ref[rows, :].astype(F32)
            for g in range(N_KV_HEADS):
                vt_ref[kb, g, 0:HEAD_DIM, :] = vb[:, g * HEAD_DIM:(g + 1) * HEAD_DIM].T.astype(BF16)
                vt_ref[kb, g, HEAD_DIM:, :] = jnp.ones((vt_ref.shape[2] - HEAD_DIM, t), BF16)
            kib_ref[rows, :] = kiw_ref[rows, 0:IDX_DIM].astype(BF16)

    half = IDX_HEADS // 2
    for h in range(IDX_HEADS):
        src = qia_ref if h < half else qib_ref
        qih_ref[h] = src[:, (h % half) * IDX_DIM:(h % half + 1) * IDX_DIM]
    w_t = kiw_ref[pl.ds(q0, t), :].T[IDX_DIM:IDX_DIM + IDX_HEADS, :] * (IDX_HEADS ** -0.5)

    def score_block(kb, diag):
        kblk = kib_ref[pl.ds(pl.multiple_of(kb * t, t), t), :]
        acc = jnp.zeros((t, t), F32)
        for h in range(IDX_HEADS):
            acc = acc + w_t[h:h + 1, :] * jnp.maximum(_dot_nt(kblk, qih_ref[h]), 0.0)
        if diag:
            acc = jnp.where(causal, acc, -jnp.inf)
        bits = pltpu.bitcast(acc, I32)
        sc_ref[kb] = bits ^ ((bits >> 31) & INT_MAX)

    def score_loop(kb, carry):
        score_block(kb, False)
        return carry

    lax.fori_loop(0, i, score_loop, 0)
    score_block(i, True)

    @pl.when(q0 + t > topk)
    def _():
        needs = q0 + qry_pos >= topk

        def bisect(_, state):
            lo, hi, cnt_lo, cnt_hi = state
            mid = (lo & hi) + ((lo ^ hi) >> 1)

            def count(kb, c):
                ge = jnp.where(sc_ref[kb] >= mid, 1, 0)
                return c + jnp.sum(ge.reshape(t // 8, 8, t), axis=0)

            c = jnp.sum(lax.fori_loop(0, i + 1, count, jnp.zeros((8, t), I32)), axis=0, keepdims=True)
            enough = c >= topk
            return (jnp.where(enough, mid, lo), jnp.where(enough, hi, mid),
                    jnp.where(enough, c, cnt_lo), jnp.where(enough, cnt_hi, c))

        lo, _, cnt_lo, cnt_hi = lax.fori_loop(
            0, KEY_BITS, bisect,
            (jnp.full((1, t), INT_MIN, I32), jnp.full((1, t), INT_MAX, I32),
             jnp.full((1, t), topk + 1, I32), jnp.zeros((1, t), I32)))
        thr_ref[...] = jnp.where(needs, lo, INT_MIN)

        tied = jnp.where(jnp.logical_and(needs, cnt_lo > topk), 1, 0)

        @pl.when(jnp.max(tied) > 0)
        def _():
            wanted = (topk - cnt_hi).astype(F32)
            tied_f = tied.astype(F32)
            earlier = jnp.where(lax.broadcasted_iota(I32, (t, t), 1) < lax.broadcasted_iota(I32, (t, t), 0),
                                1.0, 0.0).astype(BF16)

            def demote(kb, seen):
                key = sc_ref[kb]
                is_tie = jnp.where(key == lo, tied_f, 0.0)
                rank = seen + jnp.dot(earlier, is_tie.astype(BF16), preferred_element_type=F32)
                drop = is_tie * jnp.where(rank >= wanted, 1.0, 0.0)
                sc_ref[kb] = jnp.where(drop > 0.0, lo - 1, key)
                return seen + jnp.sum(is_tie, axis=0, keepdims=True)

            lax.fori_loop(0, i + 1, demote, jnp.zeros((1, t), F32))

    @pl.when(q0 + t <= topk)
    def _():
        thr_ref[...] = jnp.full((1, t), INT_MIN, I32)

    m_ref[...] = jnp.full(m_ref.shape, NEG, F32)
    acc_ref[...] = jnp.zeros(acc_ref.shape, F32)
    thr = thr_ref[...]

    def attn_block(kb, mode):
        ks = pl.ds(pl.multiple_of(kb * t, t), t)
        mask = jnp.where(sc_ref[kb] >= thr, 0.0, NEG)
        if mode == 0:
            mask = jnp.where(causal, mask, NEG)
        mask_ref[...] = mask

        def logits(h):
            g = h // GROUP
            s_ref[h][...] = _dot_nt(k_ref[ks, g * HEAD_DIM:(g + 1) * HEAD_DIM],
                                    q_ref[:, h * HEAD_DIM:(h + 1) * HEAD_DIM])

        def numerators(h):
            s = s_ref[h][...] + mask_ref[...]
            if mode != "far":
                s = s + bias_ref[h, mode]
            m_old = m_ref[h]
            m_new = jnp.maximum(m_old, jnp.max(s, axis=0, keepdims=True))
            p_ref[h][...] = jnp.exp2(s - m_new).astype(BF16)
            alpha_ref[h] = jnp.exp2(m_old - m_new)
            m_ref[h] = m_new

        def weighted_values(h):
            pv = jnp.dot(vt_ref[kb, h // GROUP], p_ref[h][...], preferred_element_type=F32)
            acc_ref[h] = alpha_ref[h] * acc_ref[h] + pv

        for stage in (logits, numerators, weighted_values):
            for h in range(N_HEADS):
                stage(h)

    def far_loop(kb, carry):
        attn_block(kb, "far")
        return carry

    lax.fori_loop(0, jnp.maximum(i - 1, 0), far_loop, 0)

    @pl.when(i >= 1)
    def _():
        attn_block(i - 1, 1)

    attn_block(i, 0)

    for h in range(N_HEADS):
        acc = acc_ref[h]
        out_t = acc[0:HEAD_DIM, :] / acc[HEAD_DIM:HEAD_DIM + 1, :]
        o_ref[:, h * HEAD_DIM:(h + 1) * HEAD_DIM] = out_t.T.astype(o_ref.dtype)


def _attention(proj, kiw, bias_tiles, batch, seq, q_col, layer, casts):
    t = ATT_BLOCK
    nq = seq // t
    steps = batch * nq
    cast_in, cast_out, cast_shapes = [], [], []
    for w in casts:
        rows = w.shape[1] // steps
        assert rows * steps == w.shape[1] and rows % BF16_SUBLANES == 0
        cast_in.append(pl.BlockSpec((None, rows, w.shape[2]), lambda b, i: (layer, b * nq + i, 0)))
        cast_out.append(pl.BlockSpec((rows, w.shape[2]), lambda b, i: (b * nq + i, 0)))
        cast_shapes.append(jax.ShapeDtypeStruct(w.shape[1:], BF16))
    topk = min(TOPK_MAX, seq // 4)
    attn_dim = N_HEADS * HEAD_DIM
    kv_dim = N_KV_HEADS * HEAD_DIM
    qi_half = IDX_HEADS * IDX_DIM // 2
    k_col = q_col + attn_dim
    qi_col = k_col + 2 * kv_dim
    assert q_col % attn_dim == 0 and k_col % kv_dim == 0 and qi_col % qi_half == 0
    return pl.pallas_call(
        functools.partial(_attn_body, t=t, topk=topk, n_casts=len(casts)),
        grid=(batch, nq),
        in_specs=[
            pl.BlockSpec((t, attn_dim), lambda b, i: (b * nq + i, q_col // attn_dim)),
            pl.BlockSpec((seq, kv_dim), lambda b, i: (b, k_col // kv_dim)),
            pl.BlockSpec((seq, kv_dim), lambda b, i: (b, k_col // kv_dim + 1)),
            pl.BlockSpec((t, qi_half), lambda b, i: (b * nq + i, qi_col // qi_half)),
            pl.BlockSpec((t, qi_half), lambda b, i: (b * nq + i, qi_col // qi_half + 1)),
            pl.BlockSpec((seq, IDX_PAD), lambda b, i: (b, 0)),
            pl.BlockSpec((N_HEADS, 2, t, t), lambda b, i: (0, 0, 0, 0)),
        ] + cast_in,
        out_specs=[pl.BlockSpec((t, attn_dim), lambda b, i: (b * nq + i, 0))] + cast_out,
        out_shape=[jax.ShapeDtypeStruct((batch * seq, attn_dim), BF16)] + cast_shapes,
        scratch_shapes=[
            pltpu.VMEM((nq, t, t), I32),
            pltpu.VMEM((1, t), I32),
            pltpu.VMEM((nq, N_KV_HEADS, HEAD_DIM + ONES_ROWS, t), BF16),
            pltpu.VMEM((seq, IDX_DIM), BF16),
            pltpu.VMEM((IDX_HEADS, t, IDX_DIM), BF16),
            pltpu.VMEM((t, t), F32),
            pltpu.VMEM((N_HEADS, 1, t), F32),
            pltpu.VMEM((N_HEADS, 1, t), F32),
            pltpu.VMEM((N_HEADS, HEAD_DIM + ONES_ROWS, t), F32),
        ] + [pltpu.VMEM((t, t), F32)] * N_HEADS + [pltpu.VMEM((t, t), BF16)] * N_HEADS,
        compiler_params=_params("parallel", "arbitrary"),
        name="sparse_attention",
    )(proj, proj, proj, proj, proj, kiw, bias_tiles, *casts)


def _mixer_body(main_ref, halo_ref, ya_ref, g0_ref, g1_ref, g2_ref, wb_ref, pw_ref, ps_ref, cw_ref,
                o_ref, y_ref, acc_ref, *, tm, tn, seq, pool_dim, conv_dim):
    pos0 = (pl.program_id(0) * tm) % seq
    pos = lax.broadcasted_iota(I32, (tm, 1), 0) + pos0
    keep = jnp.where(pos0 == 0, 0.0, 1.0)
    n_out = o_ref.shape[-1]

    def branch(n, y, c0):
        return g_refs[n][:, c0:c0 + tn].astype(F32) * jnp.dot(y, wb_ref[n, :, c0:c0 + tn], preferred_element_type=F32)

    g_refs = (g0_ref, g1_ref, g2_ref)
    for c0 in range(0, n_out, tn):
        acc_ref[:, c0:c0 + tn] = branch(2, ya_ref[...], c0)

    def ext(c0, c1):
        halo = halo_ref[:, c0:c1].astype(F32) * keep
        return jnp.concatenate([halo, main_ref[:, c0:c1].astype(F32)], axis=0)

    for g, win in enumerate(POOL_WINDOWS):
        c0 = g * POOL_GROUP_DIM
        u = ext(c0, c0 + POOL_GROUP_DIM)
        s, sh = u, 1
        while sh < win:
            s = s + pltpu.roll(s, sh, axis=0)
            sh *= 2
        cnt = jnp.minimum(pos + 1, win).astype(F32)
        dlt = s[POOL_HALO:] / cnt - u[POOL_HALO:]
        yp = jnp.dot(dlt.astype(BF16), pw_ref[g], preferred_element_type=F32)
        y_ref[0, :, c0:c0 + POOL_GROUP_DIM] = (yp * ps_ref[:, c0:c0 + POOL_GROUP_DIM]).astype(BF16)

    z = ext(pool_dim, pool_dim + conv_dim) * ext(pool_dim + conv_dim, pool_dim + 2 * conv_dim)
    yc = z[POOL_HALO:] * cw_ref[CONV_WIDTH - 1:CONV_WIDTH, :]
    for tap in range(1, CONV_WIDTH):
        yc = yc + pltpu.roll(z, tap, axis=0)[POOL_HALO:] * cw_ref[CONV_WIDTH - 1 - tap:CONV_WIDTH - tap, :]
    gate_b = main_ref[:, pool_dim + 2 * conv_dim:pool_dim + 3 * conv_dim].astype(F32)
    y_ref[1] = (gate_b * yc).astype(BF16)

    for c0 in range(0, n_out, tn):
        mixed = acc_ref[:, c0:c0 + tn] + branch(0, y_ref[0], c0) + branch(1, y_ref[1], c0)
        o_ref[:, c0:c0 + tn] = mixed.astype(o_ref.dtype)


def _mixer(proj, gate_col, y_attn, w_branch, pool_w, pool_scale, conv_w, layer, seq, *, tm, tn):
    main = gates = proj
    tok = proj.shape[0]
    branch_dim = y_attn.shape[-1]
    width = 4 * branch_dim
    d = w_branch.shape[-1]
    assert gate_col % d == 0 and d % tn == 0
    gj = gate_col // d
    halo_blocks = tm // POOL_HALO
    resident = dict(pipeline_mode=pl.Buffered(1))
    return pl.pallas_call(
        functools.partial(_mixer_body, tm=tm, tn=tn, seq=seq, pool_dim=branch_dim, conv_dim=branch_dim),
        grid=(tok // tm,),
        in_specs=[
            pl.BlockSpec((tm, width), lambda i: (i, 0)),
            pl.BlockSpec((POOL_HALO, width), lambda i: (jnp.maximum(i * halo_blocks - 1, 0), 0)),
            pl.BlockSpec((tm, branch_dim), lambda i: (i, 0)),
            pl.BlockSpec((tm, d), lambda i: (i, gj)),
            pl.BlockSpec((tm, d), lambda i: (i, gj + 1)),
            pl.BlockSpec((tm, d), lambda i: (i, gj + 2)),
            pl.BlockSpec((None, 3, branch_dim, d), lambda i: (0, 0, 0, 0), **resident),
            pl.BlockSpec((None,) + pool_w.shape[1:], lambda i: (layer, 0, 0, 0), **resident),
            pl.BlockSpec((None, 1, branch_dim), lambda i: (layer, 0, 0)),
            pl.BlockSpec((None, CONV_WIDTH, branch_dim), lambda i: (layer, 0, 0)),
        ],
        out_specs=pl.BlockSpec((tm, d), lambda i: (i, 0)),
        out_shape=jax.ShapeDtypeStruct((tok, d), BF16),
        scratch_shapes=[pltpu.VMEM((2, tm, branch_dim), BF16), pltpu.VMEM((tm, d), F32)],
        compiler_params=_params("parallel"),
        name="mixer_branches",
    )(main, main, y_attn, gates, gates, gates, w_branch, pool_w, pool_scale, conv_w)


def _proj_norm_res_body(a_ref, w_ref, x_ref, g_ref, o_ref):
    half = a_ref.shape[0] // 2
    for rows in (slice(0, half), slice(half, 2 * half)):
        m = jnp.dot(a_ref[rows, :], w_ref[...], preferred_element_type=F32)
        o_ref[rows, :] = x_ref[rows, :] + _rms(m, g_ref[...])


def _proj_norm_res(a, w, layer, x, gain, *, tm):
    tok, d = x.shape
    k = a.shape[-1]
    return pl.pallas_call(
        _proj_norm_res_body,
        grid=(tok // tm,),
        in_specs=[
            pl.BlockSpec((tm, k), lambda i: (i, 0)),
            pl.BlockSpec((None, k, d), lambda i: (layer, 0, 0)),
            pl.BlockSpec((tm, d), lambda i: (i, 0)),
            pl.BlockSpec((1, d), lambda i: (0, 0)),
        ],
        out_specs=pl.BlockSpec((tm, d), lambda i: (i, 0)),
        out_shape=jax.ShapeDtypeStruct((tok, d), F32),
        compiler_params=_params("parallel"),
        name="out_proj",
    )(a, w, x, gain)


def _mlp_body(x_ref, gpre_ref, gpost_ref, wu_ref, wd_ref, o_ref, h_ref, acc_ref):
    j = pl.program_id(1)

    @pl.when(j == 0)
    def _():
        h_ref[...] = _rms(x_ref[...], gpre_ref[...]).astype(BF16)
        acc_ref[...] = jnp.zeros(acc_ref.shape, F32)

    hid = jnp.dot(h_ref[...], wu_ref[...], preferred_element_type=F32)
    hid = jnp.square(jnp.maximum(hid, 0.0)).astype(BF16)
    acc_ref[...] += jnp.dot(hid, wd_ref[...], preferred_element_type=F32)

    @pl.when(j == pl.num_programs(1) - 1)
    def _():
        o_ref[...] = x_ref[...] + _rms(acc_ref[...], gpost_ref[...])


def _mlp(x, gpre, gpost, w_up, w_down, layer, *, tm, th):
    tok, d = x.shape
    hidden = w_up.shape[-1]
    return pl.pallas_call(
        _mlp_body,
        grid=(tok // tm, hidden // th),
        in_specs=[
            pl.BlockSpec((tm, d), lambda i, j: (i, 0)),
            pl.BlockSpec((1, d), lambda i, j: (0, 0)),
            pl.BlockSpec((1, d), lambda i, j: (0, 0)),
            pl.BlockSpec((None, d, th), lambda i, j: (layer, 0, j)),
            pl.BlockSpec((None, th, d), lambda i, j: (layer, j, 0)),
        ],
        out_specs=pl.BlockSpec((tm, d), lambda i, j: (i, 0)),
        out_shape=jax.ShapeDtypeStruct((tok, d), F32),
        scratch_shapes=[pltpu.VMEM((tm, d), BF16), pltpu.VMEM((tm, d), F32)],
        compiler_params=_params("parallel", "arbitrary"),
        name="mlp",
    )(x, gpre, gpost, w_up, w_down)


def _tile(n, pref):
    t = min(n, pref)
    assert n % t == 0, (n, pref)
    return t


@jax.jit
def _forward(x, norm_gains, w_in, pool_w, pool_scale, conv_w, rel_bias, w_branch,
             w_gate, b_gate, w_out, w_up, w_down):
    batch, seq, d = x.shape
    depth = w_in.shape[0]
    tok = batch * seq
    branch_dim = d // 2
    attn_dim = N_HEADS * HEAD_DIM
    kv_dim = N_KV_HEADS * HEAD_DIM
    qi_dim = IDX_HEADS * IDX_DIM
    assert branch_dim == attn_dim == len(POOL_WINDOWS) * POOL_GROUP_DIM
    assert seq % ATT_BLOCK == 0 and ATT_BLOCK >= MAX_DISTANCE

    main_w = 4 * branch_dim
    n_in = main_w + attn_dim + 2 * kv_dim + qi_dim
    kw_w = IDX_DIM + IDX_HEADS
    assert w_in.shape[-1] == n_in + kw_w
    w_in_t = jnp.swapaxes(w_in, 1, 2)
    w_kw_t = jnp.pad(w_in_t[:, n_in:, :], ((0, 0), (0, IDX_PAD - kw_w), (0, 0))).astype(BF16)
    later_weights = (w_branch.reshape(depth, 3 * branch_dim, d), w_out, w_up, w_down)
    pool_w_b = pool_w.astype(BF16)

    gate_col = main_w
    q_col = main_w + 3 * d
    scale = jnp.concatenate([jnp.ones((q_col,), F32), jnp.full((attn_dim,), HEAD_DIM ** -0.5 * LOG2E, F32),
                             jnp.ones((n_in - main_w - attn_dim,), F32)])

    bias_tiles = _bias_tiles(rel_bias, ATT_BLOCK)

    tm = _tile(tok, 1024)
    tm_s = _tile(seq, 512)
    xf = x.reshape(tok, d)
    for l in range(depth):
        g = norm_gains[l]
        g0, g1, g2, g3 = (g[n:n + 1] for n in range(4))
        bias = jnp.concatenate([jnp.zeros((main_w,), F32), b_gate[l].reshape(3 * d), jnp.zeros((n_in - main_w,), F32)])
        proj, kiw = _projections(xf, g0, w_in_t, w_gate, w_kw_t, l, jnp.stack([scale, bias]), main_w, n_in,
                                 tm=_tile(tok, 2048), tn=512)
        y_attn, w_branch_b, w_out_b, w_up_b, w_down_b = _attention(proj, kiw, bias_tiles, batch, seq, q_col,
                                                                    l, later_weights)
        mixed = _mixer(proj, gate_col, y_attn, w_branch_b.reshape(1, 3, branch_dim, d), pool_w_b,
                       pool_scale.reshape(depth, 1, branch_dim), conv_w, l, seq, tm=tm_s, tn=1024)
        xf = _proj_norm_res(mixed, w_out_b[None], 0, xf, g1, tm=tm_s)
        xf = _mlp(xf, g2, g3, w_up_b[None], w_down_b[None], 0, tm=tm_s, th=1024)
    return xf.reshape(batch, seq, d)


def kernel(x, norm_gains, w_in, pool_w, pool_scale, conv_w, rel_bias, w_branch, w_gate, b_gate, w_out, w_up, w_down):
    return _forward(x, norm_gains, w_in, pool_w, pool_scale, conv_w, rel_bias, w_branch,
                    w_gate, b_gate, w_out, w_up, w_down)
```

```python
import functools
import math

import jax
import jax.numpy as jnp
import numpy as np
from jax import lax
from jax.experimental import pallas as pl
from jax.experimental.pallas import tpu as pltpu

F32 = jnp.float32
BF16 = jnp.bfloat16
I32 = jnp.int32
I16 = jnp.int16

NORM_EPS = 1e-6
POOL_WINDOWS = (2, 4, 8, 16)
POOL_GROUP_DIM = 256
POOL_HALO = 16
CONV_WIDTH = 3
N_HEADS = 8
N_KV_HEADS = 2
GROUP = N_HEADS // N_KV_HEADS
HEAD_DIM = 128
IDX_HEADS = 16
IDX_DIM = 64
TOPK_MAX = 256
NUM_BUCKETS = 32
MAX_EXACT = NUM_BUCKETS // 2
MAX_DISTANCE = 128
ATT_BLOCK = 256
IDX_PAD = 128
HALF_BITS = 16
INT_MIN = -(2 ** 31)
INT_MAX = 2 ** 31 - 1
NEG = -1e30
LOG2E = math.log2(math.e)
ONES_ROWS = 16
VMEM_LIMIT = 56 * 1024 * 1024
MXU_WIDTH = 256
BF16_SUBLANES = 16
NORM_ROWS = 256


def _bucket_thresholds():
    n = np.arange(MAX_EXACT, 8 * MAX_DISTANCE, dtype=np.int64)
    nf = n.astype(np.float32)
    large = MAX_EXACT + (np.log(nf / np.float32(MAX_EXACT)) / np.float32(math.log(MAX_DISTANCE / MAX_EXACT))
                         * np.float32(NUM_BUCKETS - MAX_EXACT)).astype(np.int32)
    large = np.minimum(large, NUM_BUCKETS - 1)
    thr = []
    for b in range(MAX_EXACT + 1, NUM_BUCKETS):
        thr.append(int(n[np.argmax(large >= b)]))
    return tuple(thr)


BUCKET_THRESHOLDS = _bucket_thresholds()


def _params(*sem):
    return pltpu.CompilerParams(dimension_semantics=sem, vmem_limit_bytes=VMEM_LIMIT)


def _rms(x, gain):
    ms = jnp.mean(x * x, axis=-1, keepdims=True)
    return x * lax.rsqrt(ms + NORM_EPS) * gain


def _dot_nt(a, b):
    return lax.dot_general(a, b, (((1,), (1,)), ((), ())), preferred_element_type=F32)


def _proj_body(x_ref, g_ref, wint_ref, wg_ref, wkwt_ref, post_ref, o_ref, okw_ref, h_ref, *, gate_lo, gate_hi):
    j = pl.program_id(1)
    is_gate = jnp.logical_and(j >= gate_lo, j < gate_hi)

    @pl.when(j == 0)
    def _():
        def norm_rows(r, carry):
            rows = pl.ds(pl.multiple_of(r * NORM_ROWS, NORM_ROWS), NORM_ROWS)
            h = _rms(x_ref[rows, :], g_ref[...]).astype(BF16)
            h_ref[rows, :] = h
            okw_ref[rows, :] = _dot_nt(h, wkwt_ref[...])
            return carry

        lax.fori_loop(0, x_ref.shape[0] // NORM_ROWS, norm_rows, 0)

    chunks = [slice(c, c + MXU_WIDTH) for c in range(0, o_ref.shape[-1], MXU_WIDTH)]

    @pl.when(jnp.logical_not(is_gate))
    def _():
        for cs in chunks:
            y = _dot_nt(h_ref[...], wint_ref[cs, :].astype(BF16))
            o_ref[:, cs] = (y * post_ref[0:1, cs]).astype(o_ref.dtype)

    @pl.when(is_gate)
    def _():
        for cs in chunks:
            y = jnp.dot(h_ref[...], wg_ref[:, cs].astype(BF16), preferred_element_type=F32) + post_ref[1:2, cs]
            o_ref[:, cs] = (0.5 * jnp.tanh(0.5 * y) + 0.5).astype(o_ref.dtype)


def _projections(x, gain, w_in_t, w_gate, w_kw_t, layer, post, n_main, n_in, *, tm, tn):
    tok, d = x.shape
    n_gate = w_gate.shape[-1]
    assert n_main % tn == 0 and n_in % tn == 0 and n_gate % tn == 0 and tm % NORM_ROWS == 0
    gate_lo = n_main // tn
    gate_blocks = n_gate // tn
    gate_hi = gate_lo + gate_blocks
    kw = w_kw_t.shape[1]
    return pl.pallas_call(
        functools.partial(_proj_body, gate_lo=gate_lo, gate_hi=gate_hi),
        grid=(tok // tm, (n_in + n_gate) // tn),
        in_specs=[
            pl.BlockSpec((tm, d), lambda i, j: (i, 0), pipeline_mode=pl.Buffered(1)),
            pl.BlockSpec((1, d), lambda i, j: (0, 0)),
            pl.BlockSpec((None, tn, d),
                         lambda i, j: (layer, jnp.where(j < gate_lo, j, jnp.maximum(j - gate_blocks, gate_lo)), 0)),
            pl.BlockSpec((None, d, tn), lambda i, j: (layer, 0, jnp.clip(j - gate_lo, 0, gate_blocks - 1))),
            pl.BlockSpec((None, kw, d), lambda i, j: (layer, 0, 0)),
            pl.BlockSpec((2, tn), lambda i, j: (0, j)),
        ],
        out_specs=[
            pl.BlockSpec((tm, tn), lambda i, j: (i, j)),
            pl.BlockSpec((tm, kw), lambda i, j: (i, 0)),
        ],
        out_shape=[
            jax.ShapeDtypeStruct((tok, n_in + n_gate), BF16),
            jax.ShapeDtypeStruct((tok, kw), F32),
        ],
        scratch_shapes=[pltpu.VMEM((tm, d), BF16)],
        compiler_params=_params("parallel", "arbitrary"),
        name="projections",
    )(x, gain, w_in_t, w_gate, w_kw_t, post)


def _bias_tiles_body(rel_ref, o_ref, *, t):
    row = lax.broadcasted_iota(I32, (t, t), 0)
    col = lax.broadcasted_iota(I32, (t, t), 1)
    for off in range(2):
        n = jnp.maximum(col - row + off * t, 0)
        large = jnp.full((t, t), MAX_EXACT, I32)
        for thr in BUCKET_THRESHOLDS:
            large = large + jnp.where(n >= thr, 1, 0)
        bucket = jnp.where(n < MAX_EXACT, n, large)
        for h in range(N_HEADS):
            val = jnp.zeros((t, t), F32)
            for b in range(NUM_BUCKETS):
                val = jnp.where(bucket == b, rel_ref[b, h], val)
            o_ref[h, off] = (val - rel_ref[NUM_BUCKETS - 1, h]) * LOG2E


def _bias_tiles(rel_bias, t):
    return pl.pallas_call(
        functools.partial(_bias_tiles_body, t=t),
        in_specs=[pl.BlockSpec(memory_space=pltpu.SMEM)],
        out_specs=pl.BlockSpec(memory_space=pltpu.VMEM),
        out_shape=jax.ShapeDtypeStruct((N_HEADS, 2, t, t), F32),
        compiler_params=pltpu.CompilerParams(vmem_limit_bytes=VMEM_LIMIT),
        name="bias_tiles",
    )(rel_bias)


def _attn_body(*refs, t, topk, n_casts):
    (q_ref, k_ref, v_ref, qia_ref, qib_ref, kiw_ref, bias_ref), refs = refs[:7], refs[7:]
    cast_src, refs = refs[:n_casts], refs[n_casts:]
    o_ref, cast_dst, refs = refs[0], refs[1:1 + n_casts], refs[1 + n_casts:]
    (sc_ref, sch_ref, thr_ref, vt_ref, kib_ref, qih_ref, mask_ref, alpha_ref, m_ref, acc_ref), refs = (
        refs[:10], refs[10:])
    s_ref, p_ref = refs[:N_HEADS], refs[N_HEADS:]

    for src, dst in zip(cast_src, cast_dst):
        dst[...] = src[...].astype(BF16)

    i = pl.program_id(1)
    nkb = vt_ref.shape[0]
    q0 = pl.multiple_of(i * t, t)
    key_pos = lax.broadcasted_iota(I32, (t, 1), 0)
    qry_pos = lax.broadcasted_iota(I32, (1, t), 1)
    causal = key_pos <= qry_pos

    @pl.when(i == 0)
    def _():
        for kb in range(nkb):
            rows = slice(kb * t, (kb + 1) * t)
            vb = v_ref[rows, :].astype(F32)
            for g in range(N_KV_HEADS):
                vt_ref[kb, g, 0:HEAD_DIM, :] = vb[:, g * HEAD_DIM:(g + 1) * HEAD_DIM].T.astype(BF16)
                vt_ref[kb, g, HEAD_DIM:, :] = jnp.ones((vt_ref.shape[2] - HEAD_DIM, t), BF16)
            kib_ref[rows, :] = kiw_ref[rows, 0:IDX_DIM].astype(BF16)

    half = IDX_HEADS // 2
    for h in range(IDX_HEADS):
        src = qia_ref if h < half else qib_ref
        qih_ref[h] = src[:, (h % half) * IDX_DIM:(h % half + 1) * IDX_DIM]
    w_t = kiw_ref[pl.ds(q0, t), :].T[IDX_DIM:IDX_DIM + IDX_HEADS, :] * (IDX_HEADS ** -0.5)

    def score_block(kb, diag):
        kblk = kib_ref[pl.ds(pl.multiple_of(kb * t, t), t), :]
        acc = jnp.zeros((t, t), F32)
        for h in range(IDX_HEADS):
            acc = acc + w_t[h:h + 1, :] * jnp.maximum(_dot_nt(kblk, qih_ref[h]), 0.0)
        if diag:
            acc = jnp.where(causal, acc, -jnp.inf)
        bits = pltpu.bitcast(acc, I32)
        key = bits ^ ((bits >> 31) & INT_MAX)
        sc_ref[kb] = key
        sch_ref[kb] = (key >> HALF_BITS).astype(I16)

    def score_loop(kb, carry):
        score_block(kb, False)
        return carry

    lax.fori_loop(0, i, score_loop, 0)
    score_block(i, True)

    @pl.when(i % 2 == 0)
    def _():
        sc_ref[i + 1] = jnp.full((t, t), INT_MIN, I32)
        sch_ref[i + 1] = jnp.full((t, t), INT_MIN >> HALF_BITS, I16)

    @pl.when(q0 + t > topk)
    def _():
        needs = q0 + qry_pos >= topk

        def halving(ref):
            rows = 32 // ref.dtype.itemsize
            one, zero = jnp.ones((), ref.dtype), jnp.zeros((), ref.dtype)

            def step(_, state):
                lo, hi, cnt_lo, cnt_hi = state
                mid = (lo & hi) + ((lo ^ hi) >> 1)
                mid_c = mid.astype(ref.dtype)

                def count(pair, c):
                    for kb in (2 * pair, 2 * pair + 1):
                        ge = jnp.where(ref[kb] >= mid_c, one, zero)
                        if ref.dtype == I32:
                            c = c + jnp.sum(ge.reshape(t // rows, rows, t), axis=0)
                        else:
                            for r in range(0, t, rows):
                                c = c + ge[r:r + rows]
                    return c

                c = lax.fori_loop(0, (i + 2) // 2, count, jnp.zeros((rows, t), ref.dtype))
                c = jnp.sum(c.astype(I32), axis=0, keepdims=True)
                enough = c >= topk
                return (jnp.where(enough, mid, lo), jnp.where(enough, hi, mid),
                        jnp.where(enough, c, cnt_lo), jnp.where(enough, cnt_hi, c))

            return step

        half_lo, half_hi = INT_MIN >> HALF_BITS, -(INT_MIN >> HALF_BITS)
        lo, _, cnt_lo, cnt_hi = lax.fori_loop(
            0, HALF_BITS, halving(sch_ref),
            (jnp.full((1, t), half_lo, I32), jnp.full((1, t), half_hi, I32),
             jnp.full((1, t), topk + 1, I32), jnp.zeros((1, t), I32)))
        hi = jnp.where(lo + 1 == half_hi, INT_MAX, (lo + 1) << HALF_BITS)
        lo, _, cnt_lo, cnt_hi = lax.fori_loop(0, HALF_BITS, halving(sc_ref), (lo << HALF_BITS, hi, cnt_lo, cnt_hi))
        thr_ref[...] = jnp.where(needs, lo, INT_MIN)

        tied = jnp.where(jnp.logical_and(needs, cnt_lo > topk), 1, 0)

        @pl.when(jnp.max(tied) > 0)
        def _():
            wanted = (topk - cnt_hi).astype(F32)
            tied_f = tied.astype(F32)
            earlier = jnp.where(lax.broadcasted_iota(I32, (t, t), 1) < lax.broadcasted_iota(I32, (t, t), 0),
                                1.0, 0.0).astype(BF16)

            def demote(kb, seen):
                key = sc_ref[kb]
                is_tie = jnp.where(key == lo, tied_f, 0.0)
                rank = seen + jnp.dot(earlier, is_tie.astype(BF16), preferred_element_type=F32)
                drop = is_tie * jnp.where(rank >= wanted, 1.0, 0.0)
                sc_ref[kb] = jnp.where(drop > 0.0, lo - 1, key)
                return seen + jnp.sum(is_tie, axis=0, keepdims=True)

            lax.fori_loop(0, i + 1, demote, jnp.zeros((1, t), F32))

    @pl.when(q0 + t <= topk)
    def _():
        thr_ref[...] = jnp.full((1, t), INT_MIN, I32)

    m_ref[...] = jnp.full(m_ref.shape, NEG, F32)
    acc_ref[...] = jnp.zeros(acc_ref.shape, F32)
    thr = thr_ref[...]

    def attn_block(kb, mode):
        ks = pl.ds(pl.multiple_of(kb * t, t), t)
        mask = jnp.where(sc_ref[kb] >= thr, 0.0, NEG)
        if mode == 0:
            mask = jnp.where(causal, mask, NEG)
        mask_ref[...] = mask

        def logits(h):
            g = h // GROUP
            s_ref[h][...] = _dot_nt(k_ref[ks, g * HEAD_DIM:(g + 1) * HEAD_DIM],
                                    q_ref[:, h * HEAD_DIM:(h + 1) * HEAD_DIM])

        def numerators(h):
            s = s_ref[h][...] + mask_ref[...]
            if mode != "far":
                s = s + bias_ref[h, mode]
            m_old = m_ref[h]
            m_new = jnp.maximum(m_old, jnp.max(s, axis=0, keepdims=True))
            p_ref[h][...] = jnp.exp2(s - m_new).astype(BF16)
            alpha_ref[h] = jnp.exp2(m_old - m_new)
            m_ref[h] = m_new

        def weighted_values(h):
            pv = jnp.dot(vt_ref[kb, h // GROUP], p_ref[h][...], preferred_element_type=F32)
            acc_ref[h] = alpha_ref[h] * acc_ref[h] + pv

        for stage in (logits, numerators, weighted_values):
            for h in range(N_HEADS):
                stage(h)

    def far_loop(kb, carry):
        attn_block(kb, "far")
        return carry

    lax.fori_loop(0, jnp.maximum(i - 1, 0), far_loop, 0)

    @pl.when(i >= 1)
    def _():
        attn_block(i - 1, 1)

    attn_block(i, 0)

    for h in range(N_HEADS):
        acc = acc_ref[h]
        out_t = acc[0:HEAD_DIM, :] / acc[HEAD_DIM:HEAD_DIM + 1, :]
        o_ref[:, h * HEAD_DIM:(h + 1) * HEAD_DIM] = out_t.T.astype(o_ref.dtype)


def _attention(proj, kiw, bias_tiles, batch, seq, q_col, layer, casts):
    t = ATT_BLOCK
    nq = seq // t
    steps = batch * nq
    cast_in, cast_out, cast_shapes = [], [], []
    for w in casts:
        rows = w.shape[1] // steps
        assert rows * steps == w.shape[1] and rows % BF16_SUBLANES == 0
        cast_in.append(pl.BlockSpec((None, rows, w.shape[2]), lambda b, i: (layer, b * nq + i, 0)))
        cast_out.append(pl.BlockSpec((rows, w.shape[2]), lambda b, i: (b * nq + i, 0)))
        cast_shapes.append(jax.ShapeDtypeStruct(w.shape[1:], BF16))
    topk = min(TOPK_MAX, seq // 4)
    attn_dim = N_HEADS * HEAD_DIM
    kv_dim = N_KV_HEADS * HEAD_DIM
    qi_half = IDX_HEADS * IDX_DIM // 2
    k_col = q_col + attn_dim
    qi_col = k_col + 2 * kv_dim
    assert q_col % attn_dim == 0 and k_col % kv_dim == 0 and qi_col % qi_half == 0
    return pl.pallas_call(
        functools.partial(_attn_body, t=t, topk=topk, n_casts=len(casts)),
        grid=(batch, nq),
        in_specs=[
            pl.BlockSpec((t, attn_dim), lambda b, i: (b * nq + i, q_col // attn_dim)),
            pl.BlockSpec((seq, kv_dim), lambda b, i: (b, k_col // kv_dim)),
            pl.BlockSpec((seq, kv_dim), lambda b, i: (b, k_col // kv_dim + 1)),
            pl.BlockSpec((t, qi_half), lambda b, i: (b * nq + i, qi_col // qi_half)),
            pl.BlockSpec((t, qi_half), lambda b, i: (b * nq + i, qi_col // qi_half + 1)),
            pl.BlockSpec((seq, IDX_PAD), lambda b, i: (b, 0)),
            pl.BlockSpec((N_HEADS, 2, t, t), lambda b, i: (0, 0, 0, 0)),
        ] + cast_in,
        out_specs=[pl.BlockSpec((t, attn_dim), lambda b, i: (b * nq + i, 0))] + cast_out,
        out_shape=[jax.ShapeDtypeStruct((batch * seq, attn_dim), BF16)] + cast_shapes,
        scratch_shapes=[
            pltpu.VMEM((nq + 1, t, t), I32),
            pltpu.VMEM((nq + 1, t, t), I16),
            pltpu.VMEM((1, t), I32),
            pltpu.VMEM((nq, N_KV_HEADS, HEAD_DIM + ONES_ROWS, t), BF16),
            pltpu.VMEM((seq, IDX_DIM), BF16),
            pltpu.VMEM((IDX_HEADS, t, IDX_DIM), BF16),
            pltpu.VMEM((t, t), F32),
            pltpu.VMEM((N_HEADS, 1, t), F32),
            pltpu.VMEM((N_HEADS, 1, t), F32),
            pltpu.VMEM((N_HEADS, HEAD_DIM + ONES_ROWS, t), F32),
        ] + [pltpu.VMEM((t, t), F32)] * N_HEADS + [pltpu.VMEM((t, t), BF16)] * N_HEADS,
        compiler_params=_params("parallel", "arbitrary"),
        name="sparse_attention",
    )(proj, proj, proj, proj, proj, kiw, bias_tiles, *casts)


def _mixer_body(main_ref, halo_ref, ya_ref, g0_ref, g1_ref, g2_ref, wb_ref, pw_ref, ps_ref, cw_ref,
                o_ref, y_ref, acc_ref, *, tm, tn, seq, pool_dim, conv_dim):
    pos0 = (pl.program_id(0) * tm) % seq
    pos = lax.broadcasted_iota(I32, (tm, 1), 0) + pos0
    keep = jnp.where(pos0 == 0, 0.0, 1.0)
    n_out = o_ref.shape[-1]

    def branch(n, y, c0):
        return g_refs[n][:, c0:c0 + tn].astype(F32) * jnp.dot(y, wb_ref[n, :, c0:c0 + tn], preferred_element_type=F32)

    g_refs = (g0_ref, g1_ref, g2_ref)
    for c0 in range(0, n_out, tn):
        acc_ref[:, c0:c0 + tn] = branch(2, ya_ref[...], c0)

    def ext(c0, c1):
        halo = halo_ref[:, c0:c1].astype(F32) * keep
        return jnp.concatenate([halo, main_ref[:, c0:c1].astype(F32)], axis=0)

    for g, win in enumerate(POOL_WINDOWS):
        c0 = g * POOL_GROUP_DIM
        u = ext(c0, c0 + POOL_GROUP_DIM)
        s, sh = u, 1
        while sh < win:
            s = s + pltpu.roll(s, sh, axis=0)
            sh *= 2
        cnt = jnp.minimum(pos + 1, win).astype(F32)
        dlt = s[POOL_HALO:] / cnt - u[POOL_HALO:]
        yp = jnp.dot(dlt.astype(BF16), pw_ref[g], preferred_element_type=F32)
        y_ref[0, :, c0:c0 + POOL_GROUP_DIM] = (yp * ps_ref[:, c0:c0 + POOL_GROUP_DIM]).astype(BF16)

    z = ext(pool_dim, pool_dim + conv_dim) * ext(pool_dim + conv_dim, pool_dim + 2 * conv_dim)
    yc = z[POOL_HALO:] * cw_ref[CONV_WIDTH - 1:CONV_WIDTH, :]
    for tap in range(1, CONV_WIDTH):
        yc = yc + pltpu.roll(z, tap, axis=0)[POOL_HALO:] * cw_ref[CONV_WIDTH - 1 - tap:CONV_WIDTH - tap, :]
    gate_b = main_ref[:, pool_dim + 2 * conv_dim:pool_dim + 3 * conv_dim].astype(F32)
    y_ref[1] = (gate_b * yc).astype(BF16)

    for c0 in range(0, n_out, tn):
        mixed = acc_ref[:, c0:c0 + tn] + branch(0, y_ref[0], c0) + branch(1, y_ref[1], c0)
        o_ref[:, c0:c0 + tn] = mixed.astype(o_ref.dtype)


def _mixer(proj, gate_col, y_attn, w_branch, pool_w, pool_scale, conv_w, layer, seq, *, tm, tn):
    main = gates = proj
    tok = proj.shape[0]
    branch_dim = y_attn.shape[-1]
    width = 4 * branch_dim
    d = w_branch.shape[-1]
    assert gate_col % d == 0 and d % tn == 0
    gj = gate_col // d
    halo_blocks = tm // POOL_HALO
    resident = dict(pipeline_mode=pl.Buffered(1))
    return pl.pallas_call(
        functools.partial(_mixer_body, tm=tm, tn=tn, seq=seq, pool_dim=branch_dim, conv_dim=branch_dim),
        grid=(tok // tm,),
        in_specs=[
            pl.BlockSpec((tm, width), lambda i: (i, 0)),
            pl.BlockSpec((POOL_HALO, width), lambda i: (jnp.maximum(i * halo_blocks - 1, 0), 0)),
            pl.BlockSpec((tm, branch_dim), lambda i: (i, 0)),
            pl.BlockSpec((tm, d), lambda i: (i, gj)),
            pl.BlockSpec((tm, d), lambda i: (i, gj + 1)),
            pl.BlockSpec((tm, d), lambda i: (i, gj + 2)),
            pl.BlockSpec((None, 3, branch_dim, d), lambda i: (0, 0, 0, 0), **resident),
            pl.BlockSpec((None,) + pool_w.shape[1:], lambda i: (layer, 0, 0, 0), **resident),
            pl.BlockSpec((None, 1, branch_dim), lambda i: (layer, 0, 0)),
            pl.BlockSpec((None, CONV_WIDTH, branch_dim), lambda i: (layer, 0, 0)),
        ],
        out_specs=pl.BlockSpec((tm, d), lambda i: (i, 0)),
        out_shape=jax.ShapeDtypeStruct((tok, d), BF16),
        scratch_shapes=[pltpu.VMEM((2, tm, branch_dim), BF16), pltpu.VMEM((tm, d), F32)],
        compiler_params=_params("parallel"),
        name="mixer_branches",
    )(main, main, y_attn, gates, gates, gates, w_branch, pool_w, pool_scale, conv_w)


def _proj_norm_res_body(a_ref, w_ref, x_ref, g_ref, o_ref):
    half = a_ref.shape[0] // 2
    for rows in (slice(0, half), slice(half, 2 * half)):
        m = jnp.dot(a_ref[rows, :], w_ref[...], preferred_element_type=F32)
        o_ref[rows, :] = x_ref[rows, :] + _rms(m, g_ref[...])


def _proj_norm_res(a, w, layer, x, gain, *, tm):
    tok, d = x.shape
    k = a.shape[-1]
    return pl.pallas_call(
        _proj_norm_res_body,
        grid=(tok // tm,),
        in_specs=[
            pl.BlockSpec((tm, k), lambda i: (i, 0)),
            pl.BlockSpec((None, k, d), lambda i: (layer, 0, 0)),
            pl.BlockSpec((tm, d), lambda i: (i, 0)),
            pl.BlockSpec((1, d), lambda i: (0, 0)),
        ],
        out_specs=pl.BlockSpec((tm, d), lambda i: (i, 0)),
        out_shape=jax.ShapeDtypeStruct((tok, d), F32),
        compiler_params=_params("parallel"),
        name="out_proj",
    )(a, w, x, gain)


def _mlp_body(x_ref, gpre_ref, gpost_ref, wu_ref, wd_ref, o_ref, h_ref, acc_ref):
    j = pl.program_id(1)

    @pl.when(j == 0)
    def _():
        h_ref[...] = _rms(x_ref[...], gpre_ref[...]).astype(BF16)
        acc_ref[...] = jnp.zeros(acc_ref.shape, F32)

    hid = jnp.dot(h_ref[...], wu_ref[...], preferred_element_type=F32)
    hid = jnp.square(jnp.maximum(hid, 0.0)).astype(BF16)
    acc_ref[...] += jnp.dot(hid, wd_ref[...], preferred_element_type=F32)

    @pl.when(j == pl.num_programs(1) - 1)
    def _():
        o_ref[...] = x_ref[...] + _rms(acc_ref[...], gpost_ref[...])


def _mlp(x, gpre, gpost, w_up, w_down, layer, *, tm, th):
    tok, d = x.shape
    hidden = w_up.shape[-1]
    return pl.pallas_call(
        _mlp_body,
        grid=(tok // tm, hidden // th),
        in_specs=[
            pl.BlockSpec((tm, d), lambda i, j: (i, 0)),
            pl.BlockSpec((1, d), lambda i, j: (0, 0)),
            pl.BlockSpec((1, d), lambda i, j: (0, 0)),
            pl.BlockSpec((None, d, th), lambda i, j: (layer, 0, j)),
            pl.BlockSpec((None, th, d), lambda i, j: (layer, j, 0)),
        ],
        out_specs=pl.BlockSpec((tm, d), lambda i, j: (i, 0)),
        out_shape=jax.ShapeDtypeStruct((tok, d), F32),
        scratch_shapes=[pltpu.VMEM((tm, d), BF16), pltpu.VMEM((tm, d), F32)],
        compiler_params=_params("parallel", "arbitrary"),
        name="mlp",
    )(x, gpre, gpost, w_up, w_down)


def _tile(n, pref):
    t = min(n, pref)
    assert n % t == 0, (n, pref)
    return t


@jax.jit
def _forward(x, norm_gains, w_in, pool_w, pool_scale, conv_w, rel_bias, w_branch,
             w_gate, b_gate, w_out, w_up, w_down):
    batch, seq, d = x.shape
    depth = w_in.shape[0]
    tok = batch * seq
    branch_dim = d // 2
    attn_dim = N_HEADS * HEAD_DIM
    kv_dim = N_KV_HEADS * HEAD_DIM
    qi_dim = IDX_HEADS * IDX_DIM
    assert branch_dim == attn_dim == len(POOL_WINDOWS) * POOL_GROUP_DIM
    assert seq % ATT_BLOCK == 0 and ATT_BLOCK >= MAX_DISTANCE

    main_w = 4 * branch_dim
    n_in = main_w + attn_dim + 2 * kv_dim + qi_dim
    kw_w = IDX_DIM + IDX_HEADS
    assert w_in.shape[-1] == n_in + kw_w
    w_in_t = jnp.swapaxes(w_in, 1, 2)
    w_kw_t = jnp.pad(w_in_t[:, n_in:, :], ((0, 0), (0, IDX_PAD - kw_w), (0, 0))).astype(BF16)
    later_weights = (w_branch.reshape(depth, 3 * branch_dim, d), w_out, w_up, w_down)
    pool_w_b = pool_w.astype(BF16)

    gate_col = main_w
    q_col = main_w + 3 * d
    scale = jnp.concatenate([jnp.ones((q_col,), F32), jnp.full((attn_dim,), HEAD_DIM ** -0.5 * LOG2E, F32),
                             jnp.ones((n_in - main_w - attn_dim,), F32)])

    bias_tiles = _bias_tiles(rel_bias, ATT_BLOCK)

    tm = _tile(tok, 1024)
    tm_s = _tile(seq, 512)
    xf = x.reshape(tok, d)
    for l in range(depth):
        g = norm_gains[l]
        g0, g1, g2, g3 = (g[n:n + 1] for n in range(4))
        bias = jnp.concatenate([jnp.zeros((main_w,), F32), b_gate[l].reshape(3 * d), jnp.zeros((n_in - main_w,), F32)])
        proj, kiw = _projections(xf, g0, w_in_t, w_gate, w_kw_t, l, jnp.stack([scale, bias]), main_w, n_in,
                                 tm=_tile(tok, 2048), tn=512)
        y_attn, w_branch_b, w_out_b, w_up_b, w_down_b = _attention(proj, kiw, bias_tiles, batch, seq, q_col,
                                                                    l, later_weights)
        mixed = _mixer(proj, gate_col, y_attn, w_branch_b.reshape(1, 3, branch_dim, d), pool_w_b,
                       pool_scale.reshape(depth, 1, branch_dim), conv_w, l, seq, tm=tm_s, tn=1024)
        xf = _proj_norm_res(mixed, w_out_b[None], 0, xf, g1, tm=tm_s)
        xf = _mlp(xf, g2, g3, w_up_b[None], w_down_b[None], 0, tm=tm_s, th=1024)
    return xf.reshape(batch, seq, d)


def kernel(x, norm_gains, w_in, pool_w, pool_scale, conv_w, rel_bias, w_branch, w_gate, b_gate, w_out, w_up, w_down):
    return _forward(x, norm_gains, w_in, pool_w, pool_scale, conv_w, rel_bias, w_branch,
                    w_gate, b_gate, w_out, w_up, w_down)
```

```python
import functools
import math

import jax
import jax.numpy as jnp
import numpy as np
from jax import lax
from jax.experimental import pallas as pl
from jax.experimental.pallas import tpu as pltpu

F32 = jnp.float32
BF16 = jnp.bfloat16
I32 = jnp.int32
I16 = jnp.int16

NORM_EPS = 1e-6
POOL_WINDOWS = (2, 4, 8, 16)
POOL_GROUP_DIM = 256
POOL_HALO = 16
CONV_WIDTH = 3
N_HEADS = 8
N_KV_HEADS = 2
GROUP = N_HEADS // N_KV_HEADS
HEAD_DIM = 128
IDX_HEADS = 16
IDX_DIM = 64
TOPK_MAX = 256
NUM_BUCKETS = 32
MAX_EXACT = NUM_BUCKETS // 2
MAX_DISTANCE = 128
ATT_BLOCK = 256
IDX_PAD = 128
HALF_BITS = 16
INT_MIN = -(2 ** 31)
INT_MAX = 2 ** 31 - 1
NEG = -1e30
LOG2E = math.log2(math.e)
ONES_ROWS = 16
VMEM_LIMIT = 56 * 1024 * 1024
MXU_WIDTH = 256
BF16_SUBLANES = 16
NORM_ROWS = 256


def _bucket_thresholds():
    n = np.arange(MAX_EXACT, 8 * MAX_DISTANCE, dtype=np.int64)
    nf = n.astype(np.float32)
    large = MAX_EXACT + (np.log(nf / np.float32(MAX_EXACT)) / np.float32(math.log(MAX_DISTANCE / MAX_EXACT))
                         * np.float32(NUM_BUCKETS - MAX_EXACT)).astype(np.int32)
    large = np.minimum(large, NUM_BUCKETS - 1)
    thr = []
    for b in range(MAX_EXACT + 1, NUM_BUCKETS):
        thr.append(int(n[np.argmax(large >= b)]))
    return tuple(thr)


BUCKET_THRESHOLDS = _bucket_thresholds()


def _params(*sem):
    return pltpu.CompilerParams(dimension_semantics=sem, vmem_limit_bytes=VMEM_LIMIT)


def _rms(x, gain):
    ms = jnp.mean(x * x, axis=-1, keepdims=True)
    return x * lax.rsqrt(ms + NORM_EPS) * gain


def _dot_nt(a, b):
    return lax.dot_general(a, b, (((1,), (1,)), ((), ())), preferred_element_type=F32)


def _proj_body(x_ref, g_ref, wint_ref, wg_ref, wkwt_ref, post_ref, o_ref, okw_ref, h_ref, *, gate_lo, gate_hi):
    j = pl.program_id(1)
    is_gate = jnp.logical_and(j >= gate_lo, j < gate_hi)

    @pl.when(j == 0)
    def _():
        def norm_rows(r, carry):
            rows = pl.ds(pl.multiple_of(r * NORM_ROWS, NORM_ROWS), NORM_ROWS)
            h = _rms(x_ref[rows, :], g_ref[...]).astype(BF16)
            h_ref[rows, :] = h
            okw_ref[rows, :] = _dot_nt(h, wkwt_ref[...])
            return carry

        lax.fori_loop(0, x_ref.shape[0] // NORM_ROWS, norm_rows, 0)

    chunks = [slice(c, c + MXU_WIDTH) for c in range(0, o_ref.shape[-1], MXU_WIDTH)]

    @pl.when(jnp.logical_not(is_gate))
    def _():
        for cs in chunks:
            y = _dot_nt(h_ref[...], wint_ref[cs, :].astype(BF16))
            o_ref[:, cs] = (y * post_ref[0:1, cs]).astype(o_ref.dtype)

    @pl.when(is_gate)
    def _():
        for cs in chunks:
            y = jnp.dot(h_ref[...], wg_ref[:, cs].astype(BF16), preferred_element_type=F32) + post_ref[1:2, cs]
            o_ref[:, cs] = (0.5 * jnp.tanh(0.5 * y) + 0.5).astype(o_ref.dtype)


def _projections(x, gain, w_in_t, w_gate, w_kw_t, layer, post, n_main, n_in, *, tm, tn):
    tok, d = x.shape
    n_gate = w_gate.shape[-1]
    assert n_main % tn == 0 and n_in % tn == 0 and n_gate % tn == 0 and tm % NORM_ROWS == 0
    gate_lo = n_main // tn
    gate_blocks = n_gate // tn
    gate_hi = gate_lo + gate_blocks
    kw = w_kw_t.shape[1]
    return pl.pallas_call(
        functools.partial(_proj_body, gate_lo=gate_lo, gate_hi=gate_hi),
        grid=(tok // tm, (n_in + n_gate) // tn),
        in_specs=[
            pl.BlockSpec((tm, d), lambda i, j: (i, 0), pipeline_mode=pl.Buffered(1)),
            pl.BlockSpec((1, d), lambda i, j: (0, 0)),
            pl.BlockSpec((None, tn, d),
                         lambda i, j: (layer, jnp.where(j < gate_lo, j, jnp.maximum(j - gate_blocks, gate_lo)), 0)),
            pl.BlockSpec((None, d, tn), lambda i, j: (layer, 0, jnp.clip(j - gate_lo, 0, gate_blocks - 1))),
            pl.BlockSpec((None, kw, d), lambda i, j: (layer, 0, 0)),
            pl.BlockSpec((2, tn), lambda i, j: (0, j)),
        ],
        out_specs=[
            pl.BlockSpec((tm, tn), lambda i, j: (i, j)),
            pl.BlockSpec((tm, kw), lambda i, j: (i, 0)),
        ],
        out_shape=[
            jax.ShapeDtypeStruct((tok, n_in + n_gate), BF16),
            jax.ShapeDtypeStruct((tok, kw), F32),
        ],
        scratch_shapes=[pltpu.VMEM((tm, d), BF16)],
        compiler_params=_params("parallel", "arbitrary"),
        name="projections",
    )(x, gain, w_in_t, w_gate, w_kw_t, post)


def _bias_tiles_body(rel_ref, o_ref, *, t):
    row = lax.broadcasted_iota(I32, (t, t), 0)
    col = lax.broadcasted_iota(I32, (t, t), 1)
    for off in range(2):
        n = jnp.maximum(col - row + off * t, 0)
        large = jnp.full((t, t), MAX_EXACT, I32)
        for thr in BUCKET_THRESHOLDS:
            large = large + jnp.where(n >= thr, 1, 0)
        bucket = jnp.where(n < MAX_EXACT, n, large)
        for h in range(N_HEADS):
            val = jnp.zeros((t, t), F32)
            for b in range(NUM_BUCKETS):
                val = jnp.where(bucket == b, rel_ref[b, h], val)
            o_ref[h, off] = (val - rel_ref[NUM_BUCKETS - 1, h]) * LOG2E


def _bias_tiles(rel_bias, t):
    return pl.pallas_call(
        functools.partial(_bias_tiles_body, t=t),
        in_specs=[pl.BlockSpec(memory_space=pltpu.SMEM)],
        out_specs=pl.BlockSpec(memory_space=pltpu.VMEM),
        out_shape=jax.ShapeDtypeStruct((N_HEADS, 2, t, t), F32),
        compiler_params=pltpu.CompilerParams(vmem_limit_bytes=VMEM_LIMIT),
        name="bias_tiles",
    )(rel_bias)


def _attn_body(*refs, t, topk, n_casts):
    (q_ref, k_ref, v_ref, qia_ref, qib_ref, kiw_ref, bias_ref), refs = refs[:7], refs[7:]
    cast_src, refs = refs[:n_casts], refs[n_casts:]
    o_ref, cast_dst, refs = refs[0], refs[1:1 + n_casts], refs[1 + n_casts:]
    (sc_ref, sch_ref, thr_ref, vt_ref, kib_ref, qih_ref, mask_ref, alpha_ref, m_ref, acc_ref), refs = (
        refs[:10], refs[10:])
    s_ref, p_ref = refs[:N_HEADS], refs[N_HEADS:]

    for src, dst in zip(cast_src, cast_dst):
        dst[...] = src[...].astype(BF16)

    i = pl.program_id(1)
    nkb = vt_ref.shape[0]
    q0 = pl.multiple_of(i * t, t)
    key_pos = lax.broadcasted_iota(I32, (t, 1), 0)
    qry_pos = lax.broadcasted_iota(I32, (1, t), 1)
    causal = key_pos <= qry_pos

    @pl.when(i == 0)
    def _():
        for kb in range(nkb):
            rows = slice(kb * t, (kb + 1) * t)
            vb = v_ref[rows, :].astype(F32)
            for g in range(N_KV_HEADS):
                vt_ref[kb, g, 0:HEAD_DIM, :] = vb[:, g * HEAD_DIM:(g + 1) * HEAD_DIM].T.astype(BF16)
                vt_ref[kb, g, HEAD_DIM:, :] = jnp.ones((vt_ref.shape[2] - HEAD_DIM, t), BF16)
            kib_ref[rows, :] = kiw_ref[rows, 0:IDX_DIM].astype(BF16)

    half = IDX_HEADS // 2
    for h in range(IDX_HEADS):
        src = qia_ref if h < half else qib_ref
        qih_ref[h] = src[:, (h % half) * IDX_DIM:(h % half + 1) * IDX_DIM]
    w_t = kiw_ref[pl.ds(q0, t), :].T[IDX_DIM:IDX_DIM + IDX_HEADS, :] * (IDX_HEADS ** -0.5)

    def score_block(kb, diag):
        kblk = kib_ref[pl.ds(pl.multiple_of(kb * t, t), t), :]
        acc = jnp.zeros((t, t), F32)
        for h in range(IDX_HEADS):
            acc = acc + w_t[h:h + 1, :] * jnp.maximum(_dot_nt(kblk, qih_ref[h]), 0.0)
        if diag:
            acc = jnp.where(causal, acc, -jnp.inf)
        bits = pltpu.bitcast(acc, I32)
        key = bits ^ ((bits >> 31) & INT_MAX)
        sc_ref[kb] = key
        sch_ref[kb] = (key >> HALF_BITS).astype(I16)

    def score_loop(kb, carry):
        score_block(kb, False)
        return carry

    lax.fori_loop(0, i, score_loop, 0)
    score_block(i, True)

    @pl.when(i % 2 == 0)
    def _():
        sc_ref[i + 1] = jnp.full((t, t), INT_MIN, I32)
        sch_ref[i + 1] = jnp.full((t, t), INT_MIN >> HALF_BITS, I16)

    @pl.when(q0 + t > topk)
    def _():
        needs = q0 + qry_pos >= topk
        rows = t // BF16_SUBLANES
        half_lo, half_hi = INT_MIN >> HALF_BITS, -(INT_MIN >> HALF_BITS)

        def halving(_, state):
            lo, hi, cnt_lo, cnt_hi = state
            mid = (lo & hi) + ((lo ^ hi) >> 1)
            mid_c = mid.astype(I16)

            def count(pair, c):
                for kb in (2 * pair, 2 * pair + 1):
                    ge = jnp.where(sch_ref[kb] >= mid_c, jnp.int16(1), jnp.int16(0))
                    for r in range(rows):
                        c = c + ge[r * BF16_SUBLANES:(r + 1) * BF16_SUBLANES]
                return c

            c = lax.fori_loop(0, (i + 2) // 2, count, jnp.zeros((BF16_SUBLANES, t), I16))
            c = jnp.sum(c.astype(I32), axis=0, keepdims=True)
            fresh = mid > lo
            up = jnp.logical_and(fresh, c >= topk)
            down = jnp.logical_and(fresh, c < topk)
            return (jnp.where(up, mid, lo), jnp.where(down, mid, hi),
                    jnp.where(up, c, cnt_lo), jnp.where(down, c, cnt_hi))

        band, _, cnt_lo, cnt_hi = lax.fori_loop(
            0, HALF_BITS, halving,
            (jnp.full((1, t), half_lo, I32), jnp.full((1, t), half_hi, I32),
             jnp.full((1, t), topk + 1, I32), jnp.zeros((1, t), I32)))

        def low_halves(kb, carry):
            key = sc_ref[kb]
            high = key >> HALF_BITS
            low = (key & (2 * half_hi - 1)) + half_lo
            sch_ref[kb] = jnp.where(high == band, low, jnp.where(high > band, half_hi - 1, half_lo)).astype(I16)
            return carry

        lax.fori_loop(0, i + 1, low_halves, 0)
        low, _, cnt_lo, cnt_hi = lax.fori_loop(
            0, HALF_BITS, halving,
            (jnp.full((1, t), half_lo, I32), jnp.full((1, t), half_hi, I32), cnt_lo, cnt_hi))
        lo = (band << HALF_BITS) + (low - half_lo)
        thr_ref[...] = jnp.where(needs, lo, INT_MIN)

        tied = jnp.where(jnp.logical_and(needs, cnt_lo > topk), 1, 0)

        @pl.when(jnp.max(tied) > 0)
        def _():
            wanted = (topk - cnt_hi).astype(F32)
            tied_f = tied.astype(F32)
            earlier = jnp.where(lax.broadcasted_iota(I32, (t, t), 1) < lax.broadcasted_iota(I32, (t, t), 0),
                                1.0, 0.0).astype(BF16)

            def demote(kb, seen):
                key = sc_ref[kb]
                is_tie = jnp.where(key == lo, tied_f, 0.0)
                rank = seen + jnp.dot(earlier, is_tie.astype(BF16), preferred_element_type=F32)
                drop = is_tie * jnp.where(rank >= wanted, 1.0, 0.0)
                sc_ref[kb] = jnp.where(drop > 0.0, lo - 1, key)
                return seen + jnp.sum(is_tie, axis=0, keepdims=True)

            lax.fori_loop(0, i + 1, demote, jnp.zeros((1, t), F32))

    @pl.when(q0 + t <= topk)
    def _():
        thr_ref[...] = jnp.full((1, t), INT_MIN, I32)

    m_ref[...] = jnp.full(m_ref.shape, NEG, F32)
    acc_ref[...] = jnp.zeros(acc_ref.shape, F32)
    thr = thr_ref[...]

    def attn_block(kb, mode):
        ks = pl.ds(pl.multiple_of(kb * t, t), t)
        mask = jnp.where(sc_ref[kb] >= thr, 0.0, NEG)
        if mode == 0:
            mask = jnp.where(causal, mask, NEG)
        mask_ref[...] = mask

        def logits(h):
            g = h // GROUP
            s_ref[h][...] = _dot_nt(k_ref[ks, g * HEAD_DIM:(g + 1) * HEAD_DIM],
                                    q_ref[:, h * HEAD_DIM:(h + 1) * HEAD_DIM])

        def numerators(h):
            s = s_ref[h][...] + mask_ref[...]
            if mode != "far":
                s = s + bias_ref[h, mode]
            m_old = m_ref[h]
            m_new = jnp.maximum(m_old, jnp.max(s, axis=0, keepdims=True))
            p_ref[h][...] = jnp.exp2(s - m_new).astype(BF16)
            alpha_ref[h] = jnp.exp2(m_old - m_new)
            m_ref[h] = m_new

        def weighted_values(h):
            pv = jnp.dot(vt_ref[kb, h // GROUP], p_ref[h][...], preferred_element_type=F32)
            acc_ref[h] = alpha_ref[h] * acc_ref[h] + pv

        for stage in (logits, numerators, weighted_values):
            for h in range(N_HEADS):
                stage(h)

    def far_loop(kb, carry):
        attn_block(kb, "far")
        return carry

    lax.fori_loop(0, jnp.maximum(i - 1, 0), far_loop, 0)

    @pl.when(i >= 1)
    def _():
        attn_block(i - 1, 1)

    attn_block(i, 0)

    for h in range(N_HEADS):
        acc = acc_ref[h]
        out_t = acc[0:HEAD_DIM, :] / acc[HEAD_DIM:HEAD_DIM + 1, :]
        o_ref[:, h * HEAD_DIM:(h + 1) * HEAD_DIM] = out_t.T.astype(o_ref.dtype)


def _attention(proj, kiw, bias_tiles, batch, seq, q_col, layer, casts):
    t = ATT_BLOCK
    nq = seq // t
    steps = batch * nq
    cast_in, cast_out, cast_shapes = [], [], []
    for w in casts:
        rows = w.shape[1] // steps
        assert rows * steps == w.shape[1] and rows % BF16_SUBLANES == 0
        cast_in.append(pl.BlockSpec((None, rows, w.shape[2]), lambda b, i: (layer, b * nq + i, 0)))
        cast_out.append(pl.BlockSpec((rows, w.shape[2]), lambda b, i: (b * nq + i, 0)))
        cast_shapes.append(jax.ShapeDtypeStruct(w.shape[1:], BF16))
    topk = min(TOPK_MAX, seq // 4)
    attn_dim = N_HEADS * HEAD_DIM
    kv_dim = N_KV_HEADS * HEAD_DIM
    qi_half = IDX_HEADS * IDX_DIM // 2
    k_col = q_col + attn_dim
    qi_col = k_col + 2 * kv_dim
    assert q_col % attn_dim == 0 and k_col % kv_dim == 0 and qi_col % qi_half == 0
    return pl.pallas_call(
        functools.partial(_attn_body, t=t, topk=topk, n_casts=len(casts)),
        grid=(batch, nq),
        in_specs=[
            pl.BlockSpec((t, attn_dim), lambda b, i: (b * nq + i, q_col // attn_dim)),
            pl.BlockSpec((seq, kv_dim), lambda b, i: (b, k_col // kv_dim)),
            pl.BlockSpec((seq, kv_dim), lambda b, i: (b, k_col // kv_dim + 1)),
            pl.BlockSpec((t, qi_half), lambda b, i: (b * nq + i, qi_col // qi_half)),
            pl.BlockSpec((t, qi_half), lambda b, i: (b * nq + i, qi_col // qi_half + 1)),
            pl.BlockSpec((seq, IDX_PAD), lambda b, i: (b, 0)),
            pl.BlockSpec((N_HEADS, 2, t, t), lambda b, i: (0, 0, 0, 0)),
        ] + cast_in,
        out_specs=[pl.BlockSpec((t, attn_dim), lambda b, i: (b * nq + i, 0))] + cast_out,
        out_shape=[jax.ShapeDtypeStruct((batch * seq, attn_dim), BF16)] + cast_shapes,
        scratch_shapes=[
            pltpu.VMEM((nq + 1, t, t), I32),
            pltpu.VMEM((nq + 1, t, t), I16),
            pltpu.VMEM((1, t), I32),
            pltpu.VMEM((nq, N_KV_HEADS, HEAD_DIM + ONES_ROWS, t), BF16),
            pltpu.VMEM((seq, IDX_DIM), BF16),
            pltpu.VMEM((IDX_HEADS, t, IDX_DIM), BF16),
            pltpu.VMEM((t, t), F32),
            pltpu.VMEM((N_HEADS, 1, t), F32),
            pltpu.VMEM((N_HEADS, 1, t), F32),
            pltpu.VMEM((N_HEADS, HEAD_DIM + ONES_ROWS, t), F32),
        ] + [pltpu.VMEM((t, t), F32)] * N_HEADS + [pltpu.VMEM((t, t), BF16)] * N_HEADS,
        compiler_params=_params("parallel", "arbitrary"),
        name="sparse_attention",
    )(proj, proj, proj, proj, proj, kiw, bias_tiles, *casts)


def _mixer_body(main_ref, halo_ref, ya_ref, g0_ref, g1_ref, g2_ref, wb_ref, pw_ref, ps_ref, cw_ref,
                o_ref, y_ref, acc_ref, *, tm, tn, seq, pool_dim, conv_dim):
    pos0 = (pl.program_id(0) * tm) % seq
    pos = lax.broadcasted_iota(I32, (tm, 1), 0) + pos0
    keep = jnp.where(pos0 == 0, 0.0, 1.0)
    n_out = o_ref.shape[-1]

    def branch(n, y, c0):
        return g_refs[n][:, c0:c0 + tn].astype(F32) * jnp.dot(y, wb_ref[n, :, c0:c0 + tn], preferred_element_type=F32)

    g_refs = (g0_ref, g1_ref, g2_ref)
    for c0 in range(0, n_out, tn):
        acc_ref[:, c0:c0 + tn] = branch(2, ya_ref[...], c0)

    def ext(c0, c1):
        halo = halo_ref[:, c0:c1].astype(F32) * keep
        return jnp.concatenate([halo, main_ref[:, c0:c1].astype(F32)], axis=0)

    for g, win in enumerate(POOL_WINDOWS):
        c0 = g * POOL_GROUP_DIM
        u = ext(c0, c0 + POOL_GROUP_DIM)
        s, sh = u, 1
        while sh < win:
            s = s + pltpu.roll(s, sh, axis=0)
            sh *= 2
        cnt = jnp.minimum(pos + 1, win).astype(F32)
        dlt = s[POOL_HALO:] / cnt - u[POOL_HALO:]
        yp = jnp.dot(dlt.astype(BF16), pw_ref[g], preferred_element_type=F32)
        y_ref[0, :, c0:c0 + POOL_GROUP_DIM] = (yp * ps_ref[:, c0:c0 + POOL_GROUP_DIM]).astype(BF16)

    z = ext(pool_dim, pool_dim + conv_dim) * ext(pool_dim + conv_dim, pool_dim + 2 * conv_dim)
    yc = z[POOL_HALO:] * cw_ref[CONV_WIDTH - 1:CONV_WIDTH, :]
    for tap in range(1, CONV_WIDTH):
        yc = yc + pltpu.roll(z, tap, axis=0)[POOL_HALO:] * cw_ref[CONV_WIDTH - 1 - tap:CONV_WIDTH - tap, :]
    gate_b = main_ref[:, pool_dim + 2 * conv_dim:pool_dim + 3 * conv_dim].astype(F32)
    y_ref[1] = (gate_b * yc).astype(BF16)

    for c0 in range(0, n_out, tn):
        mixed = acc_ref[:, c0:c0 + tn] + branch(0, y_ref[0], c0) + branch(1, y_ref[1], c0)
        o_ref[:, c0:c0 + tn] = mixed.astype(o_ref.dtype)


def _mixer(proj, gate_col, y_attn, w_branch, pool_w, pool_scale, conv_w, layer, seq, *, tm, tn):
    main = gates = proj
    tok = proj.shape[0]
    branch_dim = y_attn.shape[-1]
    width = 4 * branch_dim
    d = w_branch.shape[-1]
    assert gate_col % d == 0 and d % tn == 0
    gj = gate_col // d
    halo_blocks = tm // POOL_HALO
    resident = dict(pipeline_mode=pl.Buffered(1))
    return pl.pallas_call(
        functools.partial(_mixer_body, tm=tm, tn=tn, seq=seq, pool_dim=branch_dim, conv_dim=branch_dim),
        grid=(tok // tm,),
        in_specs=[
            pl.BlockSpec((tm, width), lambda i: (i, 0)),
            pl.BlockSpec((POOL_HALO, width), lambda i: (jnp.maximum(i * halo_blocks - 1, 0), 0)),
            pl.BlockSpec((tm, branch_dim), lambda i: (i, 0)),
            pl.BlockSpec((tm, d), lambda i: (i, gj)),
            pl.BlockSpec((tm, d), lambda i: (i, gj + 1)),
            pl.BlockSpec((tm, d), lambda i: (i, gj + 2)),
            pl.BlockSpec((None, 3, branch_dim, d), lambda i: (0, 0, 0, 0), **resident),
            pl.BlockSpec((None,) + pool_w.shape[1:], lambda i: (layer, 0, 0, 0), **resident),
            pl.BlockSpec((None, 1, branch_dim), lambda i: (layer, 0, 0)),
            pl.BlockSpec((None, CONV_WIDTH, branch_dim), lambda i: (layer, 0, 0)),
        ],
        out_specs=pl.BlockSpec((tm, d), lambda i: (i, 0)),
        out_shape=jax.ShapeDtypeStruct((tok, d), BF16),
        scratch_shapes=[pltpu.VMEM((2, tm, branch_dim), BF16), pltpu.VMEM((tm, d), F32)],
        compiler_params=_params("parallel"),
        name="mixer_branches",
    )(main, main, y_attn, gates, gates, gates, w_branch, pool_w, pool_scale, conv_w)


def _proj_norm_res_body(a_ref, w_ref, x_ref, g_ref, o_ref):
    half = a_ref.shape[0] // 2
    for rows in (slice(0, half), slice(half, 2 * half)):
        m = jnp.dot(a_ref[rows, :], w_ref[...], preferred_element_type=F32)
        o_ref[rows, :] = x_ref[rows, :] + _rms(m, g_ref[...])


def _proj_norm_res(a, w, layer, x, gain, *, tm):
    tok, d = x.shape
    k = a.shape[-1]
    return pl.pallas_call(
        _proj_norm_res_body,
        grid=(tok // tm,),
        in_specs=[
            pl.BlockSpec((tm, k), lambda i: (i, 0)),
            pl.BlockSpec((None, k, d), lambda i: (layer, 0, 0), pipeline_mode=pl.Buffered(1)),
            pl.BlockSpec((tm, d), lambda i: (i, 0)),
            pl.BlockSpec((1, d), lambda i: (0, 0)),
        ],
        out_specs=pl.BlockSpec((tm, d), lambda i: (i, 0)),
        out_shape=jax.ShapeDtypeStruct((tok, d), F32),
        compiler_params=_params("parallel"),
        name="out_proj",
    )(a, w, x, gain)


def _mlp_body(x_ref, gpre_ref, gpost_ref, wu_ref, wd_ref, o_ref, h_ref, acc_ref):
    j = pl.program_id(1)

    @pl.when(j == 0)
    def _():
        h_ref[...] = _rms(x_ref[...], gpre_ref[...]).astype(BF16)
        acc_ref[...] = jnp.zeros(acc_ref.shape, F32)

    hid = jnp.dot(h_ref[...], wu_ref[...], preferred_element_type=F32)
    hid = jnp.square(jnp.maximum(hid, 0.0)).astype(BF16)
    acc_ref[...] += jnp.dot(hid, wd_ref[...], preferred_element_type=F32)

    @pl.when(j == pl.num_programs(1) - 1)
    def _():
        o_ref[...] = x_ref[...] + _rms(acc_ref[...], gpost_ref[...])


def _mlp(x, gpre, gpost, w_up, w_down, layer, *, tm, th):
    tok, d = x.shape
    hidden = w_up.shape[-1]
    return pl.pallas_call(
        _mlp_body,
        grid=(tok // tm, hidden // th),
        in_specs=[
            pl.BlockSpec((tm, d), lambda i, j: (i, 0)),
            pl.BlockSpec((1, d), lambda i, j: (0, 0)),
            pl.BlockSpec((1, d), lambda i, j: (0, 0)),
            pl.BlockSpec((None, d, th), lambda i, j: (layer, 0, j)),
            pl.BlockSpec((None, th, d), lambda i, j: (layer, j, 0)),
        ],
        out_specs=pl.BlockSpec((tm, d), lambda i, j: (i, 0)),
        out_shape=jax.ShapeDtypeStruct((tok, d), F32),
        scratch_shapes=[pltpu.VMEM((tm, d), BF16), pltpu.VMEM((tm, d), F32)],
        compiler_params=_params("parallel", "arbitrary"),
        name="mlp",
    )(x, gpre, gpost, w_up, w_down)


def _tile(n, pref):
    t = min(n, pref)
    assert n % t == 0, (n, pref)
    return t


@jax.jit
def _forward(x, norm_gains, w_in, pool_w, pool_scale, conv_w, rel_bias, w_branch,
             w_gate, b_gate, w_out, w_up, w_down):
    batch, seq, d = x.shape
    depth = w_in.shape[0]
    tok = batch * seq
    branch_dim = d // 2
    attn_dim = N_HEADS * HEAD_DIM
    kv_dim = N_KV_HEADS * HEAD_DIM
    qi_dim = IDX_HEADS * IDX_DIM
    assert branch_dim == attn_dim == len(POOL_WINDOWS) * POOL_GROUP_DIM
    assert seq % ATT_BLOCK == 0 and ATT_BLOCK >= MAX_DISTANCE

    main_w = 4 * branch_dim
    n_in = main_w + attn_dim + 2 * kv_dim + qi_dim
    kw_w = IDX_DIM + IDX_HEADS
    assert w_in.shape[-1] == n_in + kw_w
    w_in_t = jnp.swapaxes(w_in, 1, 2)
    w_kw_t = jnp.pad(w_in_t[:, n_in:, :], ((0, 0), (0, IDX_PAD - kw_w), (0, 0))).astype(BF16)
    later_weights = (w_branch.reshape(depth, 3 * branch_dim, d), w_out, w_up, w_down)
    pool_w_b = pool_w.astype(BF16)

    gate_col = main_w
    q_col = main_w + 3 * d
    scale = jnp.concatenate([jnp.ones((q_col,), F32), jnp.full((attn_dim,), HEAD_DIM ** -0.5 * LOG2E, F32),
                             jnp.ones((n_in - main_w - attn_dim,), F32)])

    bias_tiles = _bias_tiles(rel_bias, ATT_BLOCK)

    tm = _tile(tok, 1024)
    tm_s = _tile(seq, 512)
    xf = x.reshape(tok, d)
    for l in range(depth):
        g = norm_gains[l]
        g0, g1, g2, g3 = (g[n:n + 1] for n in range(4))
        bias = jnp.concatenate([jnp.zeros((main_w,), F32), b_gate[l].reshape(3 * d), jnp.zeros((n_in - main_w,), F32)])
        proj, kiw = _projections(xf, g0, w_in_t, w_gate, w_kw_t, l, jnp.stack([scale, bias]), main_w, n_in,
                                 tm=_tile(tok, 2048), tn=512)
        y_attn, w_branch_b, w_out_b, w_up_b, w_down_b = _attention(proj, kiw, bias_tiles, batch, seq, q_col,
                                                                    l, later_weights)
        mixed = _mixer(proj, gate_col, y_attn, w_branch_b.reshape(1, 3, branch_dim, d), pool_w_b,
                       pool_scale.reshape(depth, 1, branch_dim), conv_w, l, seq, tm=tm_s, tn=1024)
        xf = _proj_norm_res(mixed, w_out_b[None], 0, xf, g1, tm=tm)
        xf = _mlp(xf, g2, g3, w_up_b[None], w_down_b[None], 0, tm=tm_s, th=1024)
    return xf.reshape(batch, seq, d)


def kernel(x, norm_gains, w_in, pool_w, pool_scale, conv_w, rel_bias, w_branch, w_gate, b_gate, w_out, w_up, w_down):
    return _forward(x, norm_gains, w_in, pool_w, pool_scale, conv_w, rel_bias, w_branch,
                    w_gate, b_gate, w_out, w_up, w_down)
```

```python
import functools
import math

import jax
import jax.numpy as jnp
import numpy as np
from jax import lax
from jax.experimental import pallas as pl
from jax.experimental.pallas import tpu as pltpu

F32 = jnp.float32
BF16 = jnp.bfloat16
I32 = jnp.int32
I16 = jnp.int16

NORM_EPS = 1e-6
POOL_WINDOWS = (2, 4, 8, 16)
POOL_GROUP_DIM = 256
POOL_HALO = 16
CONV_WIDTH = 3
N_HEADS = 8
N_KV_HEADS = 2
GROUP = N_HEADS // N_KV_HEADS
HEAD_DIM = 128
IDX_HEADS = 16
IDX_DIM = 64
TOPK_MAX = 256
NUM_BUCKETS = 32
MAX_EXACT = NUM_BUCKETS // 2
MAX_DISTANCE = 128
ATT_BLOCK = 256
IDX_PAD = 128
HALF_BITS = 16
INT_MIN = -(2 ** 31)
INT_MAX = 2 ** 31 - 1
NEG = -1e30
LOG2E = math.log2(math.e)
ONES_ROWS = 16
VMEM_LIMIT = 56 * 1024 * 1024
MXU_WIDTH = 256
BF16_SUBLANES = 16
NORM_ROWS = 256


def _bucket_thresholds():
    n = np.arange(MAX_EXACT, 8 * MAX_DISTANCE, dtype=np.int64)
    nf = n.astype(np.float32)
    large = MAX_EXACT + (np.log(nf / np.float32(MAX_EXACT)) / np.float32(math.log(MAX_DISTANCE / MAX_EXACT))
                         * np.float32(NUM_BUCKETS - MAX_EXACT)).astype(np.int32)
    large = np.minimum(large, NUM_BUCKETS - 1)
    thr = []
    for b in range(MAX_EXACT + 1, NUM_BUCKETS):
        thr.append(int(n[np.argmax(large >= b)]))
    return tuple(thr)


BUCKET_THRESHOLDS = _bucket_thresholds()


def _params(*sem):
    return pltpu.CompilerParams(dimension_semantics=sem, vmem_limit_bytes=VMEM_LIMIT)


def _rms(x, gain):
    ms = jnp.mean(x * x, axis=-1, keepdims=True)
    return x * lax.rsqrt(ms + NORM_EPS) * gain


def _dot_nt(a, b):
    return lax.dot_general(a, b, (((1,), (1,)), ((), ())), preferred_element_type=F32)


def _proj_body(x_ref, g_ref, wint_ref, wg_ref, wkwt_ref, post_ref, o_ref, okw_ref, h_ref, *, gate_lo, gate_hi):
    j = pl.program_id(1)
    is_gate = jnp.logical_and(j >= gate_lo, j < gate_hi)

    @pl.when(j == 0)
    def _():
        def norm_rows(r, carry):
            rows = pl.ds(pl.multiple_of(r * NORM_ROWS, NORM_ROWS), NORM_ROWS)
            h = _rms(x_ref[rows, :], g_ref[...]).astype(BF16)
            h_ref[rows, :] = h
            okw_ref[rows, :] = _dot_nt(h, wkwt_ref[...])
            return carry

        lax.fori_loop(0, x_ref.shape[0] // NORM_ROWS, norm_rows, 0)

    chunks = [slice(c, c + MXU_WIDTH) for c in range(0, o_ref.shape[-1], MXU_WIDTH)]

    @pl.when(jnp.logical_not(is_gate))
    def _():
        for cs in chunks:
            y = _dot_nt(h_ref[...], wint_ref[cs, :].astype(BF16))
            o_ref[:, cs] = (y * post_ref[0:1, cs]).astype(o_ref.dtype)

    @pl.when(is_gate)
    def _():
        for cs in chunks:
            y = jnp.dot(h_ref[...], wg_ref[:, cs].astype(BF16), preferred_element_type=F32) + post_ref[1:2, cs]
            o_ref[:, cs] = (0.5 * jnp.tanh(0.5 * y) + 0.5).astype(o_ref.dtype)


def _projections(x, gain, w_in_t, w_gate, w_kw_t, layer, post, n_main, n_in, *, tm, tn):
    tok, d = x.shape
    n_gate = w_gate.shape[-1]
    assert n_main % tn == 0 and n_in % tn == 0 and n_gate % tn == 0 and tm % NORM_ROWS == 0
    gate_lo = n_main // tn
    gate_blocks = n_gate // tn
    gate_hi = gate_lo + gate_blocks
    kw = w_kw_t.shape[1]
    return pl.pallas_call(
        functools.partial(_proj_body, gate_lo=gate_lo, gate_hi=gate_hi),
        grid=(tok // tm, (n_in + n_gate) // tn),
        in_specs=[
            pl.BlockSpec((tm, d), lambda i, j: (i, 0), pipeline_mode=pl.Buffered(1)),
            pl.BlockSpec((1, d), lambda i, j: (0, 0)),
            pl.BlockSpec((None, tn, d),
                         lambda i, j: (layer, jnp.where(j < gate_lo, j, jnp.maximum(j - gate_blocks, gate_lo)), 0)),
            pl.BlockSpec((None, d, tn), lambda i, j: (layer, 0, jnp.clip(j - gate_lo, 0, gate_blocks - 1))),
            pl.BlockSpec((None, kw, d), lambda i, j: (layer, 0, 0)),
            pl.BlockSpec((2, tn), lambda i, j: (0, j)),
        ],
        out_specs=[
            pl.BlockSpec((tm, tn), lambda i, j: (i, j)),
            pl.BlockSpec((tm, kw), lambda i, j: (i, 0)),
        ],
        out_shape=[
            jax.ShapeDtypeStruct((tok, n_in + n_gate), BF16),
            jax.ShapeDtypeStruct((tok, kw), F32),
        ],
        scratch_shapes=[pltpu.VMEM((tm, d), BF16)],
        compiler_params=_params("parallel", "arbitrary"),
        name="projections",
    )(x, gain, w_in_t, w_gate, w_kw_t, post)


def _bias_tiles_body(rel_ref, o_ref, *, t):
    row = lax.broadcasted_iota(I32, (t, t), 0)
    col = lax.broadcasted_iota(I32, (t, t), 1)
    for off in range(2):
        n = jnp.maximum(col - row + off * t, 0)
        large = jnp.full((t, t), MAX_EXACT, I32)
        for thr in BUCKET_THRESHOLDS:
            large = large + jnp.where(n >= thr, 1, 0)
        bucket = jnp.where(n < MAX_EXACT, n, large)
        for h in range(N_HEADS):
            val = jnp.zeros((t, t), F32)
            for b in range(NUM_BUCKETS):
                val = jnp.where(bucket == b, rel_ref[b, h], val)
            o_ref[h, off] = (val - rel_ref[NUM_BUCKETS - 1, h]) * LOG2E


def _bias_tiles(rel_bias, t):
    return pl.pallas_call(
        functools.partial(_bias_tiles_body, t=t),
        in_specs=[pl.BlockSpec(memory_space=pltpu.SMEM)],
        out_specs=pl.BlockSpec(memory_space=pltpu.VMEM),
        out_shape=jax.ShapeDtypeStruct((N_HEADS, 2, t, t), F32),
        compiler_params=pltpu.CompilerParams(vmem_limit_bytes=VMEM_LIMIT),
        name="bias_tiles",
    )(rel_bias)


def _attn_body(*refs, t, topk, n_casts):
    (q_ref, k_ref, v_ref, qia_ref, qib_ref, kiw_ref, bias_ref), refs = refs[:7], refs[7:]
    cast_src, refs = refs[:n_casts], refs[n_casts:]
    o_ref, cast_dst, refs = refs[0], refs[1:1 + n_casts], refs[1 + n_casts:]
    (sc_ref, sch_ref, thr_ref, vt_ref, kib_ref, qih_ref, mask_ref, alpha_ref, m_ref, acc_ref), refs = (
        refs[:10], refs[10:])
    s_ref, p_ref = refs[:N_HEADS], refs[N_HEADS:]

    for src, dst in zip(cast_src, cast_dst):
        dst[...] = src[...].astype(BF16)

    i = pl.program_id(1)
    nkb = vt_ref.shape[0]
    q0 = pl.multiple_of(i * t, t)
    key_pos = lax.broadcasted_iota(I32, (t, 1), 0)
    qry_pos = lax.broadcasted_iota(I32, (1, t), 1)
    causal = key_pos <= qry_pos

    @pl.when(i == 0)
    def _():
        for kb in range(nkb):
            rows = slice(kb * t, (kb + 1) * t)
            vb = v_ref[rows, :].astype(F32)
            for g in range(N_KV_HEADS):
                vt_ref[kb, g, 0:HEAD_DIM, :] = vb[:, g * HEAD_DIM:(g + 1) * HEAD_DIM].T.astype(BF16)
                vt_ref[kb, g, HEAD_DIM:, :] = jnp.ones((vt_ref.shape[2] - HEAD_DIM, t), BF16)
            kib_ref[rows, :] = kiw_ref[rows, 0:IDX_DIM].astype(BF16)

    half = IDX_HEADS // 2
    for h in range(IDX_HEADS):
        src = qia_ref if h < half else qib_ref
        qih_ref[h] = src[:, (h % half) * IDX_DIM:(h % half + 1) * IDX_DIM]
    w_t = kiw_ref[pl.ds(q0, t), :].T[IDX_DIM:IDX_DIM + IDX_HEADS, :] * (IDX_HEADS ** -0.5)

    def score_block(kb, diag):
        kblk = kib_ref[pl.ds(pl.multiple_of(kb * t, t), t), :]
        acc = jnp.zeros((t, t), F32)
        for h in range(IDX_HEADS):
            acc = acc + w_t[h:h + 1, :] * jnp.maximum(_dot_nt(kblk, qih_ref[h]), 0.0)
        if diag:
            acc = jnp.where(causal, acc, -jnp.inf)
        bits = pltpu.bitcast(acc, I32)
        bits = jnp.where(bits == INT_MIN, 0, bits)
        key = bits ^ ((bits >> 31) & INT_MAX)
        sc_ref[kb] = key
        sch_ref[kb] = (key >> HALF_BITS).astype(I16)

    def score_loop(kb, carry):
        score_block(kb, False)
        return carry

    lax.fori_loop(0, i, score_loop, 0)
    score_block(i, True)

    @pl.when(i % 2 == 0)
    def _():
        sc_ref[i + 1] = jnp.full((t, t), INT_MIN, I32)
        sch_ref[i + 1] = jnp.full((t, t), INT_MIN >> HALF_BITS, I16)

    @pl.when(q0 + t > topk)
    def _():
        needs = q0 + qry_pos >= topk
        rows = t // BF16_SUBLANES
        half_lo, half_hi = INT_MIN >> HALF_BITS, -(INT_MIN >> HALF_BITS)

        def halving(_, state):
            lo, hi, cnt_lo, cnt_hi = state
            mid = (lo & hi) + ((lo ^ hi) >> 1)
            mid_c = mid.astype(I16)

            def count(pair, c):
                for kb in (2 * pair, 2 * pair + 1):
                    ge = jnp.where(sch_ref[kb] >= mid_c, jnp.int16(1), jnp.int16(0))
                    for r in range(rows):
                        c = c + ge[r * BF16_SUBLANES:(r + 1) * BF16_SUBLANES]
                return c

            c = lax.fori_loop(0, (i + 2) // 2, count, jnp.zeros((BF16_SUBLANES, t), I16))
            c = jnp.sum(c.astype(I32), axis=0, keepdims=True)
            fresh = mid > lo
            up = jnp.logical_and(fresh, c >= topk)
            down = jnp.logical_and(fresh, c < topk)
            return (jnp.where(up, mid, lo), jnp.where(down, mid, hi),
                    jnp.where(up, c, cnt_lo), jnp.where(down, c, cnt_hi))

        band, _, cnt_lo, cnt_hi = lax.fori_loop(
            0, HALF_BITS, halving,
            (jnp.full((1, t), half_lo, I32), jnp.full((1, t), half_hi, I32),
             jnp.full((1, t), topk + 1, I32), jnp.zeros((1, t), I32)))

        def low_halves(kb, carry):
            key = sc_ref[kb]
            high = key >> HALF_BITS
            low = (key & (2 * half_hi - 1)) + half_lo
            sch_ref[kb] = jnp.where(high == band, low, jnp.where(high > band, half_hi - 1, half_lo)).astype(I16)
            return carry

        lax.fori_loop(0, i + 1, low_halves, 0)
        low, _, cnt_lo, cnt_hi = lax.fori_loop(
            0, HALF_BITS, halving,
            (jnp.full((1, t), half_lo, I32), jnp.full((1, t), half_hi, I32), cnt_lo, cnt_hi))
        lo = (band << HALF_BITS) + (low - half_lo)
        thr_ref[...] = jnp.where(needs, lo, INT_MIN)

        tied = jnp.where(jnp.logical_and(needs, cnt_lo > topk), 1, 0)

        @pl.when(jnp.max(tied) > 0)
        def _():
            wanted = (topk - cnt_hi).astype(F32)
            tied_f = tied.astype(F32)
            earlier = jnp.where(lax.broadcasted_iota(I32, (t, t), 1) < lax.broadcasted_iota(I32, (t, t), 0),
                                1.0, 0.0).astype(BF16)

            def demote(kb, seen):
                key = sc_ref[kb]
                is_tie = jnp.where(key == lo, tied_f, 0.0)
                rank = seen + jnp.dot(earlier, is_tie.astype(BF16), preferred_element_type=F32)
                drop = is_tie * jnp.where(rank >= wanted, 1.0, 0.0)
                sc_ref[kb] = jnp.where(drop > 0.0, lo - 1, key)
                return seen + jnp.sum(is_tie, axis=0, keepdims=True)

            lax.fori_loop(0, i + 1, demote, jnp.zeros((1, t), F32))

    @pl.when(q0 + t <= topk)
    def _():
        thr_ref[...] = jnp.full((1, t), INT_MIN, I32)

    m_ref[...] = jnp.full(m_ref.shape, NEG, F32)
    acc_ref[...] = jnp.zeros(acc_ref.shape, F32)
    thr = thr_ref[...]

    def attn_block(kb, mode):
        ks = pl.ds(pl.multiple_of(kb * t, t), t)
        mask = jnp.where(sc_ref[kb] >= thr, 0.0, NEG)
        if mode == 0:
            mask = jnp.where(causal, mask, NEG)
        mask_ref[...] = mask

        def logits(h):
            g = h // GROUP
            s_ref[h][...] = _dot_nt(k_ref[ks, g * HEAD_DIM:(g + 1) * HEAD_DIM],
                                    q_ref[:, h * HEAD_DIM:(h + 1) * HEAD_DIM])

        def numerators(h):
            s = s_ref[h][...] + mask_ref[...]
            if mode != "far":
                s = s + bias_ref[h, mode]
            m_old = m_ref[h]
            m_new = jnp.maximum(m_old, jnp.max(s, axis=0, keepdims=True))
            p_ref[h][...] = jnp.exp2(s - m_new).astype(BF16)
            alpha_ref[h] = jnp.exp2(m_old - m_new)
            m_ref[h] = m_new

        def weighted_values(h):
            pv = jnp.dot(vt_ref[kb, h // GROUP], p_ref[h][...], preferred_element_type=F32)
            acc_ref[h] = alpha_ref[h] * acc_ref[h] + pv

        for stage in (logits, numerators, weighted_values):
            for h in range(N_HEADS):
                stage(h)

    def far_loop(kb, carry):
        attn_block(kb, "far")
        return carry

    lax.fori_loop(0, jnp.maximum(i - 1, 0), far_loop, 0)

    @pl.when(i >= 1)
    def _():
        attn_block(i - 1, 1)

    attn_block(i, 0)

    for h in range(N_HEADS):
        acc = acc_ref[h]
        out_t = acc[0:HEAD_DIM, :] / acc[HEAD_DIM:HEAD_DIM + 1, :]
        o_ref[:, h * HEAD_DIM:(h + 1) * HEAD_DIM] = out_t.T.astype(o_ref.dtype)


def _attention(proj, kiw, bias_tiles, batch, seq, q_col, layer, casts):
    t = ATT_BLOCK
    nq = seq // t
    steps = batch * nq
    cast_in, cast_out, cast_shapes = [], [], []
    for w in casts:
        rows = w.shape[1] // steps
        assert rows * steps == w.shape[1] and rows % BF16_SUBLANES == 0
        cast_in.append(pl.BlockSpec((None, rows, w.shape[2]), lambda b, i: (layer, b * nq + i, 0)))
        cast_out.append(pl.BlockSpec((rows, w.shape[2]), lambda b, i: (b * nq + i, 0)))
        cast_shapes.append(jax.ShapeDtypeStruct(w.shape[1:], BF16))
    topk = min(TOPK_MAX, seq // 4)
    attn_dim = N_HEADS * HEAD_DIM
    kv_dim = N_KV_HEADS * HEAD_DIM
    qi_half = IDX_HEADS * IDX_DIM // 2
    k_col = q_col + attn_dim
    qi_col = k_col + 2 * kv_dim
    assert q_col % attn_dim == 0 and k_col % kv_dim == 0 and qi_col % qi_half == 0
    return pl.pallas_call(
        functools.partial(_attn_body, t=t, topk=topk, n_casts=len(casts)),
        grid=(batch, nq),
        in_specs=[
            pl.BlockSpec((t, attn_dim), lambda b, i: (b * nq + i, q_col // attn_dim)),
            pl.BlockSpec((seq, kv_dim), lambda b, i: (b, k_col // kv_dim)),
            pl.BlockSpec((seq, kv_dim), lambda b, i: (b, k_col // kv_dim + 1)),
            pl.BlockSpec((t, qi_half), lambda b, i: (b * nq + i, qi_col // qi_half)),
            pl.BlockSpec((t, qi_half), lambda b, i: (b * nq + i, qi_col // qi_half + 1)),
            pl.BlockSpec((seq, IDX_PAD), lambda b, i: (b, 0)),
            pl.BlockSpec((N_HEADS, 2, t, t), lambda b, i: (0, 0, 0, 0)),
        ] + cast_in,
        out_specs=[pl.BlockSpec((t, attn_dim), lambda b, i: (b * nq + i, 0))] + cast_out,
        out_shape=[jax.ShapeDtypeStruct((batch * seq, attn_dim), BF16)] + cast_shapes,
        scratch_shapes=[
            pltpu.VMEM((nq + 1, t, t), I32),
            pltpu.VMEM((nq + 1, t, t), I16),
            pltpu.VMEM((1, t), I32),
            pltpu.VMEM((nq, N_KV_HEADS, HEAD_DIM + ONES_ROWS, t), BF16),
            pltpu.VMEM((seq, IDX_DIM), BF16),
            pltpu.VMEM((IDX_HEADS, t, IDX_DIM), BF16),
            pltpu.VMEM((t, t), F32),
            pltpu.VMEM((N_HEADS, 1, t), F32),
            pltpu.VMEM((N_HEADS, 1, t), F32),
            pltpu.VMEM((N_HEADS, HEAD_DIM + ONES_ROWS, t), F32),
        ] + [pltpu.VMEM((t, t), F32)] * N_HEADS + [pltpu.VMEM((t, t), BF16)] * N_HEADS,
        compiler_params=_params("parallel", "arbitrary"),
        name="sparse_attention",
    )(proj, proj, proj, proj, proj, kiw, bias_tiles, *casts)


def _mixer_body(main_ref, halo_ref, ya_ref, g0_ref, g1_ref, g2_ref, wb_ref, pw_ref, ps_ref, cw_ref,
                o_ref, y_ref, acc_ref, *, tm, tn, seq, pool_dim, conv_dim):
    pos0 = (pl.program_id(0) * tm) % seq
    pos = lax.broadcasted_iota(I32, (tm, 1), 0) + pos0
    keep = jnp.where(pos0 == 0, 0.0, 1.0)
    n_out = o_ref.shape[-1]

    def branch(n, y, c0):
        return g_refs[n][:, c0:c0 + tn].astype(F32) * jnp.dot(y, wb_ref[n, :, c0:c0 + tn], preferred_element_type=F32)

    g_refs = (g0_ref, g1_ref, g2_ref)
    for c0 in range(0, n_out, tn):
        acc_ref[:, c0:c0 + tn] = branch(2, ya_ref[...], c0)

    def ext(c0, c1):
        halo = halo_ref[:, c0:c1].astype(F32) * keep
        return jnp.concatenate([halo, main_ref[:, c0:c1].astype(F32)], axis=0)

    for g, win in enumerate(POOL_WINDOWS):
        c0 = g * POOL_GROUP_DIM
        u = ext(c0, c0 + POOL_GROUP_DIM)
        s, sh = u, 1
        while sh < win:
            s = s + pltpu.roll(s, sh, axis=0)
            sh *= 2
        cnt = jnp.minimum(pos + 1, win).astype(F32)
        dlt = s[POOL_HALO:] / cnt - u[POOL_HALO:]
        yp = jnp.dot(dlt.astype(BF16), pw_ref[g], preferred_element_type=F32)
        y_ref[0, :, c0:c0 + POOL_GROUP_DIM] = (yp * ps_ref[:, c0:c0 + POOL_GROUP_DIM]).astype(BF16)

    z = ext(pool_dim, pool_dim + conv_dim) * ext(pool_dim + conv_dim, pool_dim + 2 * conv_dim)
    yc = z[POOL_HALO:] * cw_ref[CONV_WIDTH - 1:CONV_WIDTH, :]
    for tap in range(1, CONV_WIDTH):
        yc = yc + pltpu.roll(z, tap, axis=0)[POOL_HALO:] * cw_ref[CONV_WIDTH - 1 - tap:CONV_WIDTH - tap, :]
    gate_b = main_ref[:, pool_dim + 2 * conv_dim:pool_dim + 3 * conv_dim].astype(F32)
    y_ref[1] = (gate_b * yc).astype(BF16)

    for c0 in range(0, n_out, tn):
        mixed = acc_ref[:, c0:c0 + tn] + branch(0, y_ref[0], c0) + branch(1, y_ref[1], c0)
        o_ref[:, c0:c0 + tn] = mixed.astype(o_ref.dtype)


def _mixer(proj, gate_col, y_attn, w_branch, pool_w, pool_scale, conv_w, layer, seq, *, tm, tn):
    main = gates = proj
    tok = proj.shape[0]
    branch_dim = y_attn.shape[-1]
    width = 4 * branch_dim
    d = w_branch.shape[-1]
    assert gate_col % d == 0 and d % tn == 0
    gj = gate_col // d
    halo_blocks = tm // POOL_HALO
    resident = dict(pipeline_mode=pl.Buffered(1))
    return pl.pallas_call(
        functools.partial(_mixer_body, tm=tm, tn=tn, seq=seq, pool_dim=branch_dim, conv_dim=branch_dim),
        grid=(tok // tm,),
        in_specs=[
            pl.BlockSpec((tm, width), lambda i: (i, 0)),
            pl.BlockSpec((POOL_HALO, width), lambda i: (jnp.maximum(i * halo_blocks - 1, 0), 0)),
            pl.BlockSpec((tm, branch_dim), lambda i: (i, 0)),
            pl.BlockSpec((tm, d), lambda i: (i, gj)),
            pl.BlockSpec((tm, d), lambda i: (i, gj + 1)),
            pl.BlockSpec((tm, d), lambda i: (i, gj + 2)),
            pl.BlockSpec((None, 3, branch_dim, d), lambda i: (0, 0, 0, 0), **resident),
            pl.BlockSpec((None,) + pool_w.shape[1:], lambda i: (layer, 0, 0, 0), **resident),
            pl.BlockSpec((None, 1, branch_dim), lambda i: (layer, 0, 0)),
            pl.BlockSpec((None, CONV_WIDTH, branch_dim), lambda i: (layer, 0, 0)),
        ],
        out_specs=pl.BlockSpec((tm, d), lambda i: (i, 0)),
        out_shape=jax.ShapeDtypeStruct((tok, d), BF16),
        scratch_shapes=[pltpu.VMEM((2, tm, branch_dim), BF16), pltpu.VMEM((tm, d), F32)],
        compiler_params=_params("parallel"),
        name="mixer_branches",
    )(main, main, y_attn, gates, gates, gates, w_branch, pool_w, pool_scale, conv_w)


def _proj_norm_res_body(a_ref, w_ref, x_ref, g_ref, o_ref):
    half = a_ref.shape[0] // 2
    for rows in (slice(0, half), slice(half, 2 * half)):
        m = jnp.dot(a_ref[rows, :], w_ref[...], preferred_element_type=F32)
        o_ref[rows, :] = x_ref[rows, :] + _rms(m, g_ref[...])


def _proj_norm_res(a, w, layer, x, gain, *, tm):
    tok, d = x.shape
    k = a.shape[-1]
    return pl.pallas_call(
        _proj_norm_res_body,
        grid=(tok // tm,),
        in_specs=[
            pl.BlockSpec((tm, k), lambda i: (i, 0)),
            pl.BlockSpec((None, k, d), lambda i: (layer, 0, 0), pipeline_mode=pl.Buffered(1)),
            pl.BlockSpec((tm, d), lambda i: (i, 0)),
            pl.BlockSpec((1, d), lambda i: (0, 0)),
        ],
        out_specs=pl.BlockSpec((tm, d), lambda i: (i, 0)),
        out_shape=jax.ShapeDtypeStruct((tok, d), F32),
        compiler_params=_params("parallel"),
        name="out_proj",
    )(a, w, x, gain)


def _mlp_body(x_ref, gpre_ref, gpost_ref, wu_ref, wd_ref, o_ref, h_ref, acc_ref):
    j = pl.program_id(1)

    @pl.when(j == 0)
    def _():
        h_ref[...] = _rms(x_ref[...], gpre_ref[...]).astype(BF16)
        acc_ref[...] = jnp.zeros(acc_ref.shape, F32)

    hid = jnp.dot(h_ref[...], wu_ref[...], preferred_element_type=F32)
    hid = jnp.square(jnp.maximum(hid, 0.0)).astype(BF16)
    acc_ref[...] += jnp.dot(hid, wd_ref[...], preferred_element_type=F32)

    @pl.when(j == pl.num_programs(1) - 1)
    def _():
        o_ref[...] = x_ref[...] + _rms(acc_ref[...], gpost_ref[...])


def _mlp(x, gpre, gpost, w_up, w_down, layer, *, tm, th):
    tok, d = x.shape
    hidden = w_up.shape[-1]
    return pl.pallas_call(
        _mlp_body,
        grid=(tok // tm, hidden // th),
        in_specs=[
            pl.BlockSpec((tm, d), lambda i, j: (i, 0)),
            pl.BlockSpec((1, d), lambda i, j: (0, 0)),
            pl.BlockSpec((1, d), lambda i, j: (0, 0)),
            pl.BlockSpec((None, d, th), lambda i, j: (layer, 0, j)),
            pl.BlockSpec((None, th, d), lambda i, j: (layer, j, 0)),
        ],
        out_specs=pl.BlockSpec((tm, d), lambda i, j: (i, 0)),
        out_shape=jax.ShapeDtypeStruct((tok, d), F32),
        scratch_shapes=[pltpu.VMEM((tm, d), BF16), pltpu.VMEM((tm, d), F32)],
        compiler_params=_params("parallel", "arbitrary"),
        name="mlp",
    )(x, gpre, gpost, w_up, w_down)


def _tile(n, pref):
    t = min(n, pref)
    assert n % t == 0, (n, pref)
    return t


@jax.jit
def _forward(x, norm_gains, w_in, pool_w, pool_scale, conv_w, rel_bias, w_branch,
             w_gate, b_gate, w_out, w_up, w_down):
    batch, seq, d = x.shape
    depth = w_in.shape[0]
    tok = batch * seq
    branch_dim = d // 2
    attn_dim = N_HEADS * HEAD_DIM
    kv_dim = N_KV_HEADS * HEAD_DIM
    qi_dim = IDX_HEADS * IDX_DIM
    assert branch_dim == attn_dim == len(POOL_WINDOWS) * POOL_GROUP_DIM
    assert seq % ATT_BLOCK == 0 and ATT_BLOCK >= MAX_DISTANCE

    main_w = 4 * branch_dim
    n_in = main_w + attn_dim + 2 * kv_dim + qi_dim
    kw_w = IDX_DIM + IDX_HEADS
    assert w_in.shape[-1] == n_in + kw_w
    w_in_t = jnp.swapaxes(w_in, 1, 2)
    w_kw_t = jnp.pad(w_in_t[:, n_in:, :], ((0, 0), (0, IDX_PAD - kw_w), (0, 0))).astype(BF16)
    later_weights = (w_branch.reshape(depth, 3 * branch_dim, d), w_out, w_up, w_down)
    pool_w_b = pool_w.astype(BF16)

    gate_col = main_w
    q_col = main_w + 3 * d
    scale = jnp.concatenate([jnp.ones((q_col,), F32), jnp.full((attn_dim,), HEAD_DIM ** -0.5 * LOG2E, F32),
                             jnp.ones((n_in - main_w - attn_dim,), F32)])

    bias_tiles = _bias_tiles(rel_bias, ATT_BLOCK)

    tm = _tile(tok, 1024)
    tm_s = _tile(seq, 512)
    xf = x.reshape(tok, d)
    for l in range(depth):
        g = norm_gains[l]
        g0, g1, g2, g3 = (g[n:n + 1] for n in range(4))
        bias = jnp.concatenate([jnp.zeros((main_w,), F32), b_gate[l].reshape(3 * d), jnp.zeros((n_in - main_w,), F32)])
        proj, kiw = _projections(xf, g0, w_in_t, w_gate, w_kw_t, l, jnp.stack([scale, bias]), main_w, n_in,
                                 tm=_tile(tok, 2048), tn=512)
        y_attn, w_branch_b, w_out_b, w_up_b, w_down_b = _attention(proj, kiw, bias_tiles, batch, seq, q_col,
                                                                    l, later_weights)
        mixed = _mixer(proj, gate_col, y_attn, w_branch_b.reshape(1, 3, branch_dim, d), pool_w_b,
                       pool_scale.reshape(depth, 1, branch_dim), conv_w, l, seq, tm=tm_s, tn=1024)
        xf = _proj_norm_res(mixed, w_out_b[None], 0, xf, g1, tm=tm)
        xf = _mlp(xf, g2, g3, w_up_b[None], w_down_b[None], 0, tm=tm_s, th=1024)
    return xf.reshape(batch, seq, d)


def kernel(x, norm_gains, w_in, pool_w, pool_scale, conv_w, rel_bias, w_branch, w_gate, b_gate, w_out, w_up, w_down):
    return _forward(x, norm_gains, w_in, pool_w, pool_scale, conv_w, rel_bias, w_branch,
                    w_gate, b_gate, w_out, w_up, w_down)
```

```python
import functools
import math

import jax
import jax.numpy as jnp
import numpy as np
from jax import lax
from jax.experimental import pallas as pl
from jax.experimental.pallas import tpu as pltpu

F32 = jnp.float32
BF16 = jnp.bfloat16
I32 = jnp.int32
I16 = jnp.int16

NORM_EPS = 1e-6
POOL_WINDOWS = (2, 4, 8, 16)
POOL_GROUP_DIM = 256
POOL_HALO = 16
CONV_WIDTH = 3
N_HEADS = 8
N_KV_HEADS = 2
GROUP = N_HEADS // N_KV_HEADS
HEAD_DIM = 128
IDX_HEADS = 16
IDX_DIM = 64
TOPK_MAX = 256
NUM_BUCKETS = 32
MAX_EXACT = NUM_BUCKETS // 2
MAX_DISTANCE = 128
ATT_BLOCK = 256
IDX_PAD = 128
HALF_BITS = 16
INT_MIN = -(2 ** 31)
INT_MAX = 2 ** 31 - 1
NEG = -1e30
LOG2E = math.log2(math.e)
ONES_ROWS = 16
VMEM_LIMIT = 56 * 1024 * 1024
MXU_WIDTH = 256
BF16_SUBLANES = 16
NORM_ROWS = 256


def _bucket_thresholds():
    n = np.arange(MAX_EXACT, 8 * MAX_DISTANCE, dtype=np.int64)
    nf = n.astype(np.float32)
    large = MAX_EXACT + (np.log(nf / np.float32(MAX_EXACT)) / np.float32(math.log(MAX_DISTANCE / MAX_EXACT))
                         * np.float32(NUM_BUCKETS - MAX_EXACT)).astype(np.int32)
    large = np.minimum(large, NUM_BUCKETS - 1)
    thr = []
    for b in range(MAX_EXACT + 1, NUM_BUCKETS):
        thr.append(int(n[np.argmax(large >= b)]))
    return tuple(thr)


BUCKET_THRESHOLDS = _bucket_thresholds()


def _params(*sem):
    return pltpu.CompilerParams(dimension_semantics=sem, vmem_limit_bytes=VMEM_LIMIT)


def _rms(x, gain):
    ms = jnp.mean(x * x, axis=-1, keepdims=True)
    return x * lax.rsqrt(ms + NORM_EPS) * gain


def _dot_nt(a, b):
    return lax.dot_general(a, b, (((1,), (1,)), ((), ())), preferred_element_type=F32)


def _proj_body(x_hbm, g_ref, wint_ref, wg_ref, wkwt_ref, post_ref, o_ref, okw_ref, h_ref, x_buf, x_sem,
               *, gate_lo, gate_hi):
    i, j = pl.program_id(0), pl.program_id(1)
    tm = x_buf.shape[0]
    is_gate = jnp.logical_and(j >= gate_lo, j < gate_hi)

    def x_copy(tile):
        rows = pl.ds(pl.multiple_of(tile * tm, tm), tm)
        return pltpu.make_async_copy(x_hbm.at[rows, :], x_buf, x_sem)

    @pl.when(jnp.logical_and(i == 0, j == 0))
    def _():
        x_copy(0).start()

    @pl.when(j == 0)
    def _():
        x_copy(i).wait()

        def norm_rows(r, carry):
            rows = pl.ds(pl.multiple_of(r * NORM_ROWS, NORM_ROWS), NORM_ROWS)
            h = _rms(x_buf[rows, :], g_ref[...]).astype(BF16)
            h_ref[rows, :] = h
            okw_ref[rows, :] = _dot_nt(h, wkwt_ref[...])
            return carry

        lax.fori_loop(0, tm // NORM_ROWS, norm_rows, 0)

    @pl.when(jnp.logical_and(j == 1, i + 1 < pl.num_programs(0)))
    def _():
        x_copy(i + 1).start()

    chunks = [slice(c, c + MXU_WIDTH) for c in range(0, o_ref.shape[-1], MXU_WIDTH)]

    @pl.when(jnp.logical_not(is_gate))
    def _():
        for cs in chunks:
            y = _dot_nt(h_ref[...], wint_ref[cs, :].astype(BF16))
            o_ref[:, cs] = (y * post_ref[0:1, cs]).astype(o_ref.dtype)

    @pl.when(is_gate)
    def _():
        for cs in chunks:
            y = jnp.dot(h_ref[...], wg_ref[:, cs].astype(BF16), preferred_element_type=F32) + post_ref[1:2, cs]
            o_ref[:, cs] = (0.5 * jnp.tanh(0.5 * y) + 0.5).astype(o_ref.dtype)


def _projections(x, gain, w_in_t, w_gate, w_kw_t, layer, post, n_main, n_in, *, tm, tn):
    tok, d = x.shape
    n_gate = w_gate.shape[-1]
    assert n_main % tn == 0 and n_in % tn == 0 and n_gate % tn == 0 and tm % NORM_ROWS == 0
    gate_lo = n_main // tn
    gate_blocks = n_gate // tn
    gate_hi = gate_lo + gate_blocks
    kw = w_kw_t.shape[1]
    return pl.pallas_call(
        functools.partial(_proj_body, gate_lo=gate_lo, gate_hi=gate_hi),
        grid=(tok // tm, (n_in + n_gate) // tn),
        in_specs=[
            pl.BlockSpec(memory_space=pl.ANY),
            pl.BlockSpec((1, d), lambda i, j: (0, 0)),
            pl.BlockSpec((None, tn, d),
                         lambda i, j: (layer, jnp.where(j < gate_lo, j, jnp.maximum(j - gate_blocks, gate_lo)), 0)),
            pl.BlockSpec((None, d, tn), lambda i, j: (layer, 0, jnp.clip(j - gate_lo, 0, gate_blocks - 1))),
            pl.BlockSpec((None, kw, d), lambda i, j: (layer, 0, 0)),
            pl.BlockSpec((2, tn), lambda i, j: (0, j)),
        ],
        out_specs=[
            pl.BlockSpec((tm, tn), lambda i, j: (i, j)),
            pl.BlockSpec((tm, kw), lambda i, j: (i, 0)),
        ],
        out_shape=[
            jax.ShapeDtypeStruct((tok, n_in + n_gate), BF16),
            jax.ShapeDtypeStruct((tok, kw), F32),
        ],
        scratch_shapes=[pltpu.VMEM((tm, d), BF16), pltpu.VMEM((tm, d), F32), pltpu.SemaphoreType.DMA(())],
        compiler_params=_params("arbitrary", "arbitrary"),
        name="projections",
    )(x, gain, w_in_t, w_gate, w_kw_t, post)


def _bias_tiles_body(rel_ref, o_ref, *, t):
    row = lax.broadcasted_iota(I32, (t, t), 0)
    col = lax.broadcasted_iota(I32, (t, t), 1)
    for off in range(2):
        n = jnp.maximum(col - row + off * t, 0)
        large = jnp.full((t, t), MAX_EXACT, I32)
        for thr in BUCKET_THRESHOLDS:
            large = large + jnp.where(n >= thr, 1, 0)
        bucket = jnp.where(n < MAX_EXACT, n, large)
        for h in range(N_HEADS):
            val = jnp.zeros((t, t), F32)
            for b in range(NUM_BUCKETS):
                val = jnp.where(bucket == b, rel_ref[b, h], val)
            o_ref[h, off] = (val - rel_ref[NUM_BUCKETS - 1, h]) * LOG2E


def _bias_tiles(rel_bias, t):
    return pl.pallas_call(
        functools.partial(_bias_tiles_body, t=t),
        in_specs=[pl.BlockSpec(memory_space=pltpu.SMEM)],
        out_specs=pl.BlockSpec(memory_space=pltpu.VMEM),
        out_shape=jax.ShapeDtypeStruct((N_HEADS, 2, t, t), F32),
        compiler_params=pltpu.CompilerParams(vmem_limit_bytes=VMEM_LIMIT),
        name="bias_tiles",
    )(rel_bias)


def _attn_body(*refs, t, topk, n_casts):
    (q_ref, k_ref, v_ref, qia_ref, qib_ref, kiw_ref, bias_ref), refs = refs[:7], refs[7:]
    cast_src, refs = refs[:n_casts], refs[n_casts:]
    o_ref, cast_dst, refs = refs[0], refs[1:1 + n_casts], refs[1 + n_casts:]
    (sc_ref, sch_ref, thr_ref, vt_ref, kib_ref, qih_ref, mask_ref, alpha_ref, m_ref, acc_ref), refs = (
        refs[:10], refs[10:])
    s_ref, p_ref = refs[:N_HEADS], refs[N_HEADS:]

    for src, dst in zip(cast_src, cast_dst):
        dst[...] = src[...].astype(BF16)

    i = pl.program_id(1)
    nkb = vt_ref.shape[0]
    q0 = pl.multiple_of(i * t, t)
    key_pos = lax.broadcasted_iota(I32, (t, 1), 0)
    qry_pos = lax.broadcasted_iota(I32, (1, t), 1)
    causal = key_pos <= qry_pos

    @pl.when(i == 0)
    def _():
        for kb in range(nkb):
            rows = slice(kb * t, (kb + 1) * t)
            vb = v_ref[rows, :].astype(F32)
            for g in range(N_KV_HEADS):
                vt_ref[kb, g, 0:HEAD_DIM, :] = vb[:, g * HEAD_DIM:(g + 1) * HEAD_DIM].T.astype(BF16)
                vt_ref[kb, g, HEAD_DIM:, :] = jnp.ones((vt_ref.shape[2] - HEAD_DIM, t), BF16)
            kib_ref[rows, :] = kiw_ref[rows, 0:IDX_DIM].astype(BF16)

    half = IDX_HEADS // 2
    for h in range(IDX_HEADS):
        src = qia_ref if h < half else qib_ref
        qih_ref[h] = src[:, (h % half) * IDX_DIM:(h % half + 1) * IDX_DIM]
    w_t = kiw_ref[pl.ds(q0, t), :].T[IDX_DIM:IDX_DIM + IDX_HEADS, :] * (IDX_HEADS ** -0.5)

    def score_block(kb, diag):
        kblk = kib_ref[pl.ds(pl.multiple_of(kb * t, t), t), :]
        acc = jnp.zeros((t, t), F32)
        for h in range(IDX_HEADS):
            acc = acc + w_t[h:h + 1, :] * jnp.maximum(_dot_nt(kblk, qih_ref[h]), 0.0)
        if diag:
            acc = jnp.where(causal, acc, -jnp.inf)
        bits = pltpu.bitcast(acc, I32)
        bits = jnp.where(bits == INT_MIN, 0, bits)
        key = bits ^ ((bits >> 31) & INT_MAX)
        sc_ref[kb] = key
        sch_ref[kb] = (key >> HALF_BITS).astype(I16)

    def score_loop(kb, carry):
        score_block(kb, False)
        return carry

    lax.fori_loop(0, i, score_loop, 0)
    score_block(i, True)

    @pl.when(i % 2 == 0)
    def _():
        sc_ref[i + 1] = jnp.full((t, t), INT_MIN, I32)
        sch_ref[i + 1] = jnp.full((t, t), INT_MIN >> HALF_BITS, I16)

    @pl.when(q0 + t > topk)
    def _():
        needs = q0 + qry_pos >= topk
        rows = t // BF16_SUBLANES
        half_lo, half_hi = INT_MIN >> HALF_BITS, -(INT_MIN >> HALF_BITS)

        def halving(_, state):
            lo, hi, cnt_lo, cnt_hi = state
            mid = (lo & hi) + ((lo ^ hi) >> 1)
            mid_c = mid.astype(I16)

            def count(pair, c):
                for kb in (2 * pair, 2 * pair + 1):
                    ge = jnp.where(sch_ref[kb] >= mid_c, jnp.int16(1), jnp.int16(0))
                    for r in range(rows):
                        c = c + ge[r * BF16_SUBLANES:(r + 1) * BF16_SUBLANES]
                return c

            c = lax.fori_loop(0, (i + 2) // 2, count, jnp.zeros((BF16_SUBLANES, t), I16))
            c = jnp.sum(c.astype(I32), axis=0, keepdims=True)
            fresh = mid > lo
            up = jnp.logical_and(fresh, c >= topk)
            down = jnp.logical_and(fresh, c < topk)
            return (jnp.where(up, mid, lo), jnp.where(down, mid, hi),
                    jnp.where(up, c, cnt_lo), jnp.where(down, c, cnt_hi))

        band, _, cnt_lo, cnt_hi = lax.fori_loop(
            0, HALF_BITS, halving,
            (jnp.full((1, t), half_lo, I32), jnp.full((1, t), half_hi, I32),
             jnp.full((1, t), topk + 1, I32), jnp.zeros((1, t), I32)))

        def low_halves(kb, carry):
            key = sc_ref[kb]
            high = key >> HALF_BITS
            low = (key & (2 * half_hi - 1)) + half_lo
            sch_ref[kb] = jnp.where(high == band, low, jnp.where(high > band, half_hi - 1, half_lo)).astype(I16)
            return carry

        lax.fori_loop(0, i + 1, low_halves, 0)
        low, _, cnt_lo, cnt_hi = lax.fori_loop(
            0, HALF_BITS, halving,
            (jnp.full((1, t), half_lo, I32), jnp.full((1, t), half_hi, I32), cnt_lo, cnt_hi))
        lo = (band << HALF_BITS) + (low - half_lo)
        thr_ref[...] = jnp.where(needs, lo, INT_MIN)

        tied = jnp.where(jnp.logical_and(needs, cnt_lo > topk), 1, 0)

        @pl.when(jnp.max(tied) > 0)
        def _():
            wanted = (topk - cnt_hi).astype(F32)
            tied_f = tied.astype(F32)
            earlier = jnp.where(lax.broadcasted_iota(I32, (t, t), 1) < lax.broadcasted_iota(I32, (t, t), 0),
                                1.0, 0.0).astype(BF16)

            def demote(kb, seen):
                key = sc_ref[kb]
                is_tie = jnp.where(key == lo, tied_f, 0.0)
                rank = seen + jnp.dot(earlier, is_tie.astype(BF16), preferred_element_type=F32)
                drop = is_tie * jnp.where(rank >= wanted, 1.0, 0.0)
                sc_ref[kb] = jnp.where(drop > 0.0, lo - 1, key)
                return seen + jnp.sum(is_tie, axis=0, keepdims=True)

            lax.fori_loop(0, i + 1, demote, jnp.zeros((1, t), F32))

    @pl.when(q0 + t <= topk)
    def _():
        thr_ref[...] = jnp.full((1, t), INT_MIN, I32)

    m_ref[...] = jnp.full(m_ref.shape, NEG, F32)
    acc_ref[...] = jnp.zeros(acc_ref.shape, F32)
    thr = thr_ref[...]

    def attn_block(kb, mode):
        ks = pl.ds(pl.multiple_of(kb * t, t), t)
        mask = jnp.where(sc_ref[kb] >= thr, 0.0, NEG)
        if mode == 0:
            mask = jnp.where(causal, mask, NEG)
        mask_ref[...] = mask

        def logits(h):
            g = h // GROUP
            s_ref[h][...] = _dot_nt(k_ref[ks, g * HEAD_DIM:(g + 1) * HEAD_DIM],
                                    q_ref[:, h * HEAD_DIM:(h + 1) * HEAD_DIM])

        def numerators(h):
            s = s_ref[h][...] + mask_ref[...]
            if mode != "far":
                s = s + bias_ref[h, mode]
            m_old = m_ref[h]
            m_new = jnp.maximum(m_old, jnp.max(s, axis=0, keepdims=True))
            p_ref[h][...] = jnp.exp2(s - m_new).astype(BF16)
            alpha_ref[h] = jnp.exp2(m_old - m_new)
            m_ref[h] = m_new

        def weighted_values(h):
            pv = jnp.dot(vt_ref[kb, h // GROUP], p_ref[h][...], preferred_element_type=F32)
            acc_ref[h] = alpha_ref[h] * acc_ref[h] + pv

        for stage in (logits, numerators, weighted_values):
            for h in range(N_HEADS):
                stage(h)

    def far_loop(kb, carry):
        attn_block(kb, "far")
        return carry

    lax.fori_loop(0, jnp.maximum(i - 1, 0), far_loop, 0)

    @pl.when(i >= 1)
    def _():
        attn_block(i - 1, 1)

    attn_block(i, 0)

    for h in range(N_HEADS):
        acc = acc_ref[h]
        out_t = acc[0:HEAD_DIM, :] / acc[HEAD_DIM:HEAD_DIM + 1, :]
        o_ref[:, h * HEAD_DIM:(h + 1) * HEAD_DIM] = out_t.T.astype(o_ref.dtype)


def _attention(proj, kiw, bias_tiles, batch, seq, q_col, layer, casts):
    t = ATT_BLOCK
    nq = seq // t
    steps = batch * nq
    cast_in, cast_out, cast_shapes = [], [], []
    for w in casts:
        rows = w.shape[1] // steps
        assert rows * steps == w.shape[1] and rows % BF16_SUBLANES == 0
        cast_in.append(pl.BlockSpec((None, rows, w.shape[2]), lambda b, i: (layer, b * nq + i, 0)))
        cast_out.append(pl.BlockSpec((rows, w.shape[2]), lambda b, i: (b * nq + i, 0)))
        cast_shapes.append(jax.ShapeDtypeStruct(w.shape[1:], BF16))
    topk = min(TOPK_MAX, seq // 4)
    attn_dim = N_HEADS * HEAD_DIM
    kv_dim = N_KV_HEADS * HEAD_DIM
    qi_half = IDX_HEADS * IDX_DIM // 2
    k_col = q_col + attn_dim
    qi_col = k_col + 2 * kv_dim
    assert q_col % attn_dim == 0 and k_col % kv_dim == 0 and qi_col % qi_half == 0
    return pl.pallas_call(
        functools.partial(_attn_body, t=t, topk=topk, n_casts=len(casts)),
        grid=(batch, nq),
        in_specs=[
            pl.BlockSpec((t, attn_dim), lambda b, i: (b * nq + i, q_col // attn_dim)),
            pl.BlockSpec((seq, kv_dim), lambda b, i: (b, k_col // kv_dim)),
            pl.BlockSpec((seq, kv_dim), lambda b, i: (b, k_col // kv_dim + 1)),
            pl.BlockSpec((t, qi_half), lambda b, i: (b * nq + i, qi_col // qi_half)),
            pl.BlockSpec((t, qi_half), lambda b, i: (b * nq + i, qi_col // qi_half + 1)),
            pl.BlockSpec((seq, IDX_PAD), lambda b, i: (b, 0)),
            pl.BlockSpec((N_HEADS, 2, t, t), lambda b, i: (0, 0, 0, 0)),
        ] + cast_in,
        out_specs=[pl.BlockSpec((t, attn_dim), lambda b, i: (b * nq + i, 0))] + cast_out,
        out_shape=[jax.ShapeDtypeStruct((batch * seq, attn_dim), BF16)] + cast_shapes,
        scratch_shapes=[
            pltpu.VMEM((nq + 1, t, t), I32),
            pltpu.VMEM((nq + 1, t, t), I16),
            pltpu.VMEM((1, t), I32),
            pltpu.VMEM((nq, N_KV_HEADS, HEAD_DIM + ONES_ROWS, t), BF16),
            pltpu.VMEM((seq, IDX_DIM), BF16),
            pltpu.VMEM((IDX_HEADS, t, IDX_DIM), BF16),
            pltpu.VMEM((t, t), F32),
            pltpu.VMEM((N_HEADS, 1, t), F32),
            pltpu.VMEM((N_HEADS, 1, t), F32),
            pltpu.VMEM((N_HEADS, HEAD_DIM + ONES_ROWS, t), F32),
        ] + [pltpu.VMEM((t, t), F32)] * N_HEADS + [pltpu.VMEM((t, t), BF16)] * N_HEADS,
        compiler_params=_params("parallel", "arbitrary"),
        name="sparse_attention",
    )(proj, proj, proj, proj, proj, kiw, bias_tiles, *casts)


def _mixer_body(main_ref, halo_ref, ya_ref, g0_ref, g1_ref, g2_ref, wb_ref, pw_ref, ps_ref, cw_ref,
                o_ref, y_ref, acc_ref, *, tm, tn, seq, pool_dim, conv_dim):
    pos0 = (pl.program_id(0) * tm) % seq
    pos = lax.broadcasted_iota(I32, (tm, 1), 0) + pos0
    keep = jnp.where(pos0 == 0, 0.0, 1.0)
    n_out = o_ref.shape[-1]

    def branch(n, y, c0):
        return g_refs[n][:, c0:c0 + tn].astype(F32) * jnp.dot(y, wb_ref[n, :, c0:c0 + tn], preferred_element_type=F32)

    g_refs = (g0_ref, g1_ref, g2_ref)
    for c0 in range(0, n_out, tn):
        acc_ref[:, c0:c0 + tn] = branch(2, ya_ref[...], c0)

    def ext(c0, c1):
        halo = halo_ref[:, c0:c1].astype(F32) * keep
        return jnp.concatenate([halo, main_ref[:, c0:c1].astype(F32)], axis=0)

    for g, win in enumerate(POOL_WINDOWS):
        c0 = g * POOL_GROUP_DIM
        u = ext(c0, c0 + POOL_GROUP_DIM)
        s, sh = u, 1
        while sh < win:
            s = s + pltpu.roll(s, sh, axis=0)
            sh *= 2
        cnt = jnp.minimum(pos + 1, win).astype(F32)
        dlt = s[POOL_HALO:] / cnt - u[POOL_HALO:]
        yp = jnp.dot(dlt.astype(BF16), pw_ref[g], preferred_element_type=F32)
        y_ref[0, :, c0:c0 + POOL_GROUP_DIM] = (yp * ps_ref[:, c0:c0 + POOL_GROUP_DIM]).astype(BF16)

    z = ext(pool_dim, pool_dim + conv_dim) * ext(pool_dim + conv_dim, pool_dim + 2 * conv_dim)
    yc = z[POOL_HALO:] * cw_ref[CONV_WIDTH - 1:CONV_WIDTH, :]
    for tap in range(1, CONV_WIDTH):
        yc = yc + pltpu.roll(z, tap, axis=0)[POOL_HALO:] * cw_ref[CONV_WIDTH - 1 - tap:CONV_WIDTH - tap, :]
    gate_b = main_ref[:, pool_dim + 2 * conv_dim:pool_dim + 3 * conv_dim].astype(F32)
    y_ref[1] = (gate_b * yc).astype(BF16)

    for c0 in range(0, n_out, tn):
        mixed = acc_ref[:, c0:c0 + tn] + branch(0, y_ref[0], c0) + branch(1, y_ref[1], c0)
        o_ref[:, c0:c0 + tn] = mixed.astype(o_ref.dtype)


def _mixer(proj, gate_col, y_attn, w_branch, pool_w, pool_scale, conv_w, layer, seq, *, tm, tn):
    main = gates = proj
    tok = proj.shape[0]
    branch_dim = y_attn.shape[-1]
    width = 4 * branch_dim
    d = w_branch.shape[-1]
    assert gate_col % d == 0 and d % tn == 0
    gj = gate_col // d
    halo_blocks = tm // POOL_HALO
    resident = dict(pipeline_mode=pl.Buffered(1))
    return pl.pallas_call(
        functools.partial(_mixer_body, tm=tm, tn=tn, seq=seq, pool_dim=branch_dim, conv_dim=branch_dim),
        grid=(tok // tm,),
        in_specs=[
            pl.BlockSpec((tm, width), lambda i: (i, 0)),
            pl.BlockSpec((POOL_HALO, width), lambda i: (jnp.maximum(i * halo_blocks - 1, 0), 0)),
            pl.BlockSpec((tm, branch_dim), lambda i: (i, 0)),
            pl.BlockSpec((tm, d), lambda i: (i, gj)),
            pl.BlockSpec((tm, d), lambda i: (i, gj + 1)),
            pl.BlockSpec((tm, d), lambda i: (i, gj + 2)),
            pl.BlockSpec((None, 3, branch_dim, d), lambda i: (0, 0, 0, 0), **resident),
            pl.BlockSpec((None,) + pool_w.shape[1:], lambda i: (layer, 0, 0, 0), **resident),
            pl.BlockSpec((None, 1, branch_dim), lambda i: (layer, 0, 0)),
            pl.BlockSpec((None, CONV_WIDTH, branch_dim), lambda i: (layer, 0, 0)),
        ],
        out_specs=pl.BlockSpec((tm, d), lambda i: (i, 0)),
        out_shape=jax.ShapeDtypeStruct((tok, d), BF16),
        scratch_shapes=[pltpu.VMEM((2, tm, branch_dim), BF16), pltpu.VMEM((tm, d), F32)],
        compiler_params=_params("parallel"),
        name="mixer_branches",
    )(main, main, y_attn, gates, gates, gates, w_branch, pool_w, pool_scale, conv_w)


def _proj_norm_res_body(a_ref, w_ref, x_ref, g_ref, o_ref):
    half = a_ref.shape[0] // 2
    for rows in (slice(0, half), slice(half, 2 * half)):
        m = jnp.dot(a_ref[rows, :], w_ref[...], preferred_element_type=F32)
        o_ref[rows, :] = x_ref[rows, :] + _rms(m, g_ref[...])


def _proj_norm_res(a, w, layer, x, gain, *, tm):
    tok, d = x.shape
    k = a.shape[-1]
    return pl.pallas_call(
        _proj_norm_res_body,
        grid=(tok // tm,),
        in_specs=[
            pl.BlockSpec((tm, k), lambda i: (i, 0)),
            pl.BlockSpec((None, k, d), lambda i: (layer, 0, 0), pipeline_mode=pl.Buffered(1)),
            pl.BlockSpec((tm, d), lambda i: (i, 0)),
            pl.BlockSpec((1, d), lambda i: (0, 0)),
        ],
        out_specs=pl.BlockSpec((tm, d), lambda i: (i, 0)),
        out_shape=jax.ShapeDtypeStruct((tok, d), F32),
        compiler_params=_params("parallel"),
        name="out_proj",
    )(a, w, x, gain)


def _mlp_body(x_ref, gpre_ref, gpost_ref, wu_ref, wd_ref, o_ref, h_ref, acc_ref):
    j = pl.program_id(1)

    @pl.when(j == 0)
    def _():
        h_ref[...] = _rms(x_ref[...], gpre_ref[...]).astype(BF16)
        acc_ref[...] = jnp.zeros(acc_ref.shape, F32)

    hid = jnp.dot(h_ref[...], wu_ref[...], preferred_element_type=F32)
    hid = jnp.square(jnp.maximum(hid, 0.0)).astype(BF16)
    acc_ref[...] += jnp.dot(hid, wd_ref[...], preferred_element_type=F32)

    @pl.when(j == pl.num_programs(1) - 1)
    def _():
        o_ref[...] = x_ref[...] + _rms(acc_ref[...], gpost_ref[...])


def _mlp(x, gpre, gpost, w_up, w_down, layer, *, tm, th):
    tok, d = x.shape
    hidden = w_up.shape[-1]
    return pl.pallas_call(
        _mlp_body,
        grid=(tok // tm, hidden // th),
        in_specs=[
            pl.BlockSpec((tm, d), lambda i, j: (i, 0)),
            pl.BlockSpec((1, d), lambda i, j: (0, 0)),
            pl.BlockSpec((1, d), lambda i, j: (0, 0)),
            pl.BlockSpec((None, d, th), lambda i, j: (layer, 0, j)),
            pl.BlockSpec((None, th, d), lambda i, j: (layer, j, 0)),
        ],
        out_specs=pl.BlockSpec((tm, d), lambda i, j: (i, 0)),
        out_shape=jax.ShapeDtypeStruct((tok, d), F32),
        scratch_shapes=[pltpu.VMEM((tm, d), BF16), pltpu.VMEM((tm, d), F32)],
        compiler_params=_params("parallel", "arbitrary"),
        name="mlp",
    )(x, gpre, gpost, w_up, w_down)


def _tile(n, pref):
    t = min(n, pref)
    assert n % t == 0, (n, pref)
    return t


@jax.jit
def _forward(x, norm_gains, w_in, pool_w, pool_scale, conv_w, rel_bias, w_branch,
             w_gate, b_gate, w_out, w_up, w_down):
    batch, seq, d = x.shape
    depth = w_in.shape[0]
    tok = batch * seq
    branch_dim = d // 2
    attn_dim = N_HEADS * HEAD_DIM
    kv_dim = N_KV_HEADS * HEAD_DIM
    qi_dim = IDX_HEADS * IDX_DIM
    assert branch_dim == attn_dim == len(POOL_WINDOWS) * POOL_GROUP_DIM
    assert seq % ATT_BLOCK == 0 and ATT_BLOCK >= MAX_DISTANCE

    main_w = 4 * branch_dim
    n_in = main_w + attn_dim + 2 * kv_dim + qi_dim
    kw_w = IDX_DIM + IDX_HEADS
    assert w_in.shape[-1] == n_in + kw_w
    w_in_t = jnp.swapaxes(w_in, 1, 2)
    w_kw_t = jnp.pad(w_in_t[:, n_in:, :], ((0, 0), (0, IDX_PAD - kw_w), (0, 0))).astype(BF16)
    later_weights = (w_branch.reshape(depth, 3 * branch_dim, d), w_out, w_up, w_down)
    pool_w_b = pool_w.astype(BF16)

    gate_col = main_w
    q_col = main_w + 3 * d
    scale = jnp.concatenate([jnp.ones((q_col,), F32), jnp.full((attn_dim,), HEAD_DIM ** -0.5 * LOG2E, F32),
                             jnp.ones((n_in - main_w - attn_dim,), F32)])

    bias_tiles = _bias_tiles(rel_bias, ATT_BLOCK)

    tm = _tile(tok, 1024)
    tm_s = _tile(seq, 512)
    xf = x.reshape(tok, d)
    for l in range(depth):
        g = norm_gains[l]
        g0, g1, g2, g3 = (g[n:n + 1] for n in range(4))
        bias = jnp.concatenate([jnp.zeros((main_w,), F32), b_gate[l].reshape(3 * d), jnp.zeros((n_in - main_w,), F32)])
        proj, kiw = _projections(xf, g0, w_in_t, w_gate, w_kw_t, l, jnp.stack([scale, bias]), main_w, n_in,
                                 tm=_tile(tok, 2048), tn=512)
        y_attn, w_branch_b, w_out_b, w_up_b, w_down_b = _attention(proj, kiw, bias_tiles, batch, seq, q_col,
                                                                    l, later_weights)
        mixed = _mixer(proj, gate_col, y_attn, w_branch_b.reshape(1, 3, branch_dim, d), pool_w_b,
                       pool_scale.reshape(depth, 1, branch_dim), conv_w, l, seq, tm=tm_s, tn=1024)
        xf = _proj_norm_res(mixed, w_out_b[None], 0, xf, g1, tm=tm)
        xf = _mlp(xf, g2, g3, w_up_b[None], w_down_b[None], 0, tm=tm_s, th=1024)
    return xf.reshape(batch, seq, d)


def kernel(x, norm_gains, w_in, pool_w, pool_scale, conv_w, rel_bias, w_branch, w_gate, b_gate, w_out, w_up, w_down):
    return _forward(x, norm_gains, w_in, pool_w, pool_scale, conv_w, rel_bias, w_branch,
                    w_gate, b_gate, w_out, w_up, w_down)
```

```python
import functools
import math

import jax
import jax.numpy as jnp
import numpy as np
from jax import lax
from jax.experimental import pallas as pl
from jax.experimental.pallas import tpu as pltpu

F32 = jnp.float32
BF16 = jnp.bfloat16
I32 = jnp.int32
I16 = jnp.int16

NORM_EPS = 1e-6
POOL_WINDOWS = (2, 4, 8, 16)
POOL_GROUP_DIM = 256
POOL_HALO = 16
CONV_WIDTH = 3
N_HEADS = 8
N_KV_HEADS = 2
GROUP = N_HEADS // N_KV_HEADS
HEAD_DIM = 128
IDX_HEADS = 16
IDX_DIM = 64
TOPK_MAX = 256
NUM_BUCKETS = 32
MAX_EXACT = NUM_BUCKETS // 2
MAX_DISTANCE = 128
ATT_BLOCK = 256
IDX_PAD = 128
HALF_BITS = 16
INT_MIN = -(2 ** 31)
INT_MAX = 2 ** 31 - 1
NEG = -1e30
LOG2E = math.log2(math.e)
ONES_ROWS = 16
VMEM_LIMIT = 56 * 1024 * 1024
MXU_WIDTH = 256
BF16_SUBLANES = 16
NORM_ROWS = 256


def _bucket_thresholds():
    n = np.arange(MAX_EXACT, 8 * MAX_DISTANCE, dtype=np.int64)
    nf = n.astype(np.float32)
    large = MAX_EXACT + (np.log(nf / np.float32(MAX_EXACT)) / np.float32(math.log(MAX_DISTANCE / MAX_EXACT))
                         * np.float32(NUM_BUCKETS - MAX_EXACT)).astype(np.int32)
    large = np.minimum(large, NUM_BUCKETS - 1)
    thr = []
    for b in range(MAX_EXACT + 1, NUM_BUCKETS):
        thr.append(int(n[np.argmax(large >= b)]))
    return tuple(thr)


BUCKET_THRESHOLDS = _bucket_thresholds()


def _params(*sem):
    return pltpu.CompilerParams(dimension_semantics=sem, vmem_limit_bytes=VMEM_LIMIT)


def _rms(x, gain):
    ms = jnp.mean(x * x, axis=-1, keepdims=True)
    return x * lax.rsqrt(ms + NORM_EPS) * gain


def _dot_nt(a, b):
    return lax.dot_general(a, b, (((1,), (1,)), ((), ())), preferred_element_type=F32)


def _proj_body(x_hbm, g_ref, wint_ref, wg_ref, wkwt_ref, post_ref, o_ref, okw_ref, h_ref, x_buf, x_sem,
               *, gate_lo, gate_hi):
    i, j = pl.program_id(0), pl.program_id(1)
    tm = x_buf.shape[0]
    is_gate = jnp.logical_and(j >= gate_lo, j < gate_hi)

    def x_copy(tile):
        rows = pl.ds(pl.multiple_of(tile * tm, tm), tm)
        return pltpu.make_async_copy(x_hbm.at[rows, :], x_buf, x_sem)

    @pl.when(jnp.logical_and(i == 0, j == 0))
    def _():
        x_copy(0).start()

    @pl.when(j == 0)
    def _():
        x_copy(i).wait()

        def norm_rows(r, carry):
            rows = pl.ds(pl.multiple_of(r * NORM_ROWS, NORM_ROWS), NORM_ROWS)
            h = _rms(x_buf[rows, :], g_ref[...]).astype(BF16)
            h_ref[rows, :] = h
            okw_ref[rows, :] = _dot_nt(h, wkwt_ref[...])
            return carry

        lax.fori_loop(0, tm // NORM_ROWS, norm_rows, 0)

    @pl.when(jnp.logical_and(j == 1, i + 1 < pl.num_programs(0)))
    def _():
        x_copy(i + 1).start()

    chunks = [slice(c, c + MXU_WIDTH) for c in range(0, o_ref.shape[-1], MXU_WIDTH)]

    @pl.when(jnp.logical_not(is_gate))
    def _():
        for cs in chunks:
            y = _dot_nt(h_ref[...], wint_ref[cs, :].astype(BF16))
            o_ref[:, cs] = (y * post_ref[0:1, cs]).astype(o_ref.dtype)

    @pl.when(is_gate)
    def _():
        for cs in chunks:
            y = jnp.dot(h_ref[...], wg_ref[:, cs].astype(BF16), preferred_element_type=F32) + post_ref[1:2, cs]
            o_ref[:, cs] = (0.5 * jnp.tanh(0.5 * y) + 0.5).astype(o_ref.dtype)


def _projections(x, gain, w_in_t, w_gate, w_kw_t, layer, post, n_main, n_in, *, tm, tn):
    tok, d = x.shape
    n_gate = w_gate.shape[-1]
    assert n_main % tn == 0 and n_in % tn == 0 and n_gate % tn == 0 and tm % NORM_ROWS == 0
    gate_lo = n_main // tn
    gate_blocks = n_gate // tn
    gate_hi = gate_lo + gate_blocks
    kw = w_kw_t.shape[1]
    return pl.pallas_call(
        functools.partial(_proj_body, gate_lo=gate_lo, gate_hi=gate_hi),
        grid=(tok // tm, (n_in + n_gate) // tn),
        in_specs=[
            pl.BlockSpec(memory_space=pl.ANY),
            pl.BlockSpec((1, d), lambda i, j: (0, 0)),
            pl.BlockSpec((None, tn, d),
                         lambda i, j: (layer, jnp.where(j < gate_lo, j, jnp.maximum(j - gate_blocks, gate_lo)), 0)),
            pl.BlockSpec((None, d, tn), lambda i, j: (layer, 0, jnp.clip(j - gate_lo, 0, gate_blocks - 1))),
            pl.BlockSpec((None, kw, d), lambda i, j: (layer, 0, 0)),
            pl.BlockSpec((2, tn), lambda i, j: (0, j)),
        ],
        out_specs=[
            pl.BlockSpec((tm, tn), lambda i, j: (i, j)),
            pl.BlockSpec((tm, kw), lambda i, j: (i, 0)),
        ],
        out_shape=[
            jax.ShapeDtypeStruct((tok, n_in + n_gate), BF16),
            jax.ShapeDtypeStruct((tok, kw), F32),
        ],
        scratch_shapes=[pltpu.VMEM((tm, d), BF16), pltpu.VMEM((tm, d), F32), pltpu.SemaphoreType.DMA(())],
        compiler_params=_params("arbitrary", "arbitrary"),
        name="projections",
    )(x, gain, w_in_t, w_gate, w_kw_t, post)


def _bias_tiles_body(rel_ref, o_ref, *, t):
    row = lax.broadcasted_iota(I32, (t, t), 0)
    col = lax.broadcasted_iota(I32, (t, t), 1)
    for off in range(2):
        n = jnp.maximum(col - row + off * t, 0)
        large = jnp.full((t, t), MAX_EXACT, I32)
        for thr in BUCKET_THRESHOLDS:
            large = large + jnp.where(n >= thr, 1, 0)
        bucket = jnp.where(n < MAX_EXACT, n, large)
        for h in range(N_HEADS):
            val = jnp.zeros((t, t), F32)
            for b in range(NUM_BUCKETS):
                val = jnp.where(bucket == b, rel_ref[b, h], val)
            o_ref[h, off] = (val - rel_ref[NUM_BUCKETS - 1, h]) * LOG2E


def _bias_tiles(rel_bias, t):
    return pl.pallas_call(
        functools.partial(_bias_tiles_body, t=t),
        in_specs=[pl.BlockSpec(memory_space=pltpu.SMEM)],
        out_specs=pl.BlockSpec(memory_space=pltpu.VMEM),
        out_shape=jax.ShapeDtypeStruct((N_HEADS, 2, t, t), F32),
        compiler_params=pltpu.CompilerParams(vmem_limit_bytes=VMEM_LIMIT),
        name="bias_tiles",
    )(rel_bias)


def _attn_body(*refs, t, topk, n_casts):
    (q_ref, k_ref, v_ref, qia_ref, qib_ref, kiw_ref, bias_ref), refs = refs[:7], refs[7:]
    cast_src, refs = refs[:n_casts], refs[n_casts:]
    o_ref, cast_dst, refs = refs[0], refs[1:1 + n_casts], refs[1 + n_casts:]
    (sc_ref, sch_ref, thr_ref, vt_ref, kib_ref, qih_ref, mask_ref, alpha_ref, m_ref, acc_ref), refs = (
        refs[:10], refs[10:])
    s_ref, p_ref = refs[:N_HEADS], refs[N_HEADS:]

    for src, dst in zip(cast_src, cast_dst):
        dst[...] = src[...].astype(BF16)

    i = pl.program_id(1)
    nkb = vt_ref.shape[0]
    q0 = pl.multiple_of(i * t, t)
    key_pos = lax.broadcasted_iota(I32, (t, 1), 0)
    qry_pos = lax.broadcasted_iota(I32, (1, t), 1)
    causal = key_pos <= qry_pos

    @pl.when(i == 0)
    def _():
        for kb in range(nkb):
            rows = slice(kb * t, (kb + 1) * t)
            vb = v_ref[rows, :].astype(F32)
            for g in range(N_KV_HEADS):
                vt_ref[kb, g, 0:HEAD_DIM, :] = vb[:, g * HEAD_DIM:(g + 1) * HEAD_DIM].T.astype(BF16)
                vt_ref[kb, g, HEAD_DIM:, :] = jnp.ones((vt_ref.shape[2] - HEAD_DIM, t), BF16)
            kib_ref[rows, :] = kiw_ref[rows, 0:IDX_DIM].astype(BF16)

    half = IDX_HEADS // 2
    for h in range(IDX_HEADS):
        src = qia_ref if h < half else qib_ref
        qih_ref[h] = src[:, (h % half) * IDX_DIM:(h % half + 1) * IDX_DIM]
    w_t = kiw_ref[pl.ds(q0, t), :].T[IDX_DIM:IDX_DIM + IDX_HEADS, :] * (IDX_HEADS ** -0.5)

    def score_block(kb, diag):
        kblk = kib_ref[pl.ds(pl.multiple_of(kb * t, t), t), :]
        acc = jnp.zeros((t, t), F32)
        for h in range(IDX_HEADS):
            acc = acc + w_t[h:h + 1, :] * jnp.maximum(_dot_nt(kblk, qih_ref[h]), 0.0)
        if diag:
            acc = jnp.where(causal, acc, -jnp.inf)
        bits = pltpu.bitcast(acc, I32)
        bits = jnp.where(bits == INT_MIN, 0, bits)
        key = bits ^ ((bits >> 31) & INT_MAX)
        sc_ref[kb] = key
        sch_ref[kb] = (key >> HALF_BITS).astype(I16)

    def score_loop(kb, carry):
        score_block(kb, False)
        return carry

    lax.fori_loop(0, i, score_loop, 0)
    score_block(i, True)

    @pl.when(i % 2 == 0)
    def _():
        sc_ref[i + 1] = jnp.full((t, t), INT_MIN, I32)
        sch_ref[i + 1] = jnp.full((t, t), INT_MIN >> HALF_BITS, I16)

    @pl.when(q0 + t > topk)
    def _():
        needs = q0 + qry_pos >= topk
        rows = t // BF16_SUBLANES
        half_lo, half_hi = INT_MIN >> HALF_BITS, -(INT_MIN >> HALF_BITS)

        def halving(_, state):
            lo, hi, cnt_lo, cnt_hi = state
            mid = (lo & hi) + ((lo ^ hi) >> 1)
            mid_c = mid.astype(I16)

            def count(pair, c):
                for kb in (2 * pair, 2 * pair + 1):
                    ge = jnp.where(sch_ref[kb] >= mid_c, jnp.int16(1), jnp.int16(0))
                    for r in range(rows):
                        c = c + ge[r * BF16_SUBLANES:(r + 1) * BF16_SUBLANES]
                return c

            c = lax.fori_loop(0, (i + 2) // 2, count, jnp.zeros((BF16_SUBLANES, t), I16))
            c = jnp.sum(c.astype(I32), axis=0, keepdims=True)
            fresh = mid > lo
            up = jnp.logical_and(fresh, c >= topk)
            down = jnp.logical_and(fresh, c < topk)
            return (jnp.where(up, mid, lo), jnp.where(down, mid, hi),
                    jnp.where(up, c, cnt_lo), jnp.where(down, c, cnt_hi))

        band, _, cnt_lo, cnt_hi = lax.fori_loop(
            0, HALF_BITS, halving,
            (jnp.full((1, t), half_lo, I32), jnp.full((1, t), half_hi, I32),
             jnp.full((1, t), topk + 1, I32), jnp.zeros((1, t), I32)))

        def low_halves(kb, carry):
            key = sc_ref[kb]
            high = key >> HALF_BITS
            low = (key & (2 * half_hi - 1)) + half_lo
            sch_ref[kb] = jnp.where(high == band, low, jnp.where(high > band, half_hi - 1, half_lo)).astype(I16)
            return carry

        lax.fori_loop(0, i + 1, low_halves, 0)
        low, _, cnt_lo, cnt_hi = lax.fori_loop(
            0, HALF_BITS, halving,
            (jnp.full((1, t), half_lo, I32), jnp.full((1, t), half_hi, I32), cnt_lo, cnt_hi))
        lo = (band << HALF_BITS) + (low - half_lo)
        thr_ref[...] = jnp.where(needs, lo, INT_MIN)

        tied = jnp.where(jnp.logical_and(needs, cnt_lo > topk), 1, 0)

        @pl.when(jnp.max(tied) > 0)
        def _():
            wanted = (topk - cnt_hi).astype(F32)
            tied_f = tied.astype(F32)
            earlier = jnp.where(lax.broadcasted_iota(I32, (t, t), 1) < lax.broadcasted_iota(I32, (t, t), 0),
                                1.0, 0.0).astype(BF16)

            def demote(kb, seen):
                key = sc_ref[kb]
                is_tie = jnp.where(key == lo, tied_f, 0.0)
                rank = seen + jnp.dot(earlier, is_tie.astype(BF16), preferred_element_type=F32)
                drop = is_tie * jnp.where(rank >= wanted, 1.0, 0.0)
                sc_ref[kb] = jnp.where(drop > 0.0, lo - 1, key)
                return seen + jnp.sum(is_tie, axis=0, keepdims=True)

            lax.fori_loop(0, i + 1, demote, jnp.zeros((1, t), F32))

    @pl.when(q0 + t <= topk)
    def _():
        thr_ref[...] = jnp.full((1, t), INT_MIN, I32)

    m_ref[...] = jnp.full(m_ref.shape, NEG, F32)
    acc_ref[...] = jnp.zeros(acc_ref.shape, F32)
    thr = thr_ref[...]

    def attn_block(kb, mode):
        ks = pl.ds(pl.multiple_of(kb * t, t), t)
        mask = jnp.where(sc_ref[kb] >= thr, 0.0, NEG)
        if mode == 0:
            mask = jnp.where(causal, mask, NEG)
        mask_ref[...] = mask

        def logits(h):
            g = h // GROUP
            s_ref[h][...] = _dot_nt(k_ref[ks, g * HEAD_DIM:(g + 1) * HEAD_DIM],
                                    q_ref[:, h * HEAD_DIM:(h + 1) * HEAD_DIM])

        def numerators(h):
            s = s_ref[h][...] + mask_ref[...]
            if mode != "far":
                s = s + bias_ref[h, mode]
            m_old = m_ref[h]
            m_new = jnp.maximum(m_old, jnp.max(s, axis=0, keepdims=True))
            p_ref[h][...] = jnp.exp2(s - m_new).astype(BF16)
            alpha_ref[h] = jnp.exp2(m_old - m_new)
            m_ref[h] = m_new

        def weighted_values(h):
            pv = jnp.dot(vt_ref[kb, h // GROUP], p_ref[h][...], preferred_element_type=F32)
            acc_ref[h] = alpha_ref[h] * acc_ref[h] + pv

        for stage in (logits, numerators, weighted_values):
            for h in range(N_HEADS):
                stage(h)

    def far_loop(kb, carry):
        attn_block(kb, "far")
        return carry

    lax.fori_loop(0, jnp.maximum(i - 1, 0), far_loop, 0)

    @pl.when(i >= 1)
    def _():
        attn_block(i - 1, 1)

    attn_block(i, 0)

    for h in range(N_HEADS):
        acc = acc_ref[h]
        out_t = acc[0:HEAD_DIM, :] / acc[HEAD_DIM:HEAD_DIM + 1, :]
        o_ref[:, h * HEAD_DIM:(h + 1) * HEAD_DIM] = out_t.T.astype(o_ref.dtype)


def _attention(proj, kiw, bias_tiles, batch, seq, q_col, layer, casts):
    t = ATT_BLOCK
    nq = seq // t
    steps = batch * nq
    cast_in, cast_out, cast_shapes = [], [], []
    for w in casts:
        rows = w.shape[1] // steps
        assert rows * steps == w.shape[1] and rows % BF16_SUBLANES == 0
        cast_in.append(pl.BlockSpec((None, rows, w.shape[2]), lambda b, i: (layer, b * nq + i, 0)))
        cast_out.append(pl.BlockSpec((rows, w.shape[2]), lambda b, i: (b * nq + i, 0)))
        cast_shapes.append(jax.ShapeDtypeStruct(w.shape[1:], BF16))
    topk = min(TOPK_MAX, seq // 4)
    attn_dim = N_HEADS * HEAD_DIM
    kv_dim = N_KV_HEADS * HEAD_DIM
    qi_half = IDX_HEADS * IDX_DIM // 2
    k_col = q_col + attn_dim
    qi_col = k_col + 2 * kv_dim
    assert q_col % attn_dim == 0 and k_col % kv_dim == 0 and qi_col % qi_half == 0
    return pl.pallas_call(
        functools.partial(_attn_body, t=t, topk=topk, n_casts=len(casts)),
        grid=(batch, nq),
        in_specs=[
            pl.BlockSpec((t, attn_dim), lambda b, i: (b * nq + i, q_col // attn_dim)),
            pl.BlockSpec((seq, kv_dim), lambda b, i: (b, k_col // kv_dim)),
            pl.BlockSpec((seq, kv_dim), lambda b, i: (b, k_col // kv_dim + 1)),
            pl.BlockSpec((t, qi_half), lambda b, i: (b * nq + i, qi_col // qi_half)),
            pl.BlockSpec((t, qi_half), lambda b, i: (b * nq + i, qi_col // qi_half + 1)),
            pl.BlockSpec((seq, IDX_PAD), lambda b, i: (b, 0)),
            pl.BlockSpec((N_HEADS, 2, t, t), lambda b, i: (0, 0, 0, 0)),
        ] + cast_in,
        out_specs=[pl.BlockSpec((t, attn_dim), lambda b, i: (b * nq + i, 0))] + cast_out,
        out_shape=[jax.ShapeDtypeStruct((batch * seq, attn_dim), BF16)] + cast_shapes,
        scratch_shapes=[
            pltpu.VMEM((nq + 1, t, t), I32),
            pltpu.VMEM((nq + 1, t, t), I16),
            pltpu.VMEM((1, t), I32),
            pltpu.VMEM((nq, N_KV_HEADS, HEAD_DIM + ONES_ROWS, t), BF16),
            pltpu.VMEM((seq, IDX_DIM), BF16),
            pltpu.VMEM((IDX_HEADS, t, IDX_DIM), BF16),
            pltpu.VMEM((t, t), F32),
            pltpu.VMEM((N_HEADS, 1, t), F32),
            pltpu.VMEM((N_HEADS, 1, t), F32),
            pltpu.VMEM((N_HEADS, HEAD_DIM + ONES_ROWS, t), F32),
        ] + [pltpu.VMEM((t, t), F32)] * N_HEADS + [pltpu.VMEM((t, t), BF16)] * N_HEADS,
        compiler_params=_params("parallel", "arbitrary"),
        name="sparse_attention",
    )(proj, proj, proj, proj, proj, kiw, bias_tiles, *casts)


def _mixer_body(main_ref, halo_ref, ya_ref, g0_ref, g1_ref, g2_ref, wb_ref, pw_ref, ps_ref, cw_ref,
                o_ref, y_ref, acc_ref, *, tm, tn, seq, pool_dim, conv_dim):
    pos0 = (pl.program_id(0) * tm) % seq
    pos = lax.broadcasted_iota(I32, (tm, 1), 0) + pos0
    keep = jnp.where(pos0 == 0, 0.0, 1.0)
    n_out = o_ref.shape[-1]

    def branch(n, y, c0):
        return g_refs[n][:, c0:c0 + tn].astype(F32) * jnp.dot(y, wb_ref[n, :, c0:c0 + tn], preferred_element_type=F32)

    g_refs = (g0_ref, g1_ref, g2_ref)
    for c0 in range(0, n_out, tn):
        acc_ref[:, c0:c0 + tn] = branch(2, ya_ref[...], c0)

    def ext(c0, c1):
        halo = halo_ref[:, c0:c1].astype(F32) * keep
        return jnp.concatenate([halo, main_ref[:, c0:c1].astype(F32)], axis=0)

    for g, win in enumerate(POOL_WINDOWS):
        c0 = g * POOL_GROUP_DIM
        u = ext(c0, c0 + POOL_GROUP_DIM)
        s, sh = u, 1
        while sh < win:
            s = s + pltpu.roll(s, sh, axis=0)
            sh *= 2
        cnt = jnp.minimum(pos + 1, win).astype(F32)
        dlt = s[POOL_HALO:] / cnt - u[POOL_HALO:]
        yp = jnp.dot(dlt.astype(BF16), pw_ref[g], preferred_element_type=F32)
        y_ref[0, :, c0:c0 + POOL_GROUP_DIM] = (yp * ps_ref[:, c0:c0 + POOL_GROUP_DIM]).astype(BF16)

    z = ext(pool_dim, pool_dim + conv_dim) * ext(pool_dim + conv_dim, pool_dim + 2 * conv_dim)
    yc = z[POOL_HALO:] * cw_ref[CONV_WIDTH - 1:CONV_WIDTH, :]
    for tap in range(1, CONV_WIDTH):
        yc = yc + pltpu.roll(z, tap, axis=0)[POOL_HALO:] * cw_ref[CONV_WIDTH - 1 - tap:CONV_WIDTH - tap, :]
    gate_b = main_ref[:, pool_dim + 2 * conv_dim:pool_dim + 3 * conv_dim].astype(F32)
    y_ref[1] = (gate_b * yc).astype(BF16)

    for c0 in range(0, n_out, tn):
        mixed = acc_ref[:, c0:c0 + tn] + branch(0, y_ref[0], c0) + branch(1, y_ref[1], c0)
        o_ref[:, c0:c0 + tn] = mixed.astype(o_ref.dtype)


def _mixer(proj, gate_col, y_attn, w_branch, pool_w, pool_scale, conv_w, layer, seq, *, tm, tn):
    main = gates = proj
    tok = proj.shape[0]
    branch_dim = y_attn.shape[-1]
    width = 4 * branch_dim
    d = w_branch.shape[-1]
    assert gate_col % d == 0 and d % tn == 0
    gj = gate_col // d
    halo_blocks = tm // POOL_HALO
    resident = dict(pipeline_mode=pl.Buffered(1))
    return pl.pallas_call(
        functools.partial(_mixer_body, tm=tm, tn=tn, seq=seq, pool_dim=branch_dim, conv_dim=branch_dim),
        grid=(tok // tm,),
        in_specs=[
            pl.BlockSpec((tm, width), lambda i: (i, 0)),
            pl.BlockSpec((POOL_HALO, width), lambda i: (jnp.maximum(i * halo_blocks - 1, 0), 0)),
            pl.BlockSpec((tm, branch_dim), lambda i: (i, 0)),
            pl.BlockSpec((tm, d), lambda i: (i, gj)),
            pl.BlockSpec((tm, d), lambda i: (i, gj + 1)),
            pl.BlockSpec((tm, d), lambda i: (i, gj + 2)),
            pl.BlockSpec((None, 3, branch_dim, d), lambda i: (0, 0, 0, 0), **resident),
            pl.BlockSpec((None,) + pool_w.shape[1:], lambda i: (layer, 0, 0, 0), **resident),
            pl.BlockSpec((None, 1, branch_dim), lambda i: (layer, 0, 0)),
            pl.BlockSpec((None, CONV_WIDTH, branch_dim), lambda i: (layer, 0, 0)),
        ],
        out_specs=pl.BlockSpec((tm, d), lambda i: (i, 0)),
        out_shape=jax.ShapeDtypeStruct((tok, d), BF16),
        scratch_shapes=[pltpu.VMEM((2, tm, branch_dim), BF16), pltpu.VMEM((tm, d), F32)],
        compiler_params=_params("parallel"),
        name="mixer_branches",
    )(main, main, y_attn, gates, gates, gates, w_branch, pool_w, pool_scale, conv_w)


def _proj_norm_res_body(a_ref, w_ref, x_ref, g_ref, o_ref):
    half = a_ref.shape[0] // 2
    for rows in (slice(0, half), slice(half, 2 * half)):
        m = jnp.dot(a_ref[rows, :], w_ref[...], preferred_element_type=F32)
        o_ref[rows, :] = x_ref[rows, :] + _rms(m, g_ref[...])


def _proj_norm_res(a, w, layer, x, gain, *, tm):
    tok, d = x.shape
    k = a.shape[-1]
    return pl.pallas_call(
        _proj_norm_res_body,
        grid=(tok // tm,),
        in_specs=[
            pl.BlockSpec((tm, k), lambda i: (i, 0)),
            pl.BlockSpec((None, k, d), lambda i: (layer, 0, 0), pipeline_mode=pl.Buffered(1)),
            pl.BlockSpec((tm, d), lambda i: (i, 0)),
            pl.BlockSpec((1, d), lambda i: (0, 0)),
        ],
        out_specs=pl.BlockSpec((tm, d), lambda i: (i, 0)),
        out_shape=jax.ShapeDtypeStruct((tok, d), F32),
        compiler_params=_params("parallel"),
        name="out_proj",
    )(a, w, x, gain)


def _mlp_body(x_ref, gpre_ref, gpost_ref, wu_ref, wd_ref, o_ref, h_ref, acc_ref, xkeep_ref):
    i, j = pl.program_id(0), pl.program_id(1)

    def finish_tile():
        o_ref[...] = xkeep_ref[...] + _rms(acc_ref[...], gpost_ref[...])

    def start_tile():
        xkeep_ref[...] = x_ref[...]
        h_ref[...] = _rms(x_ref[...], gpre_ref[...]).astype(BF16)

    def hidden_tile():
        hid = jnp.dot(h_ref[...], wu_ref[...], preferred_element_type=F32)
        hid = jnp.square(jnp.maximum(hid, 0.0)).astype(BF16)
        return jnp.dot(hid, wd_ref[...], preferred_element_type=F32)

    @pl.when(jnp.logical_and(j == 0, i == 0))
    def _():
        start_tile()
        acc_ref[...] = hidden_tile()

    @pl.when(jnp.logical_and(j == 0, i > 0))
    def _():
        finish_tile()
        start_tile()
        acc_ref[...] = hidden_tile()

    @pl.when(j > 0)
    def _():
        acc_ref[...] += hidden_tile()

    @pl.when(jnp.logical_and(j == pl.num_programs(1) - 1, i == pl.num_programs(0) - 1))
    def _():
        finish_tile()


def _mlp(x, gpre, gpost, w_up, w_down, layer, *, tm, th):
    tok, d = x.shape
    hidden = w_up.shape[-1]
    return pl.pallas_call(
        _mlp_body,
        grid=(tok // tm, hidden // th),
        in_specs=[
            pl.BlockSpec((tm, d), lambda i, j: (i, 0)),
            pl.BlockSpec((1, d), lambda i, j: (0, 0)),
            pl.BlockSpec((1, d), lambda i, j: (0, 0)),
            pl.BlockSpec((None, d, th), lambda i, j: (layer, 0, j)),
            pl.BlockSpec((None, th, d), lambda i, j: (layer, j, 0)),
        ],
        out_specs=pl.BlockSpec((tm, d), lambda i, j: (jnp.where(j == 0, jnp.maximum(i - 1, 0), i), 0)),
        out_shape=jax.ShapeDtypeStruct((tok, d), F32),
        scratch_shapes=[pltpu.VMEM((tm, d), BF16), pltpu.VMEM((tm, d), F32), pltpu.VMEM((tm, d), F32)],
        compiler_params=_params("arbitrary", "arbitrary"),
        name="mlp",
    )(x, gpre, gpost, w_up, w_down)


def _tile(n, pref):
    t = min(n, pref)
    assert n % t == 0, (n, pref)
    return t


@jax.jit
def _forward(x, norm_gains, w_in, pool_w, pool_scale, conv_w, rel_bias, w_branch,
             w_gate, b_gate, w_out, w_up, w_down):
    batch, seq, d = x.shape
    depth = w_in.shape[0]
    tok = batch * seq
    branch_dim = d // 2
    attn_dim = N_HEADS * HEAD_DIM
    kv_dim = N_KV_HEADS * HEAD_DIM
    qi_dim = IDX_HEADS * IDX_DIM
    assert branch_dim == attn_dim == len(POOL_WINDOWS) * POOL_GROUP_DIM
    assert seq % ATT_BLOCK == 0 and ATT_BLOCK >= MAX_DISTANCE

    main_w = 4 * branch_dim
    n_in = main_w + attn_dim + 2 * kv_dim + qi_dim
    kw_w = IDX_DIM + IDX_HEADS
    assert w_in.shape[-1] == n_in + kw_w
    w_in_t = jnp.swapaxes(w_in, 1, 2)
    w_kw_t = jnp.pad(w_in_t[:, n_in:, :], ((0, 0), (0, IDX_PAD - kw_w), (0, 0))).astype(BF16)
    later_weights = (w_branch.reshape(depth, 3 * branch_dim, d), w_out, w_up, w_down)
    pool_w_b = pool_w.astype(BF16)

    gate_col = main_w
    q_col = main_w + 3 * d
    scale = jnp.concatenate([jnp.ones((q_col,), F32), jnp.full((attn_dim,), HEAD_DIM ** -0.5 * LOG2E, F32),
                             jnp.ones((n_in - main_w - attn_dim,), F32)])

    bias_tiles = _bias_tiles(rel_bias, ATT_BLOCK)

    tm = _tile(tok, 1024)
    tm_s = _tile(seq, 512)
    xf = x.reshape(tok, d)
    for l in range(depth):
        g = norm_gains[l]
        g0, g1, g2, g3 = (g[n:n + 1] for n in range(4))
        bias = jnp.concatenate([jnp.zeros((main_w,), F32), b_gate[l].reshape(3 * d), jnp.zeros((n_in - main_w,), F32)])
        proj, kiw = _projections(xf, g0, w_in_t, w_gate, w_kw_t, l, jnp.stack([scale, bias]), main_w, n_in,
                                 tm=_tile(tok, 2048), tn=512)
        y_attn, w_branch_b, w_out_b, w_up_b, w_down_b = _attention(proj, kiw, bias_tiles, batch, seq, q_col,
                                                                    l, later_weights)
        mixed = _mixer(proj, gate_col, y_attn, w_branch_b.reshape(1, 3, branch_dim, d), pool_w_b,
                       pool_scale.reshape(depth, 1, branch_dim), conv_w, l, seq, tm=tm_s, tn=1024)
        xf = _proj_norm_res(mixed, w_out_b[None], 0, xf, g1, tm=tm)
        xf = _mlp(xf, g2, g3, w_up_b[None], w_down_b[None], 0, tm=tm_s, th=1024)
    return xf.reshape(batch, seq, d)


def kernel(x, norm_gains, w_in, pool_w, pool_scale, conv_w, rel_bias, w_branch, w_gate, b_gate, w_out, w_up, w_down):
    return _forward(x, norm_gains, w_in, pool_w, pool_scale, conv_w, rel_bias, w_branch,
                    w_gate, b_gate, w_out, w_up, w_down)
```

```python
import functools
import math

import jax
import jax.numpy as jnp
import numpy as np
from jax import lax
from jax.experimental import pallas as pl
from jax.experimental.pallas import tpu as pltpu

F32 = jnp.float32
BF16 = jnp.bfloat16
I32 = jnp.int32
I16 = jnp.int16

NORM_EPS = 1e-6
POOL_WINDOWS = (2, 4, 8, 16)
POOL_GROUP_DIM = 256
POOL_HALO = 16
CONV_WIDTH = 3
N_HEADS = 8
N_KV_HEADS = 2
GROUP = N_HEADS // N_KV_HEADS
HEAD_DIM = 128
IDX_HEADS = 16
IDX_DIM = 64
TOPK_MAX = 256
NUM_BUCKETS = 32
MAX_EXACT = NUM_BUCKETS // 2
MAX_DISTANCE = 128
ATT_BLOCK = 256
IDX_PAD = 128
HALF_BITS = 16
INT_MIN = -(2 ** 31)
INT_MAX = 2 ** 31 - 1
NEG = -1e30
LOG2E = math.log2(math.e)
ONES_ROWS = 16
VMEM_LIMIT = 56 * 1024 * 1024
MXU_WIDTH = 256
MLP_HIDDEN_TILE = 512
BF16_SUBLANES = 16
NORM_ROWS = 256


def _bucket_thresholds():
    n = np.arange(MAX_EXACT, 8 * MAX_DISTANCE, dtype=np.int64)
    nf = n.astype(np.float32)
    large = MAX_EXACT + (np.log(nf / np.float32(MAX_EXACT)) / np.float32(math.log(MAX_DISTANCE / MAX_EXACT))
                         * np.float32(NUM_BUCKETS - MAX_EXACT)).astype(np.int32)
    large = np.minimum(large, NUM_BUCKETS - 1)
    thr = []
    for b in range(MAX_EXACT + 1, NUM_BUCKETS):
        thr.append(int(n[np.argmax(large >= b)]))
    return tuple(thr)


BUCKET_THRESHOLDS = _bucket_thresholds()


def _params(*sem):
    return pltpu.CompilerParams(dimension_semantics=sem, vmem_limit_bytes=VMEM_LIMIT)


def _rms(x, gain):
    ms = jnp.mean(x * x, axis=-1, keepdims=True)
    return x * lax.rsqrt(ms + NORM_EPS) * gain


def _dot_nt(a, b):
    return lax.dot_general(a, b, (((1,), (1,)), ((), ())), preferred_element_type=F32)


def _proj_body(x_hbm, g_ref, wint_ref, wg_ref, wkwt_ref, post_ref, o_ref, okw_ref, h_ref, x_buf, x_sem,
               *, gate_lo, gate_hi):
    i, j = pl.program_id(0), pl.program_id(1)
    tm = x_buf.shape[0]
    is_gate = jnp.logical_and(j >= gate_lo, j < gate_hi)

    def x_copy(tile):
        rows = pl.ds(pl.multiple_of(tile * tm, tm), tm)
        return pltpu.make_async_copy(x_hbm.at[rows, :], x_buf, x_sem)

    @pl.when(jnp.logical_and(i == 0, j == 0))
    def _():
        x_copy(0).start()

    @pl.when(j == 0)
    def _():
        x_copy(i).wait()

        def norm_rows(r, carry):
            rows = pl.ds(pl.multiple_of(r * NORM_ROWS, NORM_ROWS), NORM_ROWS)
            h = _rms(x_buf[rows, :], g_ref[...]).astype(BF16)
            h_ref[rows, :] = h
            okw_ref[rows, :] = _dot_nt(h, wkwt_ref[...])
            return carry

        lax.fori_loop(0, tm // NORM_ROWS, norm_rows, 0)

    @pl.when(jnp.logical_and(j == 1, i + 1 < pl.num_programs(0)))
    def _():
        x_copy(i + 1).start()

    chunks = [slice(c, c + MXU_WIDTH) for c in range(0, o_ref.shape[-1], MXU_WIDTH)]

    @pl.when(jnp.logical_not(is_gate))
    def _():
        for cs in chunks:
            y = _dot_nt(h_ref[...], wint_ref[cs, :].astype(BF16))
            o_ref[:, cs] = (y * post_ref[0:1, cs]).astype(o_ref.dtype)

    @pl.when(is_gate)
    def _():
        for cs in chunks:
            y = jnp.dot(h_ref[...], wg_ref[:, cs].astype(BF16), preferred_element_type=F32) + post_ref[1:2, cs]
            o_ref[:, cs] = (0.5 * jnp.tanh(0.5 * y) + 0.5).astype(o_ref.dtype)


def _projections(x, gain, w_in_t, w_gate, w_kw_t, layer, post, n_main, n_in, *, tm, tn):
    tok, d = x.shape
    n_gate = w_gate.shape[-1]
    assert n_main % tn == 0 and n_in % tn == 0 and n_gate % tn == 0 and tm % NORM_ROWS == 0
    gate_lo = n_main // tn
    gate_blocks = n_gate // tn
    gate_hi = gate_lo + gate_blocks
    kw = w_kw_t.shape[1]
    return pl.pallas_call(
        functools.partial(_proj_body, gate_lo=gate_lo, gate_hi=gate_hi),
        grid=(tok // tm, (n_in + n_gate) // tn),
        in_specs=[
            pl.BlockSpec(memory_space=pl.ANY),
            pl.BlockSpec((1, d), lambda i, j: (0, 0)),
            pl.BlockSpec((None, tn, d),
                         lambda i, j: (layer, jnp.where(j < gate_lo, j, jnp.maximum(j - gate_blocks, gate_lo)), 0)),
            pl.BlockSpec((None, d, tn), lambda i, j: (layer, 0, jnp.clip(j - gate_lo, 0, gate_blocks - 1))),
            pl.BlockSpec((None, kw, d), lambda i, j: (layer, 0, 0)),
            pl.BlockSpec((2, tn), lambda i, j: (0, j)),
        ],
        out_specs=[
            pl.BlockSpec((tm, tn), lambda i, j: (i, j)),
            pl.BlockSpec((tm, kw), lambda i, j: (i, 0)),
        ],
        out_shape=[
            jax.ShapeDtypeStruct((tok, n_in + n_gate), BF16),
            jax.ShapeDtypeStruct((tok, kw), F32),
        ],
        scratch_shapes=[pltpu.VMEM((tm, d), BF16), pltpu.VMEM((tm, d), F32), pltpu.SemaphoreType.DMA(())],
        compiler_params=_params("arbitrary", "arbitrary"),
        name="projections",
    )(x, gain, w_in_t, w_gate, w_kw_t, post)


def _bias_tiles_body(rel_ref, o_ref, *, t):
    row = lax.broadcasted_iota(I32, (t, t), 0)
    col = lax.broadcasted_iota(I32, (t, t), 1)
    for off in range(2):
        n = jnp.maximum(col - row + off * t, 0)
        large = jnp.full((t, t), MAX_EXACT, I32)
        for thr in BUCKET_THRESHOLDS:
            large = large + jnp.where(n >= thr, 1, 0)
        bucket = jnp.where(n < MAX_EXACT, n, large)
        for h in range(N_HEADS):
            val = jnp.zeros((t, t), F32)
            for b in range(NUM_BUCKETS):
                val = jnp.where(bucket == b, rel_ref[b, h], val)
            o_ref[h, off] = (val - rel_ref[NUM_BUCKETS - 1, h]) * LOG2E


def _bias_tiles(rel_bias, t):
    return pl.pallas_call(
        functools.partial(_bias_tiles_body, t=t),
        in_specs=[pl.BlockSpec(memory_space=pltpu.SMEM)],
        out_specs=pl.BlockSpec(memory_space=pltpu.VMEM),
        out_shape=jax.ShapeDtypeStruct((N_HEADS, 2, t, t), F32),
        compiler_params=pltpu.CompilerParams(vmem_limit_bytes=VMEM_LIMIT),
        name="bias_tiles",
    )(rel_bias)


def _attn_body(*refs, t, topk, n_casts):
    (q_ref, k_ref, v_ref, qia_ref, qib_ref, kiw_ref, bias_ref), refs = refs[:7], refs[7:]
    cast_src, refs = refs[:n_casts], refs[n_casts:]
    o_ref, cast_dst, refs = refs[0], refs[1:1 + n_casts], refs[1 + n_casts:]
    (sc_ref, sch_ref, thr_ref, vt_ref, kib_ref, qih_ref, mask_ref, alpha_ref, m_ref, acc_ref), refs = (
        refs[:10], refs[10:])
    s_ref, p_ref = refs[:N_HEADS], refs[N_HEADS:]

    for src, dst in zip(cast_src, cast_dst):
        if len(dst.shape) == 2:
            dst[...] = src[...].astype(BF16)
        else:
            width = dst.shape[-1]
            for c in range(dst.shape[0]):
                dst[c] = src[:, c * width:(c + 1) * width].astype(BF16)

    i = pl.program_id(1)
    nkb = vt_ref.shape[0]
    q0 = pl.multiple_of(i * t, t)
    key_pos = lax.broadcasted_iota(I32, (t, 1), 0)
    qry_pos = lax.broadcasted_iota(I32, (1, t), 1)
    causal = key_pos <= qry_pos

    @pl.when(i == 0)
    def _():
        for kb in range(nkb):
            rows = slice(kb * t, (kb + 1) * t)
            vb = v_ref[rows, :].astype(F32)
            for g in range(N_KV_HEADS):
                vt_ref[kb, g, 0:HEAD_DIM, :] = vb[:, g * HEAD_DIM:(g + 1) * HEAD_DIM].T.astype(BF16)
                vt_ref[kb, g, HEAD_DIM:, :] = jnp.ones((vt_ref.shape[2] - HEAD_DIM, t), BF16)
            kib_ref[rows, :] = kiw_ref[rows, 0:IDX_DIM].astype(BF16)

    half = IDX_HEADS // 2
    for h in range(IDX_HEADS):
        src = qia_ref if h < half else qib_ref
        qih_ref[h] = src[:, (h % half) * IDX_DIM:(h % half + 1) * IDX_DIM]
    w_t = kiw_ref[pl.ds(q0, t), :].T[IDX_DIM:IDX_DIM + IDX_HEADS, :] * (IDX_HEADS ** -0.5)

    def score_block(kb, diag):
        kblk = kib_ref[pl.ds(pl.multiple_of(kb * t, t), t), :]
        acc = jnp.zeros((t, t), F32)
        for h in range(IDX_HEADS):
            acc = acc + w_t[h:h + 1, :] * jnp.maximum(_dot_nt(kblk, qih_ref[h]), 0.0)
        if diag:
            acc = jnp.where(causal, acc, -jnp.inf)
        bits = pltpu.bitcast(acc, I32)
        bits = jnp.where(bits == INT_MIN, 0, bits)
        key = bits ^ ((bits >> 31) & INT_MAX)
        sc_ref[kb] = key
        sch_ref[kb] = (key >> HALF_BITS).astype(I16)

    def score_loop(kb, carry):
        score_block(kb, False)
        return carry

    lax.fori_loop(0, i, score_loop, 0)
    score_block(i, True)

    @pl.when(i % 2 == 0)
    def _():
        sc_ref[i + 1] = jnp.full((t, t), INT_MIN, I32)
        sch_ref[i + 1] = jnp.full((t, t), INT_MIN >> HALF_BITS, I16)

    @pl.when(q0 + t > topk)
    def _():
        needs = q0 + qry_pos >= topk
        rows = t // BF16_SUBLANES
        half_lo, half_hi = INT_MIN >> HALF_BITS, -(INT_MIN >> HALF_BITS)

        def halving(_, state):
            lo, hi, cnt_lo, cnt_hi = state
            mid = (lo & hi) + ((lo ^ hi) >> 1)
            mid_c = mid.astype(I16)

            def count(pair, c):
                for kb in (2 * pair, 2 * pair + 1):
                    ge = jnp.where(sch_ref[kb] >= mid_c, jnp.int16(1), jnp.int16(0))
                    for r in range(rows):
                        c = c + ge[r * BF16_SUBLANES:(r + 1) * BF16_SUBLANES]
                return c

            c = lax.fori_loop(0, (i + 2) // 2, count, jnp.zeros((BF16_SUBLANES, t), I16))
            c = jnp.sum(c.astype(I32), axis=0, keepdims=True)
            fresh = mid > lo
            up = jnp.logical_and(fresh, c >= topk)
            down = jnp.logical_and(fresh, c < topk)
            return (jnp.where(up, mid, lo), jnp.where(down, mid, hi),
                    jnp.where(up, c, cnt_lo), jnp.where(down, c, cnt_hi))

        band, _, cnt_lo, cnt_hi = lax.fori_loop(
            0, HALF_BITS, halving,
            (jnp.full((1, t), half_lo, I32), jnp.full((1, t), half_hi, I32),
             jnp.full((1, t), topk + 1, I32), jnp.zeros((1, t), I32)))

        def low_halves(kb, carry):
            key = sc_ref[kb]
            high = key >> HALF_BITS
            low = (key & (2 * half_hi - 1)) + half_lo
            sch_ref[kb] = jnp.where(high == band, low, jnp.where(high > band, half_hi - 1, half_lo)).astype(I16)
            return carry

        lax.fori_loop(0, i + 1, low_halves, 0)
        low, _, cnt_lo, cnt_hi = lax.fori_loop(
            0, HALF_BITS, halving,
            (jnp.full((1, t), half_lo, I32), jnp.full((1, t), half_hi, I32), cnt_lo, cnt_hi))
        lo = (band << HALF_BITS) + (low - half_lo)
        thr_ref[...] = jnp.where(needs, lo, INT_MIN)

        tied = jnp.where(jnp.logical_and(needs, cnt_lo > topk), 1, 0)

        @pl.when(jnp.max(tied) > 0)
        def _():
            wanted = (topk - cnt_hi).astype(F32)
            tied_f = tied.astype(F32)
            earlier = jnp.where(lax.broadcasted_iota(I32, (t, t), 1) < lax.broadcasted_iota(I32, (t, t), 0),
                                1.0, 0.0).astype(BF16)

            def demote(kb, seen):
                key = sc_ref[kb]
                is_tie = jnp.where(key == lo, tied_f, 0.0)
                rank = seen + jnp.dot(earlier, is_tie.astype(BF16), preferred_element_type=F32)
                drop = is_tie * jnp.where(rank >= wanted, 1.0, 0.0)
                sc_ref[kb] = jnp.where(drop > 0.0, lo - 1, key)
                return seen + jnp.sum(is_tie, axis=0, keepdims=True)

            lax.fori_loop(0, i + 1, demote, jnp.zeros((1, t), F32))

    @pl.when(q0 + t <= topk)
    def _():
        thr_ref[...] = jnp.full((1, t), INT_MIN, I32)

    m_ref[...] = jnp.full(m_ref.shape, NEG, F32)
    acc_ref[...] = jnp.zeros(acc_ref.shape, F32)
    thr = thr_ref[...]

    def attn_block(kb, mode):
        ks = pl.ds(pl.multiple_of(kb * t, t), t)
        mask = jnp.where(sc_ref[kb] >= thr, 0.0, NEG)
        if mode == 0:
            mask = jnp.where(causal, mask, NEG)
        mask_ref[...] = mask

        def logits(h):
            g = h // GROUP
            s_ref[h][...] = _dot_nt(k_ref[ks, g * HEAD_DIM:(g + 1) * HEAD_DIM],
                                    q_ref[:, h * HEAD_DIM:(h + 1) * HEAD_DIM])

        def numerators(h):
            s = s_ref[h][...] + mask_ref[...]
            if mode != "far":
                s = s + bias_ref[h, mode]
            m_old = m_ref[h]
            m_new = jnp.maximum(m_old, jnp.max(s, axis=0, keepdims=True))
            p_ref[h][...] = jnp.exp2(s - m_new).astype(BF16)
            alpha_ref[h] = jnp.exp2(m_old - m_new)
            m_ref[h] = m_new

        def weighted_values(h):
            pv = jnp.dot(vt_ref[kb, h // GROUP], p_ref[h][...], preferred_element_type=F32)
            acc_ref[h] = alpha_ref[h] * acc_ref[h] + pv

        for stage in (logits, numerators, weighted_values):
            for h in range(N_HEADS):
                stage(h)

    def far_loop(kb, carry):
        attn_block(kb, "far")
        return carry

    lax.fori_loop(0, jnp.maximum(i - 1, 0), far_loop, 0)

    @pl.when(i >= 1)
    def _():
        attn_block(i - 1, 1)

    attn_block(i, 0)

    for h in range(N_HEADS):
        acc = acc_ref[h]
        out_t = acc[0:HEAD_DIM, :] / acc[HEAD_DIM:HEAD_DIM + 1, :]
        o_ref[:, h * HEAD_DIM:(h + 1) * HEAD_DIM] = out_t.T.astype(o_ref.dtype)


def _attention(proj, kiw, bias_tiles, batch, seq, q_col, layer, casts):
    t = ATT_BLOCK
    nq = seq // t
    steps = batch * nq
    cast_in, cast_out, cast_shapes = [], [], []
    for w, col_tile in casts:
        rows = w.shape[1] // steps
        assert rows * steps == w.shape[1] and rows % BF16_SUBLANES == 0
        cast_in.append(pl.BlockSpec((None, rows, w.shape[2]), lambda b, i: (layer, b * nq + i, 0)))
        if col_tile is None:
            cast_out.append(pl.BlockSpec((rows, w.shape[2]), lambda b, i: (b * nq + i, 0)))
            cast_shapes.append(jax.ShapeDtypeStruct(w.shape[1:], BF16))
        else:
            n_tiles = w.shape[2] // col_tile
            cast_out.append(pl.BlockSpec((n_tiles, rows, col_tile), lambda b, i: (0, b * nq + i, 0)))
            cast_shapes.append(jax.ShapeDtypeStruct((n_tiles, w.shape[1], col_tile), BF16))
    topk = min(TOPK_MAX, seq // 4)
    attn_dim = N_HEADS * HEAD_DIM
    kv_dim = N_KV_HEADS * HEAD_DIM
    qi_half = IDX_HEADS * IDX_DIM // 2
    k_col = q_col + attn_dim
    qi_col = k_col + 2 * kv_dim
    assert q_col % attn_dim == 0 and k_col % kv_dim == 0 and qi_col % qi_half == 0
    return pl.pallas_call(
        functools.partial(_attn_body, t=t, topk=topk, n_casts=len(casts)),
        grid=(batch, nq),
        in_specs=[
            pl.BlockSpec((t, attn_dim), lambda b, i: (b * nq + i, q_col // attn_dim)),
            pl.BlockSpec((seq, kv_dim), lambda b, i: (b, k_col // kv_dim)),
            pl.BlockSpec((seq, kv_dim), lambda b, i: (b, k_col // kv_dim + 1)),
            pl.BlockSpec((t, qi_half), lambda b, i: (b * nq + i, qi_col // qi_half)),
            pl.BlockSpec((t, qi_half), lambda b, i: (b * nq + i, qi_col // qi_half + 1)),
            pl.BlockSpec((seq, IDX_PAD), lambda b, i: (b, 0)),
            pl.BlockSpec((N_HEADS, 2, t, t), lambda b, i: (0, 0, 0, 0)),
        ] + cast_in,
        out_specs=[pl.BlockSpec((t, attn_dim), lambda b, i: (b * nq + i, 0))] + cast_out,
        out_shape=[jax.ShapeDtypeStruct((batch * seq, attn_dim), BF16)] + cast_shapes,
        scratch_shapes=[
            pltpu.VMEM((nq + 1, t, t), I32),
            pltpu.VMEM((nq + 1, t, t), I16),
            pltpu.VMEM((1, t), I32),
            pltpu.VMEM((nq, N_KV_HEADS, HEAD_DIM + ONES_ROWS, t), BF16),
            pltpu.VMEM((seq, IDX_DIM), BF16),
            pltpu.VMEM((IDX_HEADS, t, IDX_DIM), BF16),
            pltpu.VMEM((t, t), F32),
            pltpu.VMEM((N_HEADS, 1, t), F32),
            pltpu.VMEM((N_HEADS, 1, t), F32),
            pltpu.VMEM((N_HEADS, HEAD_DIM + ONES_ROWS, t), F32),
        ] + [pltpu.VMEM((t, t), F32)] * N_HEADS + [pltpu.VMEM((t, t), BF16)] * N_HEADS,
        compiler_params=_params("parallel", "arbitrary"),
        name="sparse_attention",
    )(proj, proj, proj, proj, proj, kiw, bias_tiles, *[w for w, _ in casts])


def _mixer_body(main_ref, halo_ref, ya_ref, g0_ref, g1_ref, g2_ref, wb_ref, pw_ref, ps_ref, cw_ref,
                o_ref, y_ref, acc_ref, *, tm, tn, seq, pool_dim, conv_dim):
    pos0 = (pl.program_id(0) * tm) % seq
    pos = lax.broadcasted_iota(I32, (tm, 1), 0) + pos0
    keep = jnp.where(pos0 == 0, 0.0, 1.0)
    n_out = o_ref.shape[-1]

    def branch(n, y, c0):
        return g_refs[n][:, c0:c0 + tn].astype(F32) * jnp.dot(y, wb_ref[n, :, c0:c0 + tn], preferred_element_type=F32)

    g_refs = (g0_ref, g1_ref, g2_ref)
    for c0 in range(0, n_out, tn):
        acc_ref[:, c0:c0 + tn] = branch(2, ya_ref[...], c0)

    def ext(c0, c1):
        halo = halo_ref[:, c0:c1].astype(F32) * keep
        return jnp.concatenate([halo, main_ref[:, c0:c1].astype(F32)], axis=0)

    for g, win in enumerate(POOL_WINDOWS):
        c0 = g * POOL_GROUP_DIM
        u = ext(c0, c0 + POOL_GROUP_DIM)
        s, sh = u, 1
        while sh < win:
            s = s + pltpu.roll(s, sh, axis=0)
            sh *= 2
        cnt = jnp.minimum(pos + 1, win).astype(F32)
        dlt = s[POOL_HALO:] / cnt - u[POOL_HALO:]
        yp = jnp.dot(dlt.astype(BF16), pw_ref[g], preferred_element_type=F32)
        y_ref[0, :, c0:c0 + POOL_GROUP_DIM] = (yp * ps_ref[:, c0:c0 + POOL_GROUP_DIM]).astype(BF16)

    z = ext(pool_dim, pool_dim + conv_dim) * ext(pool_dim + conv_dim, pool_dim + 2 * conv_dim)
    yc = z[POOL_HALO:] * cw_ref[CONV_WIDTH - 1:CONV_WIDTH, :]
    for tap in range(1, CONV_WIDTH):
        yc = yc + pltpu.roll(z, tap, axis=0)[POOL_HALO:] * cw_ref[CONV_WIDTH - 1 - tap:CONV_WIDTH - tap, :]
    gate_b = main_ref[:, pool_dim + 2 * conv_dim:pool_dim + 3 * conv_dim].astype(F32)
    y_ref[1] = (gate_b * yc).astype(BF16)

    for c0 in range(0, n_out, tn):
        mixed = acc_ref[:, c0:c0 + tn] + branch(0, y_ref[0], c0) + branch(1, y_ref[1], c0)
        o_ref[:, c0:c0 + tn] = mixed.astype(o_ref.dtype)


def _mixer(proj, gate_col, y_attn, w_branch, pool_w, pool_scale, conv_w, layer, seq, *, tm, tn):
    main = gates = proj
    tok = proj.shape[0]
    branch_dim = y_attn.shape[-1]
    width = 4 * branch_dim
    d = w_branch.shape[-1]
    assert gate_col % d == 0 and d % tn == 0
    gj = gate_col // d
    halo_blocks = tm // POOL_HALO
    resident = dict(pipeline_mode=pl.Buffered(1))
    return pl.pallas_call(
        functools.partial(_mixer_body, tm=tm, tn=tn, seq=seq, pool_dim=branch_dim, conv_dim=branch_dim),
        grid=(tok // tm,),
        in_specs=[
            pl.BlockSpec((tm, width), lambda i: (i, 0)),
            pl.BlockSpec((POOL_HALO, width), lambda i: (jnp.maximum(i * halo_blocks - 1, 0), 0)),
            pl.BlockSpec((tm, branch_dim), lambda i: (i, 0)),
            pl.BlockSpec((tm, d), lambda i: (i, gj)),
            pl.BlockSpec((tm, d), lambda i: (i, gj + 1)),
            pl.BlockSpec((tm, d), lambda i: (i, gj + 2)),
            pl.BlockSpec((None, 3, branch_dim, d), lambda i: (0, 0, 0, 0), **resident),
            pl.BlockSpec((None,) + pool_w.shape[1:], lambda i: (layer, 0, 0, 0), **resident),
            pl.BlockSpec((None, 1, branch_dim), lambda i: (layer, 0, 0)),
            pl.BlockSpec((None, CONV_WIDTH, branch_dim), lambda i: (layer, 0, 0)),
        ],
        out_specs=pl.BlockSpec((tm, d), lambda i: (i, 0)),
        out_shape=jax.ShapeDtypeStruct((tok, d), BF16),
        scratch_shapes=[pltpu.VMEM((2, tm, branch_dim), BF16), pltpu.VMEM((tm, d), F32)],
        compiler_params=_params("parallel"),
        name="mixer_branches",
    )(main, main, y_attn, gates, gates, gates, w_branch, pool_w, pool_scale, conv_w)


def _proj_norm_res_body(a_ref, w_ref, x_ref, g_ref, o_ref):
    half = a_ref.shape[0] // 2
    for rows in (slice(0, half), slice(half, 2 * half)):
        m = jnp.dot(a_ref[rows, :], w_ref[...], preferred_element_type=F32)
        o_ref[rows, :] = x_ref[rows, :] + _rms(m, g_ref[...])


def _proj_norm_res(a, w, layer, x, gain, *, tm):
    tok, d = x.shape
    k = a.shape[-1]
    return pl.pallas_call(
        _proj_norm_res_body,
        grid=(tok // tm,),
        in_specs=[
            pl.BlockSpec((tm, k), lambda i: (i, 0)),
            pl.BlockSpec((None, k, d), lambda i: (layer, 0, 0), pipeline_mode=pl.Buffered(1)),
            pl.BlockSpec((tm, d), lambda i: (i, 0)),
            pl.BlockSpec((1, d), lambda i: (0, 0)),
        ],
        out_specs=pl.BlockSpec((tm, d), lambda i: (i, 0)),
        out_shape=jax.ShapeDtypeStruct((tok, d), F32),
        compiler_params=_params("parallel"),
        name="out_proj",
    )(a, w, x, gain)


def _mlp_body(x_hbm, gpre_ref, gpost_ref, wu_ref, wd_ref, o_hbm, h_ref, acc_ref, x_ring, o_buf, x_sem, o_sem,
              *, n_tiles):
    i, j = pl.program_id(0), pl.program_id(1)
    tm = h_ref.shape[0]
    chunks = [slice(r, r + NORM_ROWS) for r in range(0, tm, NORM_ROWS)]

    def tile_rows(tile):
        return pl.ds(pl.multiple_of(tile * tm, tm), tm)

    def x_copy(tile):
        return pltpu.make_async_copy(x_hbm.at[tile_rows(tile), :], x_ring.at[tile % 2], x_sem.at[tile % 2])

    def o_copy(tile):
        return pltpu.make_async_copy(o_buf, o_hbm.at[tile_rows(tile), :], o_sem)

    def finish_tile(tile):
        x_tile = x_ring.at[tile % 2]
        for rows in chunks:
            o_buf[rows, :] = x_tile[rows, :] + _rms(acc_ref[rows, :], gpost_ref[...])

    def start_tile(tile):
        x_tile = x_ring.at[tile % 2]
        for rows in chunks:
            h_ref[rows, :] = _rms(x_tile[rows, :], gpre_ref[...]).astype(BF16)

    def hidden_tile():
        hid = jnp.dot(h_ref[...], wu_ref[...], preferred_element_type=F32)
        hid = jnp.square(jnp.maximum(hid, 0.0)).astype(BF16)
        return jnp.dot(hid, wd_ref[...], preferred_element_type=F32)

    @pl.when(jnp.logical_and(j == 0, i == 0))
    def _():
        x_copy(0).start()

    @pl.when(jnp.logical_and(j == 1, i + 1 < n_tiles))
    def _():
        x_copy(i + 1).start()

    @pl.when(jnp.logical_and(j == 0, i == 0))
    def _():
        x_copy(0).wait()
        start_tile(0)
        acc_ref[...] = hidden_tile()

    @pl.when(jnp.logical_and(j == 0, i == 1))
    def _():
        x_copy(1).wait()
        finish_tile(0)
        start_tile(1)
        acc_ref[...] = hidden_tile()
        o_copy(0).start()

    @pl.when(jnp.logical_and(j == 0, i > 1))
    def _():
        o_copy(i - 2).wait()
        x_copy(i).wait()
        finish_tile(i - 1)
        start_tile(i)
        acc_ref[...] = hidden_tile()
        o_copy(i - 1).start()

    @pl.when(j > 0)
    def _():
        acc_ref[...] += hidden_tile()

    @pl.when(jnp.logical_and(j == pl.num_programs(1) - 1, i == n_tiles - 1))
    def _():
        if n_tiles > 1:
            o_copy(n_tiles - 2).wait()
        finish_tile(n_tiles - 1)
        o_copy(n_tiles - 1).start()
        o_copy(n_tiles - 1).wait()


def _mlp(x, gpre, gpost, w_up_tiles, w_down, *, tm):
    tok, d = x.shape
    n_hidden_tiles, _, th = w_up_tiles.shape
    n_tiles = tok // tm
    assert tm % NORM_ROWS == 0 and n_hidden_tiles >= 2
    return pl.pallas_call(
        functools.partial(_mlp_body, n_tiles=n_tiles),
        grid=(n_tiles, n_hidden_tiles),
        in_specs=[
            pl.BlockSpec(memory_space=pl.ANY),
            pl.BlockSpec((1, d), lambda i, j: (0, 0)),
            pl.BlockSpec((1, d), lambda i, j: (0, 0)),
            pl.BlockSpec((None, d, th), lambda i, j: (j, 0, 0)),
            pl.BlockSpec((th, d), lambda i, j: (j, 0)),
        ],
        out_specs=pl.BlockSpec(memory_space=pl.ANY),
        out_shape=jax.ShapeDtypeStruct((tok, d), F32),
        scratch_shapes=[
            pltpu.VMEM((tm, d), BF16),
            pltpu.VMEM((tm, d), F32),
            pltpu.VMEM((2, tm, d), F32),
            pltpu.VMEM((tm, d), F32),
            pltpu.SemaphoreType.DMA((2,)),
            pltpu.SemaphoreType.DMA(()),
        ],
        compiler_params=_params("arbitrary", "arbitrary"),
        name="mlp",
    )(x, gpre, gpost, w_up_tiles, w_down)


def _tile(n, pref):
    t = min(n, pref)
    assert n % t == 0, (n, pref)
    return t


@jax.jit
def _forward(x, norm_gains, w_in, pool_w, pool_scale, conv_w, rel_bias, w_branch,
             w_gate, b_gate, w_out, w_up, w_down):
    batch, seq, d = x.shape
    depth = w_in.shape[0]
    tok = batch * seq
    branch_dim = d // 2
    attn_dim = N_HEADS * HEAD_DIM
    kv_dim = N_KV_HEADS * HEAD_DIM
    qi_dim = IDX_HEADS * IDX_DIM
    assert branch_dim == attn_dim == len(POOL_WINDOWS) * POOL_GROUP_DIM
    assert seq % ATT_BLOCK == 0 and ATT_BLOCK >= MAX_DISTANCE

    main_w = 4 * branch_dim
    n_in = main_w + attn_dim + 2 * kv_dim + qi_dim
    kw_w = IDX_DIM + IDX_HEADS
    assert w_in.shape[-1] == n_in + kw_w
    w_in_t = jnp.swapaxes(w_in, 1, 2)
    w_kw_t = jnp.pad(w_in_t[:, n_in:, :], ((0, 0), (0, IDX_PAD - kw_w), (0, 0))).astype(BF16)
    later_weights = ((w_branch.reshape(depth, 3 * branch_dim, d), None), (w_out, None),
                     (w_up, MLP_HIDDEN_TILE), (w_down, None))
    pool_w_b = pool_w.astype(BF16)

    gate_col = main_w
    q_col = main_w + 3 * d
    scale = jnp.concatenate([jnp.ones((q_col,), F32), jnp.full((attn_dim,), HEAD_DIM ** -0.5 * LOG2E, F32),
                             jnp.ones((n_in - main_w - attn_dim,), F32)])

    bias_tiles = _bias_tiles(rel_bias, ATT_BLOCK)

    tm = _tile(tok, 1024)
    tm_s = _tile(seq, 512)
    xf = x.reshape(tok, d)
    for l in range(depth):
        g = norm_gains[l]
        g0, g1, g2, g3 = (g[n:n + 1] for n in range(4))
        bias = jnp.concatenate([jnp.zeros((main_w,), F32), b_gate[l].reshape(3 * d), jnp.zeros((n_in - main_w,), F32)])
        proj, kiw = _projections(xf, g0, w_in_t, w_gate, w_kw_t, l, jnp.stack([scale, bias]), main_w, n_in,
                                 tm=_tile(tok, 2048), tn=512)
        y_attn, w_branch_b, w_out_b, w_up_b, w_down_b = _attention(proj, kiw, bias_tiles, batch, seq, q_col,
                                                                    l, later_weights)
        mixed = _mixer(proj, gate_col, y_attn, w_branch_b.reshape(1, 3, branch_dim, d), pool_w_b,
                       pool_scale.reshape(depth, 1, branch_dim), conv_w, l, seq, tm=tm_s, tn=1024)
        xf = _proj_norm_res(mixed, w_out_b[None], 0, xf, g1, tm=tm)
        xf = _mlp(xf, g2, g3, w_up_b, w_down_b, tm=tm)
    return xf.reshape(batch, seq, d)


def kernel(x, norm_gains, w_in, pool_w, pool_scale, conv_w, rel_bias, w_branch, w_gate, b_gate, w_out, w_up, w_down):
    return _forward(x, norm_gains, w_in, pool_w, pool_scale, conv_w, rel_bias, w_branch,
                    w_gate, b_gate, w_out, w_up, w_down)
```

```python
import functools
import math

import jax
import jax.numpy as jnp
import numpy as np
from jax import lax
from jax.experimental import pallas as pl
from jax.experimental.pallas import tpu as pltpu

F32 = jnp.float32
BF16 = jnp.bfloat16
I32 = jnp.int32
I16 = jnp.int16

NORM_EPS = 1e-6
POOL_WINDOWS = (2, 4, 8, 16)
POOL_GROUP_DIM = 256
POOL_HALO = 16
CONV_WIDTH = 3
N_HEADS = 8
N_KV_HEADS = 2
GROUP = N_HEADS // N_KV_HEADS
HEAD_DIM = 128
IDX_HEADS = 16
IDX_DIM = 64
TOPK_MAX = 256
NUM_BUCKETS = 32
MAX_EXACT = NUM_BUCKETS // 2
MAX_DISTANCE = 128
ATT_BLOCK = 256
IDX_PAD = 128
HALF_BITS = 16
INT_MIN = -(2 ** 31)
INT_MAX = 2 ** 31 - 1
NEG = -1e30
LOG2E = math.log2(math.e)
ONES_ROWS = 16
VMEM_LIMIT = 56 * 1024 * 1024
MXU_WIDTH = 256
MLP_HIDDEN_TILE = 1024
BF16_SUBLANES = 16
NORM_ROWS = 256


def _bucket_thresholds():
    n = np.arange(MAX_EXACT, 8 * MAX_DISTANCE, dtype=np.int64)
    nf = n.astype(np.float32)
    large = MAX_EXACT + (np.log(nf / np.float32(MAX_EXACT)) / np.float32(math.log(MAX_DISTANCE / MAX_EXACT))
                         * np.float32(NUM_BUCKETS - MAX_EXACT)).astype(np.int32)
    large = np.minimum(large, NUM_BUCKETS - 1)
    thr = []
    for b in range(MAX_EXACT + 1, NUM_BUCKETS):
        thr.append(int(n[np.argmax(large >= b)]))
    return tuple(thr)


BUCKET_THRESHOLDS = _bucket_thresholds()


def _params(*sem):
    return pltpu.CompilerParams(dimension_semantics=sem, vmem_limit_bytes=VMEM_LIMIT)


def _rms(x, gain):
    ms = jnp.mean(x * x, axis=-1, keepdims=True)
    return x * lax.rsqrt(ms + NORM_EPS) * gain


def _dot_nt(a, b):
    return lax.dot_general(a, b, (((1,), (1,)), ((), ())), preferred_element_type=F32)


def _proj_body(x_hbm, g_ref, wint_ref, wg_ref, wkwt_ref, post_ref, o_ref, okw_ref, h_ref, x_buf, x_sem,
               *, gate_lo, gate_hi):
    i, j = pl.program_id(0), pl.program_id(1)
    tm = x_buf.shape[0]
    is_gate = jnp.logical_and(j >= gate_lo, j < gate_hi)

    def x_copy(tile):
        rows = pl.ds(pl.multiple_of(tile * tm, tm), tm)
        return pltpu.make_async_copy(x_hbm.at[rows, :], x_buf, x_sem)

    @pl.when(jnp.logical_and(i == 0, j == 0))
    def _():
        x_copy(0).start()

    @pl.when(j == 0)
    def _():
        x_copy(i).wait()

        def norm_rows(r, carry):
            rows = pl.ds(pl.multiple_of(r * NORM_ROWS, NORM_ROWS), NORM_ROWS)
            h = _rms(x_buf[rows, :], g_ref[...]).astype(BF16)
            h_ref[rows, :] = h
            okw_ref[rows, :] = _dot_nt(h, wkwt_ref[...])
            return carry

        lax.fori_loop(0, tm // NORM_ROWS, norm_rows, 0)

    @pl.when(jnp.logical_and(j == 1, i + 1 < pl.num_programs(0)))
    def _():
        x_copy(i + 1).start()

    chunks = [slice(c, c + MXU_WIDTH) for c in range(0, o_ref.shape[-1], MXU_WIDTH)]

    @pl.when(jnp.logical_not(is_gate))
    def _():
        for cs in chunks:
            y = _dot_nt(h_ref[...], wint_ref[cs, :].astype(BF16))
            o_ref[:, cs] = (y * post_ref[0:1, cs]).astype(o_ref.dtype)

    @pl.when(is_gate)
    def _():
        for cs in chunks:
            y = jnp.dot(h_ref[...], wg_ref[:, cs].astype(BF16), preferred_element_type=F32) + post_ref[1:2, cs]
            o_ref[:, cs] = (0.5 * jnp.tanh(0.5 * y) + 0.5).astype(o_ref.dtype)


def _projections(x, gain, w_in_t, w_gate, w_kw_t, layer, post, n_main, n_in, *, tm, tn):
    tok, d = x.shape
    n_gate = w_gate.shape[-1]
    assert n_main % tn == 0 and n_in % tn == 0 and n_gate % tn == 0 and tm % NORM_ROWS == 0
    gate_lo = n_main // tn
    gate_blocks = n_gate // tn
    gate_hi = gate_lo + gate_blocks
    kw = w_kw_t.shape[1]
    return pl.pallas_call(
        functools.partial(_proj_body, gate_lo=gate_lo, gate_hi=gate_hi),
        grid=(tok // tm, (n_in + n_gate) // tn),
        in_specs=[
            pl.BlockSpec(memory_space=pl.ANY),
            pl.BlockSpec((1, d), lambda i, j: (0, 0)),
            pl.BlockSpec((None, tn, d),
                         lambda i, j: (layer, jnp.where(j < gate_lo, j, jnp.maximum(j - gate_blocks, gate_lo)), 0)),
            pl.BlockSpec((None, d, tn), lambda i, j: (layer, 0, jnp.clip(j - gate_lo, 0, gate_blocks - 1))),
            pl.BlockSpec((None, kw, d), lambda i, j: (layer, 0, 0)),
            pl.BlockSpec((2, tn), lambda i, j: (0, j)),
        ],
        out_specs=[
            pl.BlockSpec((tm, tn), lambda i, j: (i, j)),
            pl.BlockSpec((tm, kw), lambda i, j: (i, 0)),
        ],
        out_shape=[
            jax.ShapeDtypeStruct((tok, n_in + n_gate), BF16),
            jax.ShapeDtypeStruct((tok, kw), F32),
        ],
        scratch_shapes=[pltpu.VMEM((tm, d), BF16), pltpu.VMEM((tm, d), F32), pltpu.SemaphoreType.DMA(())],
        compiler_params=_params("arbitrary", "arbitrary"),
        name="projections",
    )(x, gain, w_in_t, w_gate, w_kw_t, post)


def _bias_tiles_body(rel_ref, o_ref, *, t):
    row = lax.broadcasted_iota(I32, (t, t), 0)
    col = lax.broadcasted_iota(I32, (t, t), 1)
    for off in range(2):
        n = jnp.maximum(col - row + off * t, 0)
        large = jnp.full((t, t), MAX_EXACT, I32)
        for thr in BUCKET_THRESHOLDS:
            large = large + jnp.where(n >= thr, 1, 0)
        bucket = jnp.where(n < MAX_EXACT, n, large)
        for h in range(N_HEADS):
            val = jnp.zeros((t, t), F32)
            for b in range(NUM_BUCKETS):
                val = jnp.where(bucket == b, rel_ref[b, h], val)
            o_ref[h, off] = (val - rel_ref[NUM_BUCKETS - 1, h]) * LOG2E


def _bias_tiles(rel_bias, t):
    return pl.pallas_call(
        functools.partial(_bias_tiles_body, t=t),
        in_specs=[pl.BlockSpec(memory_space=pltpu.SMEM)],
        out_specs=pl.BlockSpec(memory_space=pltpu.VMEM),
        out_shape=jax.ShapeDtypeStruct((N_HEADS, 2, t, t), F32),
        compiler_params=pltpu.CompilerParams(vmem_limit_bytes=VMEM_LIMIT),
        name="bias_tiles",
    )(rel_bias)


def _attn_body(*refs, t, topk, n_casts):
    (q_ref, k_ref, v_ref, qia_ref, qib_ref, kiw_ref, bias_ref), refs = refs[:7], refs[7:]
    cast_src, refs = refs[:n_casts], refs[n_casts:]
    o_ref, cast_dst, refs = refs[0], refs[1:1 + n_casts], refs[1 + n_casts:]
    (sc_ref, sch_ref, thr_ref, vt_ref, kib_ref, qih_ref, mask_ref, alpha_ref, m_ref, acc_ref), refs = (
        refs[:10], refs[10:])
    s_ref, p_ref = refs[:N_HEADS], refs[N_HEADS:]

    for src, dst in zip(cast_src, cast_dst):
        if len(dst.shape) == 2:
            dst[...] = src[...].astype(BF16)
        else:
            width = dst.shape[-1]
            for c in range(dst.shape[0]):
                dst[c] = src[:, c * width:(c + 1) * width].astype(BF16)

    i = pl.program_id(1)
    nkb = vt_ref.shape[0]
    q0 = pl.multiple_of(i * t, t)
    key_pos = lax.broadcasted_iota(I32, (t, 1), 0)
    qry_pos = lax.broadcasted_iota(I32, (1, t), 1)
    causal = key_pos <= qry_pos

    @pl.when(i == 0)
    def _():
        for kb in range(nkb):
            rows = slice(kb * t, (kb + 1) * t)
            vb = v_ref[rows, :].astype(F32)
            for g in range(N_KV_HEADS):
                vt_ref[kb, g, 0:HEAD_DIM, :] = vb[:, g * HEAD_DIM:(g + 1) * HEAD_DIM].T.astype(BF16)
                vt_ref[kb, g, HEAD_DIM:, :] = jnp.ones((vt_ref.shape[2] - HEAD_DIM, t), BF16)
            kib_ref[rows, :] = kiw_ref[rows, 0:IDX_DIM].astype(BF16)

    half = IDX_HEADS // 2
    for h in range(IDX_HEADS):
        src = qia_ref if h < half else qib_ref
        qih_ref[h] = src[:, (h % half) * IDX_DIM:(h % half + 1) * IDX_DIM]
    w_t = kiw_ref[pl.ds(q0, t), :].T[IDX_DIM:IDX_DIM + IDX_HEADS, :] * (IDX_HEADS ** -0.5)

    def score_block(kb, diag):
        kblk = kib_ref[pl.ds(pl.multiple_of(kb * t, t), t), :]
        acc = jnp.zeros((t, t), F32)
        for h in range(IDX_HEADS):
            acc = acc + w_t[h:h + 1, :] * jnp.maximum(_dot_nt(kblk, qih_ref[h]), 0.0)
        if diag:
            acc = jnp.where(causal, acc, -jnp.inf)
        bits = pltpu.bitcast(acc, I32)
        bits = jnp.where(bits == INT_MIN, 0, bits)
        key = bits ^ ((bits >> 31) & INT_MAX)
        sc_ref[kb] = key
        sch_ref[kb] = (key >> HALF_BITS).astype(I16)

    def score_loop(kb, carry):
        score_block(kb, False)
        return carry

    lax.fori_loop(0, i, score_loop, 0)
    score_block(i, True)

    @pl.when(i % 2 == 0)
    def _():
        sc_ref[i + 1] = jnp.full((t, t), INT_MIN, I32)
        sch_ref[i + 1] = jnp.full((t, t), INT_MIN >> HALF_BITS, I16)

    @pl.when(q0 + t > topk)
    def _():
        needs = q0 + qry_pos >= topk
        rows = t // BF16_SUBLANES
        half_lo, half_hi = INT_MIN >> HALF_BITS, -(INT_MIN >> HALF_BITS)

        def halving(_, state):
            lo, hi, cnt_lo, cnt_hi = state
            mid = (lo & hi) + ((lo ^ hi) >> 1)
            mid_c = mid.astype(I16)

            def count(pair, c):
                for kb in (2 * pair, 2 * pair + 1):
                    ge = jnp.where(sch_ref[kb] >= mid_c, jnp.int16(1), jnp.int16(0))
                    for r in range(rows):
                        c = c + ge[r * BF16_SUBLANES:(r + 1) * BF16_SUBLANES]
                return c

            c = lax.fori_loop(0, (i + 2) // 2, count, jnp.zeros((BF16_SUBLANES, t), I16))
            c = jnp.sum(c.astype(I32), axis=0, keepdims=True)
            fresh = mid > lo
            up = jnp.logical_and(fresh, c >= topk)
            down = jnp.logical_and(fresh, c < topk)
            return (jnp.where(up, mid, lo), jnp.where(down, mid, hi),
                    jnp.where(up, c, cnt_lo), jnp.where(down, c, cnt_hi))

        band, _, cnt_lo, cnt_hi = lax.fori_loop(
            0, HALF_BITS, halving,
            (jnp.full((1, t), half_lo, I32), jnp.full((1, t), half_hi, I32),
             jnp.full((1, t), topk + 1, I32), jnp.zeros((1, t), I32)))

        def low_halves(kb, carry):
            key = sc_ref[kb]
            high = key >> HALF_BITS
            low = (key & (2 * half_hi - 1)) + half_lo
            sch_ref[kb] = jnp.where(high == band, low, jnp.where(high > band, half_hi - 1, half_lo)).astype(I16)
            return carry

        lax.fori_loop(0, i + 1, low_halves, 0)
        low, _, cnt_lo, cnt_hi = lax.fori_loop(
            0, HALF_BITS, halving,
            (jnp.full((1, t), half_lo, I32), jnp.full((1, t), half_hi, I32), cnt_lo, cnt_hi))
        lo = (band << HALF_BITS) + (low - half_lo)
        thr_ref[...] = jnp.where(needs, lo, INT_MIN)

        tied = jnp.where(jnp.logical_and(needs, cnt_lo > topk), 1, 0)

        @pl.when(jnp.max(tied) > 0)
        def _():
            wanted = (topk - cnt_hi).astype(F32)
            tied_f = tied.astype(F32)
            earlier = jnp.where(lax.broadcasted_iota(I32, (t, t), 1) < lax.broadcasted_iota(I32, (t, t), 0),
                                1.0, 0.0).astype(BF16)

            def demote(kb, seen):
                key = sc_ref[kb]
                is_tie = jnp.where(key == lo, tied_f, 0.0)
                rank = seen + jnp.dot(earlier, is_tie.astype(BF16), preferred_element_type=F32)
                drop = is_tie * jnp.where(rank >= wanted, 1.0, 0.0)
                sc_ref[kb] = jnp.where(drop > 0.0, lo - 1, key)
                return seen + jnp.sum(is_tie, axis=0, keepdims=True)

            lax.fori_loop(0, i + 1, demote, jnp.zeros((1, t), F32))

    @pl.when(q0 + t <= topk)
    def _():
        thr_ref[...] = jnp.full((1, t), INT_MIN, I32)

    m_ref[...] = jnp.full(m_ref.shape, NEG, F32)
    acc_ref[...] = jnp.zeros(acc_ref.shape, F32)
    thr = thr_ref[...]

    def attn_block(kb, mode):
        ks = pl.ds(pl.multiple_of(kb * t, t), t)
        mask = jnp.where(sc_ref[kb] >= thr, 0.0, NEG)
        if mode == 0:
            mask = jnp.where(causal, mask, NEG)
        mask_ref[...] = mask

        def logits(h):
            g = h // GROUP
            s_ref[h][...] = _dot_nt(k_ref[ks, g * HEAD_DIM:(g + 1) * HEAD_DIM],
                                    q_ref[:, h * HEAD_DIM:(h + 1) * HEAD_DIM])

        def numerators(h):
            s = s_ref[h][...] + mask_ref[...]
            if mode != "far":
                s = s + bias_ref[h, mode]
            m_old = m_ref[h]
            m_new = jnp.maximum(m_old, jnp.max(s, axis=0, keepdims=True))
            p_ref[h][...] = jnp.exp2(s - m_new).astype(BF16)
            alpha_ref[h] = jnp.exp2(m_old - m_new)
            m_ref[h] = m_new

        def weighted_values(h):
            pv = jnp.dot(vt_ref[kb, h // GROUP], p_ref[h][...], preferred_element_type=F32)
            acc_ref[h] = alpha_ref[h] * acc_ref[h] + pv

        for stage in (logits, numerators, weighted_values):
            for h in range(N_HEADS):
                stage(h)

    def far_loop(kb, carry):
        attn_block(kb, "far")
        return carry

    lax.fori_loop(0, jnp.maximum(i - 1, 0), far_loop, 0)

    @pl.when(i >= 1)
    def _():
        attn_block(i - 1, 1)

    attn_block(i, 0)

    for h in range(N_HEADS):
        acc = acc_ref[h]
        out_t = acc[0:HEAD_DIM, :] / acc[HEAD_DIM:HEAD_DIM + 1, :]
        o_ref[:, h * HEAD_DIM:(h + 1) * HEAD_DIM] = out_t.T.astype(o_ref.dtype)


def _attention(proj, kiw, bias_tiles, batch, seq, q_col, layer, casts):
    t = ATT_BLOCK
    nq = seq // t
    steps = batch * nq
    cast_in, cast_out, cast_shapes = [], [], []
    for w, col_tile in casts:
        rows = w.shape[1] // steps
        assert rows * steps == w.shape[1] and rows % BF16_SUBLANES == 0
        cast_in.append(pl.BlockSpec((None, rows, w.shape[2]), lambda b, i: (layer, b * nq + i, 0)))
        if col_tile is None:
            cast_out.append(pl.BlockSpec((rows, w.shape[2]), lambda b, i: (b * nq + i, 0)))
            cast_shapes.append(jax.ShapeDtypeStruct(w.shape[1:], BF16))
        else:
            n_tiles = w.shape[2] // col_tile
            cast_out.append(pl.BlockSpec((n_tiles, rows, col_tile), lambda b, i: (0, b * nq + i, 0)))
            cast_shapes.append(jax.ShapeDtypeStruct((n_tiles, w.shape[1], col_tile), BF16))
    topk = min(TOPK_MAX, seq // 4)
    attn_dim = N_HEADS * HEAD_DIM
    kv_dim = N_KV_HEADS * HEAD_DIM
    qi_half = IDX_HEADS * IDX_DIM // 2
    k_col = q_col + attn_dim
    qi_col = k_col + 2 * kv_dim
    assert q_col % attn_dim == 0 and k_col % kv_dim == 0 and qi_col % qi_half == 0
    return pl.pallas_call(
        functools.partial(_attn_body, t=t, topk=topk, n_casts=len(casts)),
        grid=(batch, nq),
        in_specs=[
            pl.BlockSpec((t, attn_dim), lambda b, i: (b * nq + i, q_col // attn_dim)),
            pl.BlockSpec((seq, kv_dim), lambda b, i: (b, k_col // kv_dim)),
            pl.BlockSpec((seq, kv_dim), lambda b, i: (b, k_col // kv_dim + 1)),
            pl.BlockSpec((t, qi_half), lambda b, i: (b * nq + i, qi_col // qi_half)),
            pl.BlockSpec((t, qi_half), lambda b, i: (b * nq + i, qi_col // qi_half + 1)),
            pl.BlockSpec((seq, IDX_PAD), lambda b, i: (b, 0)),
            pl.BlockSpec((N_HEADS, 2, t, t), lambda b, i: (0, 0, 0, 0)),
        ] + cast_in,
        out_specs=[pl.BlockSpec((t, attn_dim), lambda b, i: (b * nq + i, 0))] + cast_out,
        out_shape=[jax.ShapeDtypeStruct((batch * seq, attn_dim), BF16)] + cast_shapes,
        scratch_shapes=[
            pltpu.VMEM((nq + 1, t, t), I32),
            pltpu.VMEM((nq + 1, t, t), I16),
            pltpu.VMEM((1, t), I32),
            pltpu.VMEM((nq, N_KV_HEADS, HEAD_DIM + ONES_ROWS, t), BF16),
            pltpu.VMEM((seq, IDX_DIM), BF16),
            pltpu.VMEM((IDX_HEADS, t, IDX_DIM), BF16),
            pltpu.VMEM((t, t), F32),
            pltpu.VMEM((N_HEADS, 1, t), F32),
            pltpu.VMEM((N_HEADS, 1, t), F32),
            pltpu.VMEM((N_HEADS, HEAD_DIM + ONES_ROWS, t), F32),
        ] + [pltpu.VMEM((t, t), F32)] * N_HEADS + [pltpu.VMEM((t, t), BF16)] * N_HEADS,
        compiler_params=_params("parallel", "arbitrary"),
        name="sparse_attention",
    )(proj, proj, proj, proj, proj, kiw, bias_tiles, *[w for w, _ in casts])


def _mixer_body(main_ref, halo_ref, ya_ref, g0_ref, g1_ref, g2_ref, wb_ref, pw_ref, ps_ref, cw_ref,
                o_ref, y_ref, acc_ref, *, tm, tn, seq, pool_dim, conv_dim):
    pos0 = (pl.program_id(0) * tm) % seq
    pos = lax.broadcasted_iota(I32, (tm, 1), 0) + pos0
    keep = jnp.where(pos0 == 0, 0.0, 1.0)
    n_out = o_ref.shape[-1]

    def branch(n, y, c0):
        return g_refs[n][:, c0:c0 + tn].astype(F32) * jnp.dot(y, wb_ref[n, :, c0:c0 + tn], preferred_element_type=F32)

    g_refs = (g0_ref, g1_ref, g2_ref)
    for c0 in range(0, n_out, tn):
        acc_ref[:, c0:c0 + tn] = branch(2, ya_ref[...], c0)

    def ext(c0, c1):
        halo = halo_ref[:, c0:c1].astype(F32) * keep
        return jnp.concatenate([halo, main_ref[:, c0:c1].astype(F32)], axis=0)

    for g, win in enumerate(POOL_WINDOWS):
        c0 = g * POOL_GROUP_DIM
        u = ext(c0, c0 + POOL_GROUP_DIM)
        s, sh = u, 1
        while sh < win:
            s = s + pltpu.roll(s, sh, axis=0)
            sh *= 2
        cnt = jnp.minimum(pos + 1, win).astype(F32)
        dlt = s[POOL_HALO:] / cnt - u[POOL_HALO:]
        yp = jnp.dot(dlt.astype(BF16), pw_ref[g], preferred_element_type=F32)
        y_ref[0, :, c0:c0 + POOL_GROUP_DIM] = (yp * ps_ref[:, c0:c0 + POOL_GROUP_DIM]).astype(BF16)

    z = ext(pool_dim, pool_dim + conv_dim) * ext(pool_dim + conv_dim, pool_dim + 2 * conv_dim)
    yc = z[POOL_HALO:] * cw_ref[CONV_WIDTH - 1:CONV_WIDTH, :]
    for tap in range(1, CONV_WIDTH):
        yc = yc + pltpu.roll(z, tap, axis=0)[POOL_HALO:] * cw_ref[CONV_WIDTH - 1 - tap:CONV_WIDTH - tap, :]
    gate_b = main_ref[:, pool_dim + 2 * conv_dim:pool_dim + 3 * conv_dim].astype(F32)
    y_ref[1] = (gate_b * yc).astype(BF16)

    for c0 in range(0, n_out, tn):
        mixed = acc_ref[:, c0:c0 + tn] + branch(0, y_ref[0], c0) + branch(1, y_ref[1], c0)
        o_ref[:, c0:c0 + tn] = mixed.astype(o_ref.dtype)


def _mixer(proj, gate_col, y_attn, w_branch, pool_w, pool_scale, conv_w, layer, seq, *, tm, tn):
    main = gates = proj
    tok = proj.shape[0]
    branch_dim = y_attn.shape[-1]
    width = 4 * branch_dim
    d = w_branch.shape[-1]
    assert gate_col % d == 0 and d % tn == 0
    gj = gate_col // d
    halo_blocks = tm // POOL_HALO
    resident = dict(pipeline_mode=pl.Buffered(1))
    return pl.pallas_call(
        functools.partial(_mixer_body, tm=tm, tn=tn, seq=seq, pool_dim=branch_dim, conv_dim=branch_dim),
        grid=(tok // tm,),
        in_specs=[
            pl.BlockSpec((tm, width), lambda i: (i, 0)),
            pl.BlockSpec((POOL_HALO, width), lambda i: (jnp.maximum(i * halo_blocks - 1, 0), 0)),
            pl.BlockSpec((tm, branch_dim), lambda i: (i, 0)),
            pl.BlockSpec((tm, d), lambda i: (i, gj)),
            pl.BlockSpec((tm, d), lambda i: (i, gj + 1)),
            pl.BlockSpec((tm, d), lambda i: (i, gj + 2)),
            pl.BlockSpec((None, 3, branch_dim, d), lambda i: (0, 0, 0, 0), **resident),
            pl.BlockSpec((None,) + pool_w.shape[1:], lambda i: (layer, 0, 0, 0), **resident),
            pl.BlockSpec((None, 1, branch_dim), lambda i: (layer, 0, 0)),
            pl.BlockSpec((None, CONV_WIDTH, branch_dim), lambda i: (layer, 0, 0)),
        ],
        out_specs=pl.BlockSpec((tm, d), lambda i: (i, 0)),
        out_shape=jax.ShapeDtypeStruct((tok, d), BF16),
        scratch_shapes=[pltpu.VMEM((2, tm, branch_dim), BF16), pltpu.VMEM((tm, d), F32)],
        compiler_params=_params("parallel"),
        name="mixer_branches",
    )(main, main, y_attn, gates, gates, gates, w_branch, pool_w, pool_scale, conv_w)


def _proj_norm_res_body(a_ref, w_ref, x_ref, g_ref, o_ref):
    half = a_ref.shape[0] // 2
    for rows in (slice(0, half), slice(half, 2 * half)):
        m = jnp.dot(a_ref[rows, :], w_ref[...], preferred_element_type=F32)
        o_ref[rows, :] = x_ref[rows, :] + _rms(m, g_ref[...])


def _proj_norm_res(a, w, layer, x, gain, *, tm):
    tok, d = x.shape
    k = a.shape[-1]
    return pl.pallas_call(
        _proj_norm_res_body,
        grid=(tok // tm,),
        in_specs=[
            pl.BlockSpec((tm, k), lambda i: (i, 0)),
            pl.BlockSpec((None, k, d), lambda i: (layer, 0, 0), pipeline_mode=pl.Buffered(1)),
            pl.BlockSpec((tm, d), lambda i: (i, 0)),
            pl.BlockSpec((1, d), lambda i: (0, 0)),
        ],
        out_specs=pl.BlockSpec((tm, d), lambda i: (i, 0)),
        out_shape=jax.ShapeDtypeStruct((tok, d), F32),
        compiler_params=_params("parallel"),
        name="out_proj",
    )(a, w, x, gain)


def _mlp_body(x_hbm, gpre_ref, gpost_ref, wu_ref, wd_ref, o_hbm, h_ref, acc_ref, x_ring, x_sem, o_sem, *, n_tiles):
    i, j = pl.program_id(0), pl.program_id(1)
    tm = h_ref.shape[0]
    chunks = [slice(r, r + NORM_ROWS) for r in range(0, tm, NORM_ROWS)]

    def tile_rows(tile):
        return pl.ds(pl.multiple_of(tile * tm, tm), tm)

    def x_copy(tile):
        return pltpu.make_async_copy(x_hbm.at[tile_rows(tile), :], x_ring.at[tile % 2], x_sem.at[tile % 2])

    def o_copy(tile):
        return pltpu.make_async_copy(x_ring.at[tile % 2], o_hbm.at[tile_rows(tile), :], o_sem.at[tile % 2])

    def finish_tile(tile):
        x_tile = x_ring.at[tile % 2]
        for rows in chunks:
            x_tile[rows, :] = x_tile[rows, :] + _rms(acc_ref[rows, :], gpost_ref[...])

    def start_tile(tile):
        x_tile = x_ring.at[tile % 2]
        for rows in chunks:
            h_ref[rows, :] = _rms(x_tile[rows, :], gpre_ref[...]).astype(BF16)

    def hidden_tile():
        hid = jnp.dot(h_ref[...], wu_ref[...], preferred_element_type=F32)
        hid = jnp.square(jnp.maximum(hid, 0.0)).astype(BF16)
        return jnp.dot(hid, wd_ref[...], preferred_element_type=F32)

    @pl.when(jnp.logical_and(j == 0, i == 0))
    def _():
        x_copy(0).start()

    @pl.when(jnp.logical_and(j == 1, i + 1 < n_tiles))
    def _():
        @pl.when(i > 0)
        def _():
            o_copy(i - 1).wait()

        x_copy(i + 1).start()

    @pl.when(jnp.logical_and(j == 0, i == 0))
    def _():
        x_copy(0).wait()
        start_tile(0)
        acc_ref[...] = hidden_tile()

    @pl.when(jnp.logical_and(j == 0, i > 0))
    def _():
        x_copy(i).wait()
        finish_tile(i - 1)
        start_tile(i)
        acc_ref[...] = hidden_tile()
        o_copy(i - 1).start()

    @pl.when(j > 0)
    def _():
        acc_ref[...] += hidden_tile()

    @pl.when(jnp.logical_and(j == pl.num_programs(1) - 1, i == n_tiles - 1))
    def _():
        if n_tiles > 1:
            o_copy(n_tiles - 2).wait()
        finish_tile(n_tiles - 1)
        o_copy(n_tiles - 1).start()
        o_copy(n_tiles - 1).wait()


def _mlp(x, gpre, gpost, w_up_tiles, w_down, *, tm):
    tok, d = x.shape
    n_hidden_tiles, _, th = w_up_tiles.shape
    n_tiles = tok // tm
    assert tm % NORM_ROWS == 0 and n_hidden_tiles >= 2
    return pl.pallas_call(
        functools.partial(_mlp_body, n_tiles=n_tiles),
        grid=(n_tiles, n_hidden_tiles),
        in_specs=[
            pl.BlockSpec(memory_space=pl.ANY),
            pl.BlockSpec((1, d), lambda i, j: (0, 0)),
            pl.BlockSpec((1, d), lambda i, j: (0, 0)),
            pl.BlockSpec((None, d, th), lambda i, j: (j, 0, 0)),
            pl.BlockSpec((th, d), lambda i, j: (j, 0)),
        ],
        out_specs=pl.BlockSpec(memory_space=pl.ANY),
        out_shape=jax.ShapeDtypeStruct((tok, d), F32),
        scratch_shapes=[
            pltpu.VMEM((tm, d), BF16),
            pltpu.VMEM((tm, d), F32),
            pltpu.VMEM((2, tm, d), F32),
            pltpu.SemaphoreType.DMA((2,)),
            pltpu.SemaphoreType.DMA((2,)),
        ],
        compiler_params=_params("arbitrary", "arbitrary"),
        name="mlp",
    )(x, gpre, gpost, w_up_tiles, w_down)


def _tile(n, pref):
    t = min(n, pref)
    assert n % t == 0, (n, pref)
    return t


@jax.jit
def _forward(x, norm_gains, w_in, pool_w, pool_scale, conv_w, rel_bias, w_branch,
             w_gate, b_gate, w_out, w_up, w_down):
    batch, seq, d = x.shape
    depth = w_in.shape[0]
    tok = batch * seq
    branch_dim = d // 2
    attn_dim = N_HEADS * HEAD_DIM
    kv_dim = N_KV_HEADS * HEAD_DIM
    qi_dim = IDX_HEADS * IDX_DIM
    assert branch_dim == attn_dim == len(POOL_WINDOWS) * POOL_GROUP_DIM
    assert seq % ATT_BLOCK == 0 and ATT_BLOCK >= MAX_DISTANCE

    main_w = 4 * branch_dim
    n_in = main_w + attn_dim + 2 * kv_dim + qi_dim
    kw_w = IDX_DIM + IDX_HEADS
    assert w_in.shape[-1] == n_in + kw_w
    w_in_t = jnp.swapaxes(w_in, 1, 2)
    w_kw_t = jnp.pad(w_in_t[:, n_in:, :], ((0, 0), (0, IDX_PAD - kw_w), (0, 0))).astype(BF16)
    later_weights = ((w_branch.reshape(depth, 3 * branch_dim, d), None), (w_out, None),
                     (w_up, MLP_HIDDEN_TILE), (w_down, None))
    pool_w_b = pool_w.astype(BF16)

    gate_col = main_w
    q_col = main_w + 3 * d
    scale = jnp.concatenate([jnp.ones((q_col,), F32), jnp.full((attn_dim,), HEAD_DIM ** -0.5 * LOG2E, F32),
                             jnp.ones((n_in - main_w - attn_dim,), F32)])

    bias_tiles = _bias_tiles(rel_bias, ATT_BLOCK)

    tm = _tile(tok, 1024)
    tm_s = _tile(seq, 512)
    xf = x.reshape(tok, d)
    for l in range(depth):
        g = norm_gains[l]
        g0, g1, g2, g3 = (g[n:n + 1] for n in range(4))
        bias = jnp.concatenate([jnp.zeros((main_w,), F32), b_gate[l].reshape(3 * d), jnp.zeros((n_in - main_w,), F32)])
        proj, kiw = _projections(xf, g0, w_in_t, w_gate, w_kw_t, l, jnp.stack([scale, bias]), main_w, n_in,
                                 tm=_tile(tok, 2048), tn=512)
        y_attn, w_branch_b, w_out_b, w_up_b, w_down_b = _attention(proj, kiw, bias_tiles, batch, seq, q_col,
                                                                    l, later_weights)
        mixed = _mixer(proj, gate_col, y_attn, w_branch_b.reshape(1, 3, branch_dim, d), pool_w_b,
                       pool_scale.reshape(depth, 1, branch_dim), conv_w, l, seq, tm=tm_s, tn=1024)
        xf = _proj_norm_res(mixed, w_out_b[None], 0, xf, g1, tm=tm)
        xf = _mlp(xf, g2, g3, w_up_b, w_down_b, tm=tm)
    return xf.reshape(batch, seq, d)


def kernel(x, norm_gains, w_in, pool_w, pool_scale, conv_w, rel_bias, w_branch, w_gate, b_gate, w_out, w_up, w_down):
    return _forward(x, norm_gains, w_in, pool_w, pool_scale, conv_w, rel_bias, w_branch,
                    w_gate, b_gate, w_out, w_up, w_down)
```

```python
import functools
import math

import jax
import jax.numpy as jnp
import numpy as np
from jax import lax
from jax.experimental import pallas as pl
from jax.experimental.pallas import tpu as pltpu

F32 = jnp.float32
BF16 = jnp.bfloat16
I32 = jnp.int32
I16 = jnp.int16

NORM_EPS = 1e-6
POOL_WINDOWS = (2, 4, 8, 16)
POOL_GROUP_DIM = 256
POOL_HALO = 16
CONV_WIDTH = 3
N_HEADS = 8
N_KV_HEADS = 2
GROUP = N_HEADS // N_KV_HEADS
HEAD_DIM = 128
IDX_HEADS = 16
IDX_DIM = 64
TOPK_MAX = 256
NUM_BUCKETS = 32
MAX_EXACT = NUM_BUCKETS // 2
MAX_DISTANCE = 128
ATT_BLOCK = 256
IDX_PAD = 128
HALF_BITS = 16
INT_MIN = -(2 ** 31)
INT_MAX = 2 ** 31 - 1
NEG = -1e30
LOG2E = math.log2(math.e)
ONES_ROWS = 16
VMEM_LIMIT = 60 * 1024 * 1024
MXU_WIDTH = 256
MLP_HIDDEN_TILE = 1024
BF16_SUBLANES = 16
NORM_ROWS = 256


def _bucket_thresholds():
    n = np.arange(MAX_EXACT, 8 * MAX_DISTANCE, dtype=np.int64)
    nf = n.astype(np.float32)
    large = MAX_EXACT + (np.log(nf / np.float32(MAX_EXACT)) / np.float32(math.log(MAX_DISTANCE / MAX_EXACT))
                         * np.float32(NUM_BUCKETS - MAX_EXACT)).astype(np.int32)
    large = np.minimum(large, NUM_BUCKETS - 1)
    thr = []
    for b in range(MAX_EXACT + 1, NUM_BUCKETS):
        thr.append(int(n[np.argmax(large >= b)]))
    return tuple(thr)


BUCKET_THRESHOLDS = _bucket_thresholds()


def _params(*sem):
    return pltpu.CompilerParams(dimension_semantics=sem, vmem_limit_bytes=VMEM_LIMIT)


def _rms(x, gain):
    ms = jnp.mean(x * x, axis=-1, keepdims=True)
    return x * lax.rsqrt(ms + NORM_EPS) * gain


def _dot_nt(a, b):
    return lax.dot_general(a, b, (((1,), (1,)), ((), ())), preferred_element_type=F32)


def _proj_body(x_hbm, g_ref, wint_ref, wg_ref, wkwt_ref, scale_ref, bias_ref, o_ref, og_ref, okw_ref,
               h_ref, x_buf, x_sem, *, gate_blocks):
    i, j = pl.program_id(0), pl.program_id(1)
    tm = x_buf.shape[0]

    def x_copy(tile):
        rows = pl.ds(pl.multiple_of(tile * tm, tm), tm)
        return pltpu.make_async_copy(x_hbm.at[rows, :], x_buf, x_sem)

    @pl.when(jnp.logical_and(i == 0, j == 0))
    def _():
        x_copy(0).start()

    @pl.when(j == 0)
    def _():
        x_copy(i).wait()

        def norm_rows(r, carry):
            rows = pl.ds(pl.multiple_of(r * NORM_ROWS, NORM_ROWS), NORM_ROWS)
            h = _rms(x_buf[rows, :], g_ref[...]).astype(BF16)
            h_ref[rows, :] = h
            okw_ref[rows, :] = _dot_nt(h, wkwt_ref[...])
            return carry

        lax.fori_loop(0, tm // NORM_ROWS, norm_rows, 0)

    @pl.when(jnp.logical_and(j == 1, i + 1 < pl.num_programs(0)))
    def _():
        x_copy(i + 1).start()

    chunks = [slice(c, c + MXU_WIDTH) for c in range(0, o_ref.shape[-1], MXU_WIDTH)]

    for cs in chunks:
        y = _dot_nt(h_ref[...], wint_ref[cs, :].astype(BF16))
        o_ref[:, cs] = (y * scale_ref[:, cs]).astype(o_ref.dtype)

    @pl.when(j < gate_blocks)
    def _():
        for cs in chunks:
            y = jnp.dot(h_ref[...], wg_ref[:, cs].astype(BF16), preferred_element_type=F32) + bias_ref[:, cs]
            og_ref[:, cs] = (0.5 * jnp.tanh(0.5 * y) + 0.5).astype(og_ref.dtype)


def _projections(x, gain, w_in_t, w_gate, w_kw_t, layer, scale, bias, n_in, *, tm, tn):
    tok, d = x.shape
    n_gate = w_gate.shape[-1]
    assert n_in % tn == 0 and n_gate % tn == 0 and n_gate <= n_in and tm % NORM_ROWS == 0
    gate_blocks = n_gate // tn
    kw = w_kw_t.shape[1]
    gate_block = lambda j: jnp.minimum(j, gate_blocks - 1)
    return pl.pallas_call(
        functools.partial(_proj_body, gate_blocks=gate_blocks),
        grid=(tok // tm, n_in // tn),
        in_specs=[
            pl.BlockSpec(memory_space=pl.ANY),
            pl.BlockSpec((1, d), lambda i, j: (0, 0)),
            pl.BlockSpec((None, tn, d), lambda i, j: (layer, j, 0)),
            pl.BlockSpec((None, d, tn), lambda i, j: (layer, 0, gate_block(j))),
            pl.BlockSpec((None, kw, d), lambda i, j: (layer, 0, 0)),
            pl.BlockSpec((1, tn), lambda i, j: (0, j)),
            pl.BlockSpec((1, tn), lambda i, j: (0, gate_block(j))),
        ],
        out_specs=[
            pl.BlockSpec((tm, tn), lambda i, j: (i, j)),
            pl.BlockSpec((tm, tn), lambda i, j: (i, gate_block(j))),
            pl.BlockSpec((tm, kw), lambda i, j: (i, 0)),
        ],
        out_shape=[
            jax.ShapeDtypeStruct((tok, n_in), BF16),
            jax.ShapeDtypeStruct((tok, n_gate), BF16),
            jax.ShapeDtypeStruct((tok, kw), F32),
        ],
        scratch_shapes=[pltpu.VMEM((tm, d), BF16), pltpu.VMEM((tm, d), F32), pltpu.SemaphoreType.DMA(())],
        compiler_params=_params("arbitrary", "arbitrary"),
        name="projections",
    )(x, gain, w_in_t, w_gate, w_kw_t, scale, bias)


def _bias_tiles_body(rel_ref, o_ref, *, t):
    row = lax.broadcasted_iota(I32, (t, t), 0)
    col = lax.broadcasted_iota(I32, (t, t), 1)
    for off in range(2):
        n = jnp.maximum(col - row + off * t, 0)
        large = jnp.full((t, t), MAX_EXACT, I32)
        for thr in BUCKET_THRESHOLDS:
            large = large + jnp.where(n >= thr, 1, 0)
        bucket = jnp.where(n < MAX_EXACT, n, large)
        for h in range(N_HEADS):
            val = jnp.zeros((t, t), F32)
            for b in range(NUM_BUCKETS):
                val = jnp.where(bucket == b, rel_ref[b, h], val)
            o_ref[h, off] = (val - rel_ref[NUM_BUCKETS - 1, h]) * LOG2E


def _bias_tiles(rel_bias, t):
    return pl.pallas_call(
        functools.partial(_bias_tiles_body, t=t),
        in_specs=[pl.BlockSpec(memory_space=pltpu.SMEM)],
        out_specs=pl.BlockSpec(memory_space=pltpu.VMEM),
        out_shape=jax.ShapeDtypeStruct((N_HEADS, 2, t, t), F32),
        compiler_params=pltpu.CompilerParams(vmem_limit_bytes=VMEM_LIMIT),
        name="bias_tiles",
    )(rel_bias)


def _attn_body(*refs, t, topk, n_casts):
    (q_ref, k_ref, v_ref, qia_ref, qib_ref, kiw_ref, bias_ref), refs = refs[:7], refs[7:]
    cast_src, refs = refs[:n_casts], refs[n_casts:]
    o_ref, cast_dst, refs = refs[0], refs[1:1 + n_casts], refs[1 + n_casts:]
    (sc_ref, sch_ref, thr_ref, vt_ref, kib_ref, qih_ref, mask_ref, alpha_ref, m_ref, acc_ref), refs = (
        refs[:10], refs[10:])
    s_ref, p_ref = refs[:N_HEADS], refs[N_HEADS:]

    for src, dst in zip(cast_src, cast_dst):
        if len(dst.shape) == 2:
            dst[...] = src[...].astype(BF16)
        else:
            width = dst.shape[-1]
            for c in range(dst.shape[0]):
                dst[c] = src[:, c * width:(c + 1) * width].astype(BF16)

    i = pl.program_id(1)
    nkb = vt_ref.shape[0]
    q0 = pl.multiple_of(i * t, t)
    key_pos = lax.broadcasted_iota(I32, (t, 1), 0)
    qry_pos = lax.broadcasted_iota(I32, (1, t), 1)
    causal = key_pos <= qry_pos

    @pl.when(i == 0)
    def _():
        for kb in range(nkb):
            rows = slice(kb * t, (kb + 1) * t)
            vb = v_ref[rows, :].astype(F32)
            for g in range(N_KV_HEADS):
                vt_ref[kb, g, 0:HEAD_DIM, :] = vb[:, g * HEAD_DIM:(g + 1) * HEAD_DIM].T.astype(BF16)
                vt_ref[kb, g, HEAD_DIM:, :] = jnp.ones((vt_ref.shape[2] - HEAD_DIM, t), BF16)
            kib_ref[rows, :] = kiw_ref[rows, 0:IDX_DIM].astype(BF16)

    half = IDX_HEADS // 2
    for h in range(IDX_HEADS):
        src = qia_ref if h < half else qib_ref
        qih_ref[h] = src[:, (h % half) * IDX_DIM:(h % half + 1) * IDX_DIM]
    w_t = kiw_ref[pl.ds(q0, t), :].T[IDX_DIM:IDX_DIM + IDX_HEADS, :] * (IDX_HEADS ** -0.5)

    def score_block(kb, diag):
        kblk = kib_ref[pl.ds(pl.multiple_of(kb * t, t), t), :]
        acc = jnp.zeros((t, t), F32)
        for h in range(IDX_HEADS):
            acc = acc + w_t[h:h + 1, :] * jnp.maximum(_dot_nt(kblk, qih_ref[h]), 0.0)
        if diag:
            acc = jnp.where(causal, acc, -jnp.inf)
        bits = pltpu.bitcast(acc, I32)
        bits = jnp.where(bits == INT_MIN, 0, bits)
        key = bits ^ ((bits >> 31) & INT_MAX)
        sc_ref[kb] = key
        sch_ref[kb] = (key >> HALF_BITS).astype(I16)

    def score_loop(kb, carry):
        score_block(kb, False)
        return carry

    lax.fori_loop(0, i, score_loop, 0)
    score_block(i, True)

    @pl.when(i % 2 == 0)
    def _():
        sc_ref[i + 1] = jnp.full((t, t), INT_MIN, I32)
        sch_ref[i + 1] = jnp.full((t, t), INT_MIN >> HALF_BITS, I16)

    @pl.when(q0 + t > topk)
    def _():
        needs = q0 + qry_pos >= topk
        rows = t // BF16_SUBLANES
        half_lo, half_hi = INT_MIN >> HALF_BITS, -(INT_MIN >> HALF_BITS)

        def halving(_, state):
            lo, hi, cnt_lo, cnt_hi = state
            mid = (lo & hi) + ((lo ^ hi) >> 1)
            mid_c = mid.astype(I16)

            def count(pair, c):
                for kb in (2 * pair, 2 * pair + 1):
                    ge = jnp.where(sch_ref[kb] >= mid_c, jnp.int16(1), jnp.int16(0))
                    for r in range(rows):
                        c = c + ge[r * BF16_SUBLANES:(r + 1) * BF16_SUBLANES]
                return c

            c = lax.fori_loop(0, (i + 2) // 2, count, jnp.zeros((BF16_SUBLANES, t), I16))
            c = jnp.sum(c.astype(I32), axis=0, keepdims=True)
            fresh = mid > lo
            up = jnp.logical_and(fresh, c >= topk)
            down = jnp.logical_and(fresh, c < topk)
            return (jnp.where(up, mid, lo), jnp.where(down, mid, hi),
                    jnp.where(up, c, cnt_lo), jnp.where(down, c, cnt_hi))

        band, _, cnt_lo, cnt_hi = lax.fori_loop(
            0, HALF_BITS, halving,
            (jnp.full((1, t), half_lo, I32), jnp.full((1, t), half_hi, I32),
             jnp.full((1, t), topk + 1, I32), jnp.zeros((1, t), I32)))

        def low_halves(kb, carry):
            key = sc_ref[kb]
            high = key >> HALF_BITS
            low = (key & (2 * half_hi - 1)) + half_lo
            sch_ref[kb] = jnp.where(high == band, low, jnp.where(high > band, half_hi - 1, half_lo)).astype(I16)
            return carry

        lax.fori_loop(0, i + 1, low_halves, 0)
        low, _, cnt_lo, cnt_hi = lax.fori_loop(
            0, HALF_BITS, halving,
            (jnp.full((1, t), half_lo, I32), jnp.full((1, t), half_hi, I32), cnt_lo, cnt_hi))
        lo = (band << HALF_BITS) + (low - half_lo)
        thr_ref[...] = jnp.where(needs, lo, INT_MIN)

        tied = jnp.where(jnp.logical_and(needs, cnt_lo > topk), 1, 0)

        @pl.when(jnp.max(tied) > 0)
        def _():
            wanted = (topk - cnt_hi).astype(F32)
            tied_f = tied.astype(F32)
            earlier = jnp.where(lax.broadcasted_iota(I32, (t, t), 1) < lax.broadcasted_iota(I32, (t, t), 0),
                                1.0, 0.0).astype(BF16)

            def demote(kb, seen):
                key = sc_ref[kb]
                is_tie = jnp.where(key == lo, tied_f, 0.0)
                rank = seen + jnp.dot(earlier, is_tie.astype(BF16), preferred_element_type=F32)
                drop = is_tie * jnp.where(rank >= wanted, 1.0, 0.0)
                sc_ref[kb] = jnp.where(drop > 0.0, lo - 1, key)
                return seen + jnp.sum(is_tie, axis=0, keepdims=True)

            lax.fori_loop(0, i + 1, demote, jnp.zeros((1, t), F32))

    @pl.when(q0 + t <= topk)
    def _():
        thr_ref[...] = jnp.full((1, t), INT_MIN, I32)

    m_ref[...] = jnp.full(m_ref.shape, NEG, F32)
    acc_ref[...] = jnp.zeros(acc_ref.shape, F32)
    thr = thr_ref[...]

    def attn_block(kb, mode):
        ks = pl.ds(pl.multiple_of(kb * t, t), t)
        mask = jnp.where(sc_ref[kb] >= thr, 0.0, NEG)
        if mode == 0:
            mask = jnp.where(causal, mask, NEG)
        mask_ref[...] = mask

        def logits(h):
            g = h // GROUP
            s_ref[h][...] = _dot_nt(k_ref[ks, g * HEAD_DIM:(g + 1) * HEAD_DIM],
                                    q_ref[:, h * HEAD_DIM:(h + 1) * HEAD_DIM])

        def numerators(h):
            s = s_ref[h][...] + mask_ref[...]
            if mode != "far":
                s = s + bias_ref[h, mode]
            m_old = m_ref[h]
            m_new = jnp.maximum(m_old, jnp.max(s, axis=0, keepdims=True))
            p_ref[h][...] = jnp.exp2(s - m_new).astype(BF16)
            alpha_ref[h] = jnp.exp2(m_old - m_new)
            m_ref[h] = m_new

        def weighted_values(h):
            pv = jnp.dot(vt_ref[kb, h // GROUP], p_ref[h][...], preferred_element_type=F32)
            acc_ref[h] = alpha_ref[h] * acc_ref[h] + pv

        for stage in (logits, numerators, weighted_values):
            for h in range(N_HEADS):
                stage(h)

    def far_loop(kb, carry):
        attn_block(kb, "far")
        return carry

    lax.fori_loop(0, jnp.maximum(i - 1, 0), far_loop, 0)

    @pl.when(i >= 1)
    def _():
        attn_block(i - 1, 1)

    attn_block(i, 0)

    for h in range(N_HEADS):
        acc = acc_ref[h]
        out_t = acc[0:HEAD_DIM, :] / acc[HEAD_DIM:HEAD_DIM + 1, :]
        o_ref[:, h * HEAD_DIM:(h + 1) * HEAD_DIM] = out_t.T.astype(o_ref.dtype)


def _attention(proj, kiw, bias_tiles, batch, seq, q_col, layer, casts):
    t = ATT_BLOCK
    nq = seq // t
    steps = batch * nq
    cast_in, cast_out, cast_shapes = [], [], []
    for w, col_tile in casts:
        rows = w.shape[1] // steps
        assert rows * steps == w.shape[1] and rows % BF16_SUBLANES == 0
        cast_in.append(pl.BlockSpec((None, rows, w.shape[2]), lambda b, i: (layer, b * nq + i, 0)))
        if col_tile is None:
            cast_out.append(pl.BlockSpec((rows, w.shape[2]), lambda b, i: (b * nq + i, 0)))
            cast_shapes.append(jax.ShapeDtypeStruct(w.shape[1:], BF16))
        else:
            n_tiles = w.shape[2] // col_tile
            cast_out.append(pl.BlockSpec((n_tiles, rows, col_tile), lambda b, i: (0, b * nq + i, 0)))
            cast_shapes.append(jax.ShapeDtypeStruct((n_tiles, w.shape[1], col_tile), BF16))
    topk = min(TOPK_MAX, seq // 4)
    attn_dim = N_HEADS * HEAD_DIM
    kv_dim = N_KV_HEADS * HEAD_DIM
    qi_half = IDX_HEADS * IDX_DIM // 2
    k_col = q_col + attn_dim
    qi_col = k_col + 2 * kv_dim
    assert q_col % attn_dim == 0 and k_col % kv_dim == 0 and qi_col % qi_half == 0
    return pl.pallas_call(
        functools.partial(_attn_body, t=t, topk=topk, n_casts=len(casts)),
        grid=(batch, nq),
        in_specs=[
            pl.BlockSpec((t, attn_dim), lambda b, i: (b * nq + i, q_col // attn_dim)),
            pl.BlockSpec((seq, kv_dim), lambda b, i: (b, k_col // kv_dim)),
            pl.BlockSpec((seq, kv_dim), lambda b, i: (b, k_col // kv_dim + 1)),
            pl.BlockSpec((t, qi_half), lambda b, i: (b * nq + i, qi_col // qi_half)),
            pl.BlockSpec((t, qi_half), lambda b, i: (b * nq + i, qi_col // qi_half + 1)),
            pl.BlockSpec((seq, IDX_PAD), lambda b, i: (b, 0)),
            pl.BlockSpec((N_HEADS, 2, t, t), lambda b, i: (0, 0, 0, 0)),
        ] + cast_in,
        out_specs=[pl.BlockSpec((t, attn_dim), lambda b, i: (b * nq + i, 0))] + cast_out,
        out_shape=[jax.ShapeDtypeStruct((batch * seq, attn_dim), BF16)] + cast_shapes,
        scratch_shapes=[
            pltpu.VMEM((nq + 1, t, t), I32),
            pltpu.VMEM((nq + 1, t, t), I16),
            pltpu.VMEM((1, t), I32),
            pltpu.VMEM((nq, N_KV_HEADS, HEAD_DIM + ONES_ROWS, t), BF16),
            pltpu.VMEM((seq, IDX_DIM), BF16),
            pltpu.VMEM((IDX_HEADS, t, IDX_DIM), BF16),
            pltpu.VMEM((t, t), F32),
            pltpu.VMEM((N_HEADS, 1, t), F32),
            pltpu.VMEM((N_HEADS, 1, t), F32),
            pltpu.VMEM((N_HEADS, HEAD_DIM + ONES_ROWS, t), F32),
        ] + [pltpu.VMEM((t, t), F32)] * N_HEADS + [pltpu.VMEM((t, t), BF16)] * N_HEADS,
        compiler_params=_params("parallel", "arbitrary"),
        name="sparse_attention",
    )(proj, proj, proj, proj, proj, kiw, bias_tiles, *[w for w, _ in casts])


def _mixer_body(main_ref, halo_ref, ya_ref, g0_ref, g1_ref, g2_ref, wb_ref, pw_ref, ps_ref, cw_ref,
                o_ref, y_ref, acc_ref, *, tm, tn, seq, pool_dim, conv_dim):
    pos0 = (pl.program_id(0) * tm) % seq
    pos = lax.broadcasted_iota(I32, (tm, 1), 0) + pos0
    keep = jnp.where(pos0 == 0, 0.0, 1.0)
    n_out = o_ref.shape[-1]

    def branch(n, y, c0):
        return g_refs[n][:, c0:c0 + tn].astype(F32) * jnp.dot(y, wb_ref[n, :, c0:c0 + tn], preferred_element_type=F32)

    g_refs = (g0_ref, g1_ref, g2_ref)
    for c0 in range(0, n_out, tn):
        acc_ref[:, c0:c0 + tn] = branch(2, ya_ref[...], c0)

    def ext(c0, c1):
        halo = halo_ref[:, c0:c1].astype(F32) * keep
        return jnp.concatenate([halo, main_ref[:, c0:c1].astype(F32)], axis=0)

    for g, win in enumerate(POOL_WINDOWS):
        c0 = g * POOL_GROUP_DIM
        u = ext(c0, c0 + POOL_GROUP_DIM)
        s, sh = u, 1
        while sh < win:
            s = s + pltpu.roll(s, sh, axis=0)
            sh *= 2
        cnt = jnp.minimum(pos + 1, win).astype(F32)
        dlt = s[POOL_HALO:] / cnt - u[POOL_HALO:]
        yp = jnp.dot(dlt.astype(BF16), pw_ref[g], preferred_element_type=F32)
        y_ref[0, :, c0:c0 + POOL_GROUP_DIM] = (yp * ps_ref[:, c0:c0 + POOL_GROUP_DIM]).astype(BF16)

    z = ext(pool_dim, pool_dim + conv_dim) * ext(pool_dim + conv_dim, pool_dim + 2 * conv_dim)
    yc = z[POOL_HALO:] * cw_ref[CONV_WIDTH - 1:CONV_WIDTH, :]
    for tap in range(1, CONV_WIDTH):
        yc = yc + pltpu.roll(z, tap, axis=0)[POOL_HALO:] * cw_ref[CONV_WIDTH - 1 - tap:CONV_WIDTH - tap, :]
    gate_b = main_ref[:, pool_dim + 2 * conv_dim:pool_dim + 3 * conv_dim].astype(F32)
    y_ref[1] = (gate_b * yc).astype(BF16)

    for c0 in range(0, n_out, tn):
        mixed = acc_ref[:, c0:c0 + tn] + branch(0, y_ref[0], c0) + branch(1, y_ref[1], c0)
        o_ref[:, c0:c0 + tn] = mixed.astype(o_ref.dtype)


def _mixer(main, gates, y_attn, w_branch, pool_w, pool_scale, conv_w, layer, seq, *, tm, tn):
    tok = main.shape[0]
    branch_dim = y_attn.shape[-1]
    width = 4 * branch_dim
    d = w_branch.shape[-1]
    assert d % tn == 0
    gj = 0
    halo_blocks = tm // POOL_HALO
    resident = dict(pipeline_mode=pl.Buffered(1))
    return pl.pallas_call(
        functools.partial(_mixer_body, tm=tm, tn=tn, seq=seq, pool_dim=branch_dim, conv_dim=branch_dim),
        grid=(tok // tm,),
        in_specs=[
            pl.BlockSpec((tm, width), lambda i: (i, 0)),
            pl.BlockSpec((POOL_HALO, width), lambda i: (jnp.maximum(i * halo_blocks - 1, 0), 0)),
            pl.BlockSpec((tm, branch_dim), lambda i: (i, 0)),
            pl.BlockSpec((tm, d), lambda i: (i, gj)),
            pl.BlockSpec((tm, d), lambda i: (i, gj + 1)),
            pl.BlockSpec((tm, d), lambda i: (i, gj + 2)),
            pl.BlockSpec((None, 3, branch_dim, d), lambda i: (0, 0, 0, 0), **resident),
            pl.BlockSpec((None,) + pool_w.shape[1:], lambda i: (layer, 0, 0, 0), **resident),
            pl.BlockSpec((None, 1, branch_dim), lambda i: (layer, 0, 0)),
            pl.BlockSpec((None, CONV_WIDTH, branch_dim), lambda i: (layer, 0, 0)),
        ],
        out_specs=pl.BlockSpec((tm, d), lambda i: (i, 0)),
        out_shape=jax.ShapeDtypeStruct((tok, d), BF16),
        scratch_shapes=[pltpu.VMEM((2, tm, branch_dim), BF16), pltpu.VMEM((tm, d), F32)],
        compiler_params=_params("parallel"),
        name="mixer_branches",
    )(main, main, y_attn, gates, gates, gates, w_branch, pool_w, pool_scale, conv_w)


def _proj_norm_res_body(a_ref, w_ref, x_ref, g_ref, o_ref):
    half = a_ref.shape[0] // 2
    for rows in (slice(0, half), slice(half, 2 * half)):
        m = jnp.dot(a_ref[rows, :], w_ref[...], preferred_element_type=F32)
        o_ref[rows, :] = x_ref[rows, :] + _rms(m, g_ref[...])


def _proj_norm_res(a, w, layer, x, gain, *, tm):
    tok, d = x.shape
    k = a.shape[-1]
    return pl.pallas_call(
        _proj_norm_res_body,
        grid=(tok // tm,),
        in_specs=[
            pl.BlockSpec((tm, k), lambda i: (i, 0)),
            pl.BlockSpec((None, k, d), lambda i: (layer, 0, 0), pipeline_mode=pl.Buffered(1)),
            pl.BlockSpec((tm, d), lambda i: (i, 0)),
            pl.BlockSpec((1, d), lambda i: (0, 0)),
        ],
        out_specs=pl.BlockSpec((tm, d), lambda i: (i, 0)),
        out_shape=jax.ShapeDtypeStruct((tok, d), F32),
        compiler_params=_params("parallel"),
        name="out_proj",
    )(a, w, x, gain)


def _mlp_body(x_hbm, gpre_ref, gpost_ref, wu_ref, wd_ref, o_hbm, h_ref, acc_ref, x_ring, x_sem, o_sem, *, n_tiles):
    i, j = pl.program_id(0), pl.program_id(1)
    tm = h_ref.shape[0]
    chunks = [slice(r, r + NORM_ROWS) for r in range(0, tm, NORM_ROWS)]

    def tile_rows(tile):
        return pl.ds(pl.multiple_of(tile * tm, tm), tm)

    def x_copy(tile):
        return pltpu.make_async_copy(x_hbm.at[tile_rows(tile), :], x_ring.at[tile % 2], x_sem.at[tile % 2])

    def o_copy(tile):
        return pltpu.make_async_copy(x_ring.at[tile % 2], o_hbm.at[tile_rows(tile), :], o_sem.at[tile % 2])

    def finish_tile(tile):
        x_tile = x_ring.at[tile % 2]
        for rows in chunks:
            x_tile[rows, :] = x_tile[rows, :] + _rms(acc_ref[rows, :], gpost_ref[...])

    def start_tile(tile):
        x_tile = x_ring.at[tile % 2]
        for rows in chunks:
            h_ref[rows, :] = _rms(x_tile[rows, :], gpre_ref[...]).astype(BF16)

    def hidden_tile():
        hid = jnp.dot(h_ref[...], wu_ref[...], preferred_element_type=F32)
        hid = jnp.square(jnp.maximum(hid, 0.0)).astype(BF16)
        return jnp.dot(hid, wd_ref[...], preferred_element_type=F32)

    @pl.when(jnp.logical_and(j == 0, i == 0))
    def _():
        x_copy(0).start()

    @pl.when(jnp.logical_and(j == 1, i + 1 < n_tiles))
    def _():
        @pl.when(i > 0)
        def _():
            o_copy(i - 1).wait()

        x_copy(i + 1).start()

    @pl.when(jnp.logical_and(j == 0, i == 0))
    def _():
        x_copy(0).wait()
        start_tile(0)
        acc_ref[...] = hidden_tile()

    @pl.when(jnp.logical_and(j == 0, i > 0))
    def _():
        x_copy(i).wait()
        finish_tile(i - 1)
        start_tile(i)
        acc_ref[...] = hidden_tile()
        o_copy(i - 1).start()

    @pl.when(j > 0)
    def _():
        acc_ref[...] += hidden_tile()

    @pl.when(jnp.logical_and(j == pl.num_programs(1) - 1, i == n_tiles - 1))
    def _():
        if n_tiles > 1:
            o_copy(n_tiles - 2).wait()
        finish_tile(n_tiles - 1)
        o_copy(n_tiles - 1).start()
        o_copy(n_tiles - 1).wait()


def _mlp(x, gpre, gpost, w_up_tiles, w_down, *, tm):
    tok, d = x.shape
    n_hidden_tiles, _, th = w_up_tiles.shape
    n_tiles = tok // tm
    assert tm % NORM_ROWS == 0 and n_hidden_tiles >= 2
    return pl.pallas_call(
        functools.partial(_mlp_body, n_tiles=n_tiles),
        grid=(n_tiles, n_hidden_tiles),
        in_specs=[
            pl.BlockSpec(memory_space=pl.ANY),
            pl.BlockSpec((1, d), lambda i, j: (0, 0)),
            pl.BlockSpec((1, d), lambda i, j: (0, 0)),
            pl.BlockSpec((None, d, th), lambda i, j: (j, 0, 0)),
            pl.BlockSpec((th, d), lambda i, j: (j, 0)),
        ],
        out_specs=pl.BlockSpec(memory_space=pl.ANY),
        out_shape=jax.ShapeDtypeStruct((tok, d), F32),
        scratch_shapes=[
            pltpu.VMEM((tm, d), BF16),
            pltpu.VMEM((tm, d), F32),
            pltpu.VMEM((2, tm, d), F32),
            pltpu.SemaphoreType.DMA((2,)),
            pltpu.SemaphoreType.DMA((2,)),
        ],
        compiler_params=_params("arbitrary", "arbitrary"),
        name="mlp",
    )(x, gpre, gpost, w_up_tiles, w_down)


def _tile(n, pref):
    t = min(n, pref)
    assert n % t == 0, (n, pref)
    return t


@jax.jit
def _forward(x, norm_gains, w_in, pool_w, pool_scale, conv_w, rel_bias, w_branch,
             w_gate, b_gate, w_out, w_up, w_down):
    batch, seq, d = x.shape
    depth = w_in.shape[0]
    tok = batch * seq
    branch_dim = d // 2
    attn_dim = N_HEADS * HEAD_DIM
    kv_dim = N_KV_HEADS * HEAD_DIM
    qi_dim = IDX_HEADS * IDX_DIM
    assert branch_dim == attn_dim == len(POOL_WINDOWS) * POOL_GROUP_DIM
    assert seq % ATT_BLOCK == 0 and ATT_BLOCK >= MAX_DISTANCE

    main_w = 4 * branch_dim
    n_in = main_w + attn_dim + 2 * kv_dim + qi_dim
    kw_w = IDX_DIM + IDX_HEADS
    assert w_in.shape[-1] == n_in + kw_w
    w_in_t = jnp.swapaxes(w_in, 1, 2)
    w_kw_t = jnp.pad(w_in_t[:, n_in:, :], ((0, 0), (0, IDX_PAD - kw_w), (0, 0))).astype(BF16)
    later_weights = ((w_branch.reshape(depth, 3 * branch_dim, d), None), (w_out, None),
                     (w_up, MLP_HIDDEN_TILE), (w_down, None))
    pool_w_b = pool_w.astype(BF16)

    q_col = main_w
    scale = jnp.concatenate([jnp.ones((q_col,), F32), jnp.full((attn_dim,), HEAD_DIM ** -0.5 * LOG2E, F32),
                             jnp.ones((n_in - main_w - attn_dim,), F32)]).reshape(1, n_in)

    bias_tiles = _bias_tiles(rel_bias, ATT_BLOCK)

    tm = _tile(tok, 1024)
    tm_s = _tile(seq, 512)
    xf = x.reshape(tok, d)
    for l in range(depth):
        g = norm_gains[l]
        g0, g1, g2, g3 = (g[n:n + 1] for n in range(4))
        proj, gates, kiw = _projections(xf, g0, w_in_t, w_gate, w_kw_t, l, scale, b_gate[l].reshape(1, 3 * d), n_in,
                                        tm=_tile(tok, 2048), tn=512)
        y_attn, w_branch_b, w_out_b, w_up_b, w_down_b = _attention(proj, kiw, bias_tiles, batch, seq, q_col,
                                                                    l, later_weights)
        mixed = _mixer(proj, gates, y_attn, w_branch_b.reshape(1, 3, branch_dim, d), pool_w_b,
                       pool_scale.reshape(depth, 1, branch_dim), conv_w, l, seq, tm=tm_s, tn=1024)
        xf = _proj_norm_res(mixed, w_out_b[None], 0, xf, g1, tm=tm)
        xf = _mlp(xf, g2, g3, w_up_b, w_down_b, tm=tm)
    return xf.reshape(batch, seq, d)


def kernel(x, norm_gains, w_in, pool_w, pool_scale, conv_w, rel_bias, w_branch, w_gate, b_gate, w_out, w_up, w_down):
    return _forward(x, norm_gains, w_in, pool_w, pool_scale, conv_w, rel_bias, w_branch,
                    w_gate, b_gate, w_out, w_up, w_down)
```

```python
import functools
import math

import jax
import jax.numpy as jnp
import numpy as np
from jax import lax
from jax.experimental import pallas as pl
from jax.experimental.pallas import tpu as pltpu

F32 = jnp.float32
BF16 = jnp.bfloat16
I32 = jnp.int32
I16 = jnp.int16

NORM_EPS = 1e-6
POOL_WINDOWS = (2, 4, 8, 16)
POOL_GROUP_DIM = 256
POOL_HALO = 16
CONV_WIDTH = 3
N_HEADS = 8
N_KV_HEADS = 2
GROUP = N_HEADS // N_KV_HEADS
HEAD_DIM = 128
IDX_HEADS = 16
IDX_DIM = 64
TOPK_MAX = 256
NUM_BUCKETS = 32
MAX_EXACT = NUM_BUCKETS // 2
MAX_DISTANCE = 128
ATT_BLOCK = 256
IDX_PAD = 128
HALF_BITS = 16
INT_MIN = -(2 ** 31)
INT_MAX = 2 ** 31 - 1
NEG = -1e30
LOG2E = math.log2(math.e)
ONES_ROWS = 16
VMEM_LIMIT = 60 * 1024 * 1024
MXU_WIDTH = 256
MLP_HIDDEN_TILE = 1024
BF16_SUBLANES = 16
NORM_ROWS = 256


def _bucket_thresholds():
    n = np.arange(MAX_EXACT, 8 * MAX_DISTANCE, dtype=np.int64)
    nf = n.astype(np.float32)
    large = MAX_EXACT + (np.log(nf / np.float32(MAX_EXACT)) / np.float32(math.log(MAX_DISTANCE / MAX_EXACT))
                         * np.float32(NUM_BUCKETS - MAX_EXACT)).astype(np.int32)
    large = np.minimum(large, NUM_BUCKETS - 1)
    thr = []
    for b in range(MAX_EXACT + 1, NUM_BUCKETS):
        thr.append(int(n[np.argmax(large >= b)]))
    return tuple(thr)


BUCKET_THRESHOLDS = _bucket_thresholds()


def _params(*sem):
    return pltpu.CompilerParams(dimension_semantics=sem, vmem_limit_bytes=VMEM_LIMIT)


def _rms(x, gain):
    ms = jnp.mean(x * x, axis=-1, keepdims=True)
    return x * lax.rsqrt(ms + NORM_EPS) * gain


def _dot_nt(a, b):
    return lax.dot_general(a, b, (((1,), (1,)), ((), ())), preferred_element_type=F32)


def _proj_body(x_hbm, g_ref, wint_ref, wg_ref, wkwt_ref, scale_ref, bias_ref, o_ref, og_ref, okw_ref,
               h_ref, x_buf, x_sem, *, gate_blocks):
    i, j = pl.program_id(0), pl.program_id(1)
    tm = x_buf.shape[0]

    def x_copy(tile):
        rows = pl.ds(pl.multiple_of(tile * tm, tm), tm)
        return pltpu.make_async_copy(x_hbm.at[rows, :], x_buf, x_sem)

    @pl.when(jnp.logical_and(i == 0, j == 0))
    def _():
        x_copy(0).start()

    @pl.when(j == 0)
    def _():
        x_copy(i).wait()

        def norm_rows(r, carry):
            rows = pl.ds(pl.multiple_of(r * NORM_ROWS, NORM_ROWS), NORM_ROWS)
            h = _rms(x_buf[rows, :], g_ref[...]).astype(BF16)
            h_ref[rows, :] = h
            okw_ref[rows, :] = _dot_nt(h, wkwt_ref[...])
            return carry

        lax.fori_loop(0, tm // NORM_ROWS, norm_rows, 0)

    @pl.when(jnp.logical_and(j == 1, i + 1 < pl.num_programs(0)))
    def _():
        x_copy(i + 1).start()

    chunks = [slice(c, c + MXU_WIDTH) for c in range(0, o_ref.shape[-1], MXU_WIDTH)]

    for cs in chunks:
        y = _dot_nt(h_ref[...], wint_ref[cs, :].astype(BF16))
        o_ref[:, cs] = (y * scale_ref[:, cs]).astype(o_ref.dtype)

    @pl.when(j < gate_blocks)
    def _():
        for cs in chunks:
            y = jnp.dot(h_ref[...], wg_ref[:, cs].astype(BF16), preferred_element_type=F32) + bias_ref[:, cs]
            og_ref[:, cs] = (0.5 * jnp.tanh(0.5 * y) + 0.5).astype(og_ref.dtype)


def _projections(x, gain, w_in_t, w_gate, w_kw_t, layer, scale, bias, n_in, *, tm, tn):
    tok, d = x.shape
    n_gate = w_gate.shape[-1]
    assert n_in % tn == 0 and n_gate % tn == 0 and n_gate <= n_in and tm % NORM_ROWS == 0
    gate_blocks = n_gate // tn
    kw = w_kw_t.shape[1]
    gate_block = lambda j: jnp.minimum(j, gate_blocks - 1)
    return pl.pallas_call(
        functools.partial(_proj_body, gate_blocks=gate_blocks),
        grid=(tok // tm, n_in // tn),
        in_specs=[
            pl.BlockSpec(memory_space=pl.ANY),
            pl.BlockSpec((1, d), lambda i, j: (0, 0)),
            pl.BlockSpec((None, tn, d), lambda i, j: (layer, j, 0)),
            pl.BlockSpec((None, d, tn), lambda i, j: (layer, 0, gate_block(j))),
            pl.BlockSpec((None, kw, d), lambda i, j: (layer, 0, 0)),
            pl.BlockSpec((1, tn), lambda i, j: (0, j)),
            pl.BlockSpec((1, tn), lambda i, j: (0, gate_block(j))),
        ],
        out_specs=[
            pl.BlockSpec((tm, tn), lambda i, j: (i, j)),
            pl.BlockSpec((tm, tn), lambda i, j: (i, gate_block(j))),
            pl.BlockSpec((tm, kw), lambda i, j: (i, 0)),
        ],
        out_shape=[
            jax.ShapeDtypeStruct((tok, n_in), BF16),
            jax.ShapeDtypeStruct((tok, n_gate), BF16),
            jax.ShapeDtypeStruct((tok, kw), F32),
        ],
        scratch_shapes=[pltpu.VMEM((tm, d), BF16), pltpu.VMEM((tm, d), F32), pltpu.SemaphoreType.DMA(())],
        compiler_params=_params("arbitrary", "arbitrary"),
        name="projections",
    )(x, gain, w_in_t, w_gate, w_kw_t, scale, bias)


def _bias_tiles_body(rel_ref, o_ref, *, t):
    row = lax.broadcasted_iota(I32, (t, t), 0)
    col = lax.broadcasted_iota(I32, (t, t), 1)
    for off in range(2):
        n = jnp.maximum(col - row + off * t, 0)
        large = jnp.full((t, t), MAX_EXACT, I32)
        for thr in BUCKET_THRESHOLDS:
            large = large + jnp.where(n >= thr, 1, 0)
        bucket = jnp.where(n < MAX_EXACT, n, large)
        for h in range(N_HEADS):
            val = jnp.zeros((t, t), F32)
            for b in range(NUM_BUCKETS):
                val = jnp.where(bucket == b, rel_ref[b, h], val)
            o_ref[h, off] = (val - rel_ref[NUM_BUCKETS - 1, h]) * LOG2E


def _bias_tiles(rel_bias, t):
    return pl.pallas_call(
        functools.partial(_bias_tiles_body, t=t),
        in_specs=[pl.BlockSpec(memory_space=pltpu.SMEM)],
        out_specs=pl.BlockSpec(memory_space=pltpu.VMEM),
        out_shape=jax.ShapeDtypeStruct((N_HEADS, 2, t, t), F32),
        compiler_params=pltpu.CompilerParams(vmem_limit_bytes=VMEM_LIMIT),
        name="bias_tiles",
    )(rel_bias)


def _attn_body(*refs, t, topk, n_casts):
    (q_ref, k_ref, v_ref, qia_ref, qib_ref, kiw_ref, bias_ref), refs = refs[:7], refs[7:]
    cast_src, refs = refs[:n_casts], refs[n_casts:]
    o_ref, cast_dst, refs = refs[0], refs[1:1 + n_casts], refs[1 + n_casts:]
    (sc_ref, sch_ref, thr_ref, vt_ref, kib_ref, qih_ref, mask_ref, alpha_ref, m_ref, acc_ref), refs = (
        refs[:10], refs[10:])
    s_ref, p_ref = refs[:N_HEADS], refs[N_HEADS:]

    for src, dst in zip(cast_src, cast_dst):
        if len(dst.shape) == 2:
            dst[...] = src[...].astype(BF16)
        else:
            width = dst.shape[-1]
            for c in range(dst.shape[0]):
                dst[c] = src[:, c * width:(c + 1) * width].astype(BF16)

    i = pl.program_id(1)
    nkb = vt_ref.shape[0]
    q0 = pl.multiple_of(i * t, t)
    key_pos = lax.broadcasted_iota(I32, (t, 1), 0)
    qry_pos = lax.broadcasted_iota(I32, (1, t), 1)
    causal = key_pos <= qry_pos

    @pl.when(i == 0)
    def _():
        for kb in range(nkb):
            rows = slice(kb * t, (kb + 1) * t)
            vb = v_ref[rows, :].astype(F32)
            for g in range(N_KV_HEADS):
                vt_ref[kb, g, 0:HEAD_DIM, :] = vb[:, g * HEAD_DIM:(g + 1) * HEAD_DIM].T.astype(BF16)
                vt_ref[kb, g, HEAD_DIM:, :] = jnp.ones((vt_ref.shape[2] - HEAD_DIM, t), BF16)
            kib_ref[rows, :] = kiw_ref[rows, 0:IDX_DIM].astype(BF16)

    half = IDX_HEADS // 2
    for h in range(IDX_HEADS):
        src = qia_ref if h < half else qib_ref
        qih_ref[h] = src[:, (h % half) * IDX_DIM:(h % half + 1) * IDX_DIM]
    w_t = kiw_ref[pl.ds(q0, t), :].T[IDX_DIM:IDX_DIM + IDX_HEADS, :] * (IDX_HEADS ** -0.5)

    def score_block(kb, diag):
        kblk = kib_ref[pl.ds(pl.multiple_of(kb * t, t), t), :]
        acc = jnp.zeros((t, t), F32)
        for h in range(IDX_HEADS):
            acc = acc + w_t[h:h + 1, :] * jnp.maximum(_dot_nt(kblk, qih_ref[h]), 0.0)
        if diag:
            acc = jnp.where(causal, acc, -jnp.inf)
        bits = pltpu.bitcast(acc, I32)
        bits = jnp.where(bits == INT_MIN, 0, bits)
        key = bits ^ ((bits >> 31) & INT_MAX)
        sc_ref[kb] = key
        sch_ref[kb] = (key >> HALF_BITS).astype(I16)

    def score_loop(kb, carry):
        score_block(kb, False)
        return carry

    lax.fori_loop(0, i, score_loop, 0)
    score_block(i, True)

    @pl.when(i % 2 == 0)
    def _():
        sc_ref[i + 1] = jnp.full((t, t), INT_MIN, I32)
        sch_ref[i + 1] = jnp.full((t, t), INT_MIN >> HALF_BITS, I16)

    @pl.when(q0 + t > topk)
    def _():
        needs = q0 + qry_pos >= topk
        rows = t // BF16_SUBLANES
        half_lo, half_hi = INT_MIN >> HALF_BITS, -(INT_MIN >> HALF_BITS)

        def halving(_, state):
            lo, hi, cnt_lo, cnt_hi = state
            mid = (lo & hi) + ((lo ^ hi) >> 1)
            mid_c = mid.astype(I16)

            def count(pair, c):
                for kb in (2 * pair, 2 * pair + 1):
                    ge = jnp.where(sch_ref[kb] >= mid_c, jnp.int16(1), jnp.int16(0))
                    tiles = [ge[r * BF16_SUBLANES:(r + 1) * BF16_SUBLANES] for r in range(rows)]
                    while len(tiles) > 1:
                        tiles = [a + b for a, b in zip(tiles[::2], tiles[1::2])]
                    c = c + tiles[0]
                return c

            c = lax.fori_loop(0, (i + 2) // 2, count, jnp.zeros((BF16_SUBLANES, t), I16))
            c = jnp.sum(c.astype(I32), axis=0, keepdims=True)
            fresh = mid > lo
            up = jnp.logical_and(fresh, c >= topk)
            down = jnp.logical_and(fresh, c < topk)
            return (jnp.where(up, mid, lo), jnp.where(down, mid, hi),
                    jnp.where(up, c, cnt_lo), jnp.where(down, c, cnt_hi))

        band, _, cnt_lo, cnt_hi = lax.fori_loop(
            0, HALF_BITS, halving,
            (jnp.full((1, t), half_lo, I32), jnp.full((1, t), half_hi, I32),
             jnp.full((1, t), topk + 1, I32), jnp.zeros((1, t), I32)))

        def low_halves(kb, carry):
            key = sc_ref[kb]
            high = key >> HALF_BITS
            low = (key & (2 * half_hi - 1)) + half_lo
            sch_ref[kb] = jnp.where(high == band, low, jnp.where(high > band, half_hi - 1, half_lo)).astype(I16)
            return carry

        lax.fori_loop(0, i + 1, low_halves, 0)
        low, _, cnt_lo, cnt_hi = lax.fori_loop(
            0, HALF_BITS, halving,
            (jnp.full((1, t), half_lo, I32), jnp.full((1, t), half_hi, I32), cnt_lo, cnt_hi))
        lo = (band << HALF_BITS) + (low - half_lo)
        thr_ref[...] = jnp.where(needs, lo, INT_MIN)

        tied = jnp.where(jnp.logical_and(needs, cnt_lo > topk), 1, 0)

        @pl.when(jnp.max(tied) > 0)
        def _():
            wanted = (topk - cnt_hi).astype(F32)
            tied_f = tied.astype(F32)
            earlier = jnp.where(lax.broadcasted_iota(I32, (t, t), 1) < lax.broadcasted_iota(I32, (t, t), 0),
                                1.0, 0.0).astype(BF16)

            def demote(kb, seen):
                key = sc_ref[kb]
                is_tie = jnp.where(key == lo, tied_f, 0.0)
                rank = seen + jnp.dot(earlier, is_tie.astype(BF16), preferred_element_type=F32)
                drop = is_tie * jnp.where(rank >= wanted, 1.0, 0.0)
                sc_ref[kb] = jnp.where(drop > 0.0, lo - 1, key)
                return seen + jnp.sum(is_tie, axis=0, keepdims=True)

            lax.fori_loop(0, i + 1, demote, jnp.zeros((1, t), F32))

    @pl.when(q0 + t <= topk)
    def _():
        thr_ref[...] = jnp.full((1, t), INT_MIN, I32)

    m_ref[...] = jnp.full(m_ref.shape, NEG, F32)
    acc_ref[...] = jnp.zeros(acc_ref.shape, F32)
    thr = thr_ref[...]

    def attn_block(kb, mode):
        ks = pl.ds(pl.multiple_of(kb * t, t), t)
        mask = jnp.where(sc_ref[kb] >= thr, 0.0, NEG)
        if mode == 0:
            mask = jnp.where(causal, mask, NEG)
        mask_ref[...] = mask

        def logits(h):
            g = h // GROUP
            s_ref[h][...] = _dot_nt(k_ref[ks, g * HEAD_DIM:(g + 1) * HEAD_DIM],
                                    q_ref[:, h * HEAD_DIM:(h + 1) * HEAD_DIM])

        def numerators(h):
            s = s_ref[h][...] + mask_ref[...]
            if mode != "far":
                s = s + bias_ref[h, mode]
            m_old = m_ref[h]
            m_new = jnp.maximum(m_old, jnp.max(s, axis=0, keepdims=True))
            p_ref[h][...] = jnp.exp2(s - m_new).astype(BF16)
            alpha_ref[h] = jnp.exp2(m_old - m_new)
            m_ref[h] = m_new

        def weighted_values(h):
            pv = jnp.dot(vt_ref[kb, h // GROUP], p_ref[h][...], preferred_element_type=F32)
            acc_ref[h] = alpha_ref[h] * acc_ref[h] + pv

        for stage in (logits, numerators, weighted_values):
            for h in range(N_HEADS):
                stage(h)

    def far_loop(kb, carry):
        attn_block(kb, "far")
        return carry

    lax.fori_loop(0, jnp.maximum(i - 1, 0), far_loop, 0)

    @pl.when(i >= 1)
    def _():
        attn_block(i - 1, 1)

    attn_block(i, 0)

    for h in range(N_HEADS):
        acc = acc_ref[h]
        out_t = acc[0:HEAD_DIM, :] / acc[HEAD_DIM:HEAD_DIM + 1, :]
        o_ref[:, h * HEAD_DIM:(h + 1) * HEAD_DIM] = out_t.T.astype(o_ref.dtype)


def _attention(proj, kiw, bias_tiles, batch, seq, q_col, layer, casts):
    t = ATT_BLOCK
    nq = seq // t
    steps = batch * nq
    cast_in, cast_out, cast_shapes = [], [], []
    for w, col_tile in casts:
        rows = w.shape[1] // steps
        assert rows * steps == w.shape[1] and rows % BF16_SUBLANES == 0
        cast_in.append(pl.BlockSpec((None, rows, w.shape[2]), lambda b, i: (layer, b * nq + i, 0)))
        if col_tile is None:
            cast_out.append(pl.BlockSpec((rows, w.shape[2]), lambda b, i: (b * nq + i, 0)))
            cast_shapes.append(jax.ShapeDtypeStruct(w.shape[1:], BF16))
        else:
            n_tiles = w.shape[2] // col_tile
            cast_out.append(pl.BlockSpec((n_tiles, rows, col_tile), lambda b, i: (0, b * nq + i, 0)))
            cast_shapes.append(jax.ShapeDtypeStruct((n_tiles, w.shape[1], col_tile), BF16))
    topk = min(TOPK_MAX, seq // 4)
    attn_dim = N_HEADS * HEAD_DIM
    kv_dim = N_KV_HEADS * HEAD_DIM
    qi_half = IDX_HEADS * IDX_DIM // 2
    k_col = q_col + attn_dim
    qi_col = k_col + 2 * kv_dim
    assert q_col % attn_dim == 0 and k_col % kv_dim == 0 and qi_col % qi_half == 0
    return pl.pallas_call(
        functools.partial(_attn_body, t=t, topk=topk, n_casts=len(casts)),
        grid=(batch, nq),
        in_specs=[
            pl.BlockSpec((t, attn_dim), lambda b, i: (b * nq + i, q_col // attn_dim)),
            pl.BlockSpec((seq, kv_dim), lambda b, i: (b, k_col // kv_dim)),
            pl.BlockSpec((seq, kv_dim), lambda b, i: (b, k_col // kv_dim + 1)),
            pl.BlockSpec((t, qi_half), lambda b, i: (b * nq + i, qi_col // qi_half)),
            pl.BlockSpec((t, qi_half), lambda b, i: (b * nq + i, qi_col // qi_half + 1)),
            pl.BlockSpec((seq, IDX_PAD), lambda b, i: (b, 0)),
            pl.BlockSpec((N_HEADS, 2, t, t), lambda b, i: (0, 0, 0, 0)),
        ] + cast_in,
        out_specs=[pl.BlockSpec((t, attn_dim), lambda b, i: (b * nq + i, 0))] + cast_out,
        out_shape=[jax.ShapeDtypeStruct((batch * seq, attn_dim), BF16)] + cast_shapes,
        scratch_shapes=[
            pltpu.VMEM((nq + 1, t, t), I32),
            pltpu.VMEM((nq + 1, t, t), I16),
            pltpu.VMEM((1, t), I32),
            pltpu.VMEM((nq, N_KV_HEADS, HEAD_DIM + ONES_ROWS, t), BF16),
            pltpu.VMEM((seq, IDX_DIM), BF16),
            pltpu.VMEM((IDX_HEADS, t, IDX_DIM), BF16),
            pltpu.VMEM((t, t), F32),
            pltpu.VMEM((N_HEADS, 1, t), F32),
            pltpu.VMEM((N_HEADS, 1, t), F32),
            pltpu.VMEM((N_HEADS, HEAD_DIM + ONES_ROWS, t), F32),
        ] + [pltpu.VMEM((t, t), F32)] * N_HEADS + [pltpu.VMEM((t, t), BF16)] * N_HEADS,
        compiler_params=_params("parallel", "arbitrary"),
        name="sparse_attention",
    )(proj, proj, proj, proj, proj, kiw, bias_tiles, *[w for w, _ in casts])


def _mixer_body(main_ref, halo_ref, ya_ref, g0_ref, g1_ref, g2_ref, wb_ref, pw_ref, ps_ref, cw_ref,
                o_ref, y_ref, acc_ref, *, tm, tn, seq, pool_dim, conv_dim):
    pos0 = (pl.program_id(0) * tm) % seq
    pos = lax.broadcasted_iota(I32, (tm, 1), 0) + pos0
    keep = jnp.where(pos0 == 0, 0.0, 1.0)
    n_out = o_ref.shape[-1]

    def branch(n, y, c0):
        return g_refs[n][:, c0:c0 + tn].astype(F32) * jnp.dot(y, wb_ref[n, :, c0:c0 + tn], preferred_element_type=F32)

    g_refs = (g0_ref, g1_ref, g2_ref)
    for c0 in range(0, n_out, tn):
        acc_ref[:, c0:c0 + tn] = branch(2, ya_ref[...], c0)

    def ext(c0, c1):
        halo = halo_ref[:, c0:c1].astype(F32) * keep
        return jnp.concatenate([halo, main_ref[:, c0:c1].astype(F32)], axis=0)

    for g, win in enumerate(POOL_WINDOWS):
        c0 = g * POOL_GROUP_DIM
        u = ext(c0, c0 + POOL_GROUP_DIM)
        s, sh = u, 1
        while sh < win:
            s = s + pltpu.roll(s, sh, axis=0)
            sh *= 2
        cnt = jnp.minimum(pos + 1, win).astype(F32)
        dlt = s[POOL_HALO:] / cnt - u[POOL_HALO:]
        yp = jnp.dot(dlt.astype(BF16), pw_ref[g], preferred_element_type=F32)
        y_ref[0, :, c0:c0 + POOL_GROUP_DIM] = (yp * ps_ref[:, c0:c0 + POOL_GROUP_DIM]).astype(BF16)

    z = ext(pool_dim, pool_dim + conv_dim) * ext(pool_dim + conv_dim, pool_dim + 2 * conv_dim)
    yc = z[POOL_HALO:] * cw_ref[CONV_WIDTH - 1:CONV_WIDTH, :]
    for tap in range(1, CONV_WIDTH):
        yc = yc + pltpu.roll(z, tap, axis=0)[POOL_HALO:] * cw_ref[CONV_WIDTH - 1 - tap:CONV_WIDTH - tap, :]
    gate_b = main_ref[:, pool_dim + 2 * conv_dim:pool_dim + 3 * conv_dim].astype(F32)
    y_ref[1] = (gate_b * yc).astype(BF16)

    for c0 in range(0, n_out, tn):
        mixed = acc_ref[:, c0:c0 + tn] + branch(0, y_ref[0], c0) + branch(1, y_ref[1], c0)
        o_ref[:, c0:c0 + tn] = mixed.astype(o_ref.dtype)


def _mixer(main, gates, y_attn, w_branch, pool_w, pool_scale, conv_w, layer, seq, *, tm, tn):
    tok = main.shape[0]
    branch_dim = y_attn.shape[-1]
    width = 4 * branch_dim
    d = w_branch.shape[-1]
    assert d % tn == 0
    gj = 0
    halo_blocks = tm // POOL_HALO
    resident = dict(pipeline_mode=pl.Buffered(1))
    return pl.pallas_call(
        functools.partial(_mixer_body, tm=tm, tn=tn, seq=seq, pool_dim=branch_dim, conv_dim=branch_dim),
        grid=(tok // tm,),
        in_specs=[
            pl.BlockSpec((tm, width), lambda i: (i, 0)),
            pl.BlockSpec((POOL_HALO, width), lambda i: (jnp.maximum(i * halo_blocks - 1, 0), 0)),
            pl.BlockSpec((tm, branch_dim), lambda i: (i, 0)),
            pl.BlockSpec((tm, d), lambda i: (i, gj)),
            pl.BlockSpec((tm, d), lambda i: (i, gj + 1)),
            pl.BlockSpec((tm, d), lambda i: (i, gj + 2)),
            pl.BlockSpec((None, 3, branch_dim, d), lambda i: (0, 0, 0, 0), **resident),
            pl.BlockSpec((None,) + pool_w.shape[1:], lambda i: (layer, 0, 0, 0), **resident),
            pl.BlockSpec((None, 1, branch_dim), lambda i: (layer, 0, 0)),
            pl.BlockSpec((None, CONV_WIDTH, branch_dim), lambda i: (layer, 0, 0)),
        ],
        out_specs=pl.BlockSpec((tm, d), lambda i: (i, 0)),
        out_shape=jax.ShapeDtypeStruct((tok, d), BF16),
        scratch_shapes=[pltpu.VMEM((2, tm, branch_dim), BF16), pltpu.VMEM((tm, d), F32)],
        compiler_params=_params("parallel"),
        name="mixer_branches",
    )(main, main, y_attn, gates, gates, gates, w_branch, pool_w, pool_scale, conv_w)


def _proj_norm_res_body(a_ref, w_ref, x_ref, g_ref, o_ref):
    half = a_ref.shape[0] // 2
    for rows in (slice(0, half), slice(half, 2 * half)):
        m = jnp.dot(a_ref[rows, :], w_ref[...], preferred_element_type=F32)
        o_ref[rows, :] = x_ref[rows, :] + _rms(m, g_ref[...])


def _proj_norm_res(a, w, layer, x, gain, *, tm):
    tok, d = x.shape
    k = a.shape[-1]
    return pl.pallas_call(
        _proj_norm_res_body,
        grid=(tok // tm,),
        in_specs=[
            pl.BlockSpec((tm, k), lambda i: (i, 0)),
            pl.BlockSpec((None, k, d), lambda i: (layer, 0, 0), pipeline_mode=pl.Buffered(1)),
            pl.BlockSpec((tm, d), lambda i: (i, 0)),
            pl.BlockSpec((1, d), lambda i: (0, 0)),
        ],
        out_specs=pl.BlockSpec((tm, d), lambda i: (i, 0)),
        out_shape=jax.ShapeDtypeStruct((tok, d), F32),
        compiler_params=_params("parallel"),
        name="out_proj",
    )(a, w, x, gain)


def _mlp_body(x_hbm, gpre_ref, gpost_ref, wu_ref, wd_ref, o_hbm, h_ref, acc_ref, x_ring, x_sem, o_sem, *, n_tiles):
    i, j = pl.program_id(0), pl.program_id(1)
    tm = h_ref.shape[0]
    chunks = [slice(r, r + NORM_ROWS) for r in range(0, tm, NORM_ROWS)]

    def tile_rows(tile):
        return pl.ds(pl.multiple_of(tile * tm, tm), tm)

    def x_copy(tile):
        return pltpu.make_async_copy(x_hbm.at[tile_rows(tile), :], x_ring.at[tile % 2], x_sem.at[tile % 2])

    def o_copy(tile):
        return pltpu.make_async_copy(x_ring.at[tile % 2], o_hbm.at[tile_rows(tile), :], o_sem.at[tile % 2])

    def finish_tile(tile):
        x_tile = x_ring.at[tile % 2]
        for rows in chunks:
            x_tile[rows, :] = x_tile[rows, :] + _rms(acc_ref[rows, :], gpost_ref[...])

    def start_tile(tile):
        x_tile = x_ring.at[tile % 2]
        for rows in chunks:
            h_ref[rows, :] = _rms(x_tile[rows, :], gpre_ref[...]).astype(BF16)

    def hidden_tile():
        hid = jnp.dot(h_ref[...], wu_ref[...], preferred_element_type=F32)
        hid = jnp.square(jnp.maximum(hid, 0.0)).astype(BF16)
        return jnp.dot(hid, wd_ref[...], preferred_element_type=F32)

    @pl.when(jnp.logical_and(j == 0, i == 0))
    def _():
        x_copy(0).start()

    @pl.when(jnp.logical_and(j == 1, i + 1 < n_tiles))
    def _():
        @pl.when(i > 0)
        def _():
            o_copy(i - 1).wait()

        x_copy(i + 1).start()

    @pl.when(jnp.logical_and(j == 0, i == 0))
    def _():
        x_copy(0).wait()
        start_tile(0)
        acc_ref[...] = hidden_tile()

    @pl.when(jnp.logical_and(j == 0, i > 0))
    def _():
        x_copy(i).wait()
        finish_tile(i - 1)
        start_tile(i)
        acc_ref[...] = hidden_tile()
        o_copy(i - 1).start()

    @pl.when(j > 0)
    def _():
        acc_ref[...] += hidden_tile()

    @pl.when(jnp.logical_and(j == pl.num_programs(1) - 1, i == n_tiles - 1))
    def _():
        if n_tiles > 1:
            o_copy(n_tiles - 2).wait()
        finish_tile(n_tiles - 1)
        o_copy(n_tiles - 1).start()
        o_copy(n_tiles - 1).wait()


def _mlp(x, gpre, gpost, w_up_tiles, w_down, *, tm):
    tok, d = x.shape
    n_hidden_tiles, _, th = w_up_tiles.shape
    n_tiles = tok // tm
    assert tm % NORM_ROWS == 0 and n_hidden_tiles >= 2
    return pl.pallas_call(
        functools.partial(_mlp_body, n_tiles=n_tiles),
        grid=(n_tiles, n_hidden_tiles),
        in_specs=[
            pl.BlockSpec(memory_space=pl.ANY),
            pl.BlockSpec((1, d), lambda i, j: (0, 0)),
            pl.BlockSpec((1, d), lambda i, j: (0, 0)),
            pl.BlockSpec((None, d, th), lambda i, j: (j, 0, 0)),
            pl.BlockSpec((th, d), lambda i, j: (j, 0)),
        ],
        out_specs=pl.BlockSpec(memory_space=pl.ANY),
        out_shape=jax.ShapeDtypeStruct((tok, d), F32),
        scratch_shapes=[
            pltpu.VMEM((tm, d), BF16),
            pltpu.VMEM((tm, d), F32),
            pltpu.VMEM((2, tm, d), F32),
            pltpu.SemaphoreType.DMA((2,)),
            pltpu.SemaphoreType.DMA((2,)),
        ],
        compiler_params=_params("arbitrary", "arbitrary"),
        name="mlp",
    )(x, gpre, gpost, w_up_tiles, w_down)


def _tile(n, pref):
    t = min(n, pref)
    assert n % t == 0, (n, pref)
    return t


@jax.jit
def _forward(x, norm_gains, w_in, pool_w, pool_scale, conv_w, rel_bias, w_branch,
             w_gate, b_gate, w_out, w_up, w_down):
    batch, seq, d = x.shape
    depth = w_in.shape[0]
    tok = batch * seq
    branch_dim = d // 2
    attn_dim = N_HEADS * HEAD_DIM
    kv_dim = N_KV_HEADS * HEAD_DIM
    qi_dim = IDX_HEADS * IDX_DIM
    assert branch_dim == attn_dim == len(POOL_WINDOWS) * POOL_GROUP_DIM
    assert seq % ATT_BLOCK == 0 and ATT_BLOCK >= MAX_DISTANCE

    main_w = 4 * branch_dim
    n_in = main_w + attn_dim + 2 * kv_dim + qi_dim
    kw_w = IDX_DIM + IDX_HEADS
    assert w_in.shape[-1] == n_in + kw_w
    w_in_t = jnp.swapaxes(w_in, 1, 2)
    w_kw_t = jnp.pad(w_in_t[:, n_in:, :], ((0, 0), (0, IDX_PAD - kw_w), (0, 0))).astype(BF16)
    later_weights = ((w_branch.reshape(depth, 3 * branch_dim, d), None), (w_out, None),
                     (w_up, MLP_HIDDEN_TILE), (w_down, None))
    pool_w_b = pool_w.astype(BF16)

    q_col = main_w
    scale = jnp.concatenate([jnp.ones((q_col,), F32), jnp.full((attn_dim,), HEAD_DIM ** -0.5 * LOG2E, F32),
                             jnp.ones((n_in - main_w - attn_dim,), F32)]).reshape(1, n_in)

    bias_tiles = _bias_tiles(rel_bias, ATT_BLOCK)

    tm = _tile(tok, 1024)
    tm_s = _tile(seq, 512)
    xf = x.reshape(tok, d)
    for l in range(depth):
        g = norm_gains[l]
        g0, g1, g2, g3 = (g[n:n + 1] for n in range(4))
        proj, gates, kiw = _projections(xf, g0, w_in_t, w_gate, w_kw_t, l, scale, b_gate[l].reshape(1, 3 * d), n_in,
                                        tm=_tile(tok, 2048), tn=512)
        y_attn, w_branch_b, w_out_b, w_up_b, w_down_b = _attention(proj, kiw, bias_tiles, batch, seq, q_col,
                                                                    l, later_weights)
        mixed = _mixer(proj, gates, y_attn, w_branch_b.reshape(1, 3, branch_dim, d), pool_w_b,
                       pool_scale.reshape(depth, 1, branch_dim), conv_w, l, seq, tm=tm_s, tn=1024)
        xf = _proj_norm_res(mixed, w_out_b[None], 0, xf, g1, tm=tm)
        xf = _mlp(xf, g2, g3, w_up_b, w_down_b, tm=tm)
    return xf.reshape(batch, seq, d)


def kernel(x, norm_gains, w_in, pool_w, pool_scale, conv_w, rel_bias, w_branch, w_gate, b_gate, w_out, w_up, w_down):
    return _forward(x, norm_gains, w_in, pool_w, pool_scale, conv_w, rel_bias, w_branch,
                    w_gate, b_gate, w_out, w_up, w_down)
```

```python
import functools
import math

import jax
import jax.numpy as jnp
import numpy as np
from jax import lax
from jax.experimental import pallas as pl
from jax.experimental.pallas import tpu as pltpu

F32 = jnp.float32
BF16 = jnp.bfloat16
I32 = jnp.int32
I16 = jnp.int16

NORM_EPS = 1e-6
POOL_WINDOWS = (2, 4, 8, 16)
POOL_GROUP_DIM = 256
POOL_HALO = 16
CONV_WIDTH = 3
N_HEADS = 8
N_KV_HEADS = 2
GROUP = N_HEADS // N_KV_HEADS
HEAD_DIM = 128
IDX_HEADS = 16
IDX_DIM = 64
TOPK_MAX = 256
NUM_BUCKETS = 32
MAX_EXACT = NUM_BUCKETS // 2
MAX_DISTANCE = 128
ATT_BLOCK = 256
IDX_PAD = 128
HALF_BITS = 16
INT_MIN = -(2 ** 31)
INT_MAX = 2 ** 31 - 1
NEG = -1e30
LOG2E = math.log2(math.e)
ONES_ROWS = 16
VMEM_LIMIT = 60 * 1024 * 1024
MXU_WIDTH = 256
MLP_HIDDEN_TILE = 1024
BF16_SUBLANES = 16
NORM_ROWS = 256


def _bucket_thresholds():
    n = np.arange(MAX_EXACT, 8 * MAX_DISTANCE, dtype=np.int64)
    nf = n.astype(np.float32)
    large = MAX_EXACT + (np.log(nf / np.float32(MAX_EXACT)) / np.float32(math.log(MAX_DISTANCE / MAX_EXACT))
                         * np.float32(NUM_BUCKETS - MAX_EXACT)).astype(np.int32)
    large = np.minimum(large, NUM_BUCKETS - 1)
    thr = []
    for b in range(MAX_EXACT + 1, NUM_BUCKETS):
        thr.append(int(n[np.argmax(large >= b)]))
    return tuple(thr)


BUCKET_THRESHOLDS = _bucket_thresholds()


def _params(*sem):
    return pltpu.CompilerParams(dimension_semantics=sem, vmem_limit_bytes=VMEM_LIMIT)


def _rms(x, gain):
    ms = jnp.mean(x * x, axis=-1, keepdims=True)
    return x * lax.rsqrt(ms + NORM_EPS) * gain


def _dot_nt(a, b):
    return lax.dot_general(a, b, (((1,), (1,)), ((), ())), preferred_element_type=F32)


def _proj_body(x_hbm, g_ref, wint_ref, wg_ref, wkwt_ref, scale_ref, bias_ref, o_ref, og_ref, okw_ref,
               h_ref, x_buf, x_sem, *, gate_blocks):
    i, j = pl.program_id(0), pl.program_id(1)
    tm = x_buf.shape[0]

    def x_copy(tile):
        rows = pl.ds(pl.multiple_of(tile * tm, tm), tm)
        return pltpu.make_async_copy(x_hbm.at[rows, :], x_buf, x_sem)

    @pl.when(jnp.logical_and(i == 0, j == 0))
    def _():
        x_copy(0).start()

    @pl.when(j == 0)
    def _():
        x_copy(i).wait()

        def norm_rows(r, carry):
            rows = pl.ds(pl.multiple_of(r * NORM_ROWS, NORM_ROWS), NORM_ROWS)
            h = _rms(x_buf[rows, :], g_ref[...]).astype(BF16)
            h_ref[rows, :] = h
            okw_ref[rows, :] = _dot_nt(h, wkwt_ref[...])
            return carry

        lax.fori_loop(0, tm // NORM_ROWS, norm_rows, 0)

    @pl.when(jnp.logical_and(j == 1, i + 1 < pl.num_programs(0)))
    def _():
        x_copy(i + 1).start()

    chunks = [slice(c, c + MXU_WIDTH) for c in range(0, o_ref.shape[-1], MXU_WIDTH)]

    scale = scale_ref[pl.ds(j, 1), :]
    for cs in chunks:
        y = _dot_nt(h_ref[...], wint_ref[cs, :].astype(BF16))
        o_ref[:, cs] = (y * scale[:, cs]).astype(o_ref.dtype)

    @pl.when(j < gate_blocks)
    def _():
        bias = bias_ref[pl.ds(j, 1), :]
        for cs in chunks:
            y = jnp.dot(h_ref[...], wg_ref[:, cs].astype(BF16), preferred_element_type=F32) + bias[:, cs]
            og_ref[:, cs] = (0.5 * jnp.tanh(0.5 * y) + 0.5).astype(og_ref.dtype)


def _projections(x, gain, w_in_t, w_gate, w_kw_t, layer, scale, bias, n_in, *, tm, tn):
    tok, d = x.shape
    n_gate = w_gate.shape[-1]
    assert n_in % tn == 0 and n_gate % tn == 0 and n_gate <= n_in and tm % NORM_ROWS == 0
    gate_blocks = n_gate // tn
    kw = w_kw_t.shape[1]
    gate_block = lambda j: jnp.minimum(j, gate_blocks - 1)
    return pl.pallas_call(
        functools.partial(_proj_body, gate_blocks=gate_blocks),
        grid=(tok // tm, n_in // tn),
        in_specs=[
            pl.BlockSpec(memory_space=pl.ANY),
            pl.BlockSpec((1, d), lambda i, j: (0, 0)),
            pl.BlockSpec((None, tn, d), lambda i, j: (layer, j, 0)),
            pl.BlockSpec((None, d, tn), lambda i, j: (layer, 0, gate_block(j))),
            pl.BlockSpec((None, kw, d), lambda i, j: (layer, 0, 0)),
            pl.BlockSpec((n_in // tn, tn), lambda i, j: (0, 0)),
            pl.BlockSpec((gate_blocks, tn), lambda i, j: (0, 0)),
        ],
        out_specs=[
            pl.BlockSpec((tm, tn), lambda i, j: (i, j)),
            pl.BlockSpec((tm, tn), lambda i, j: (i, gate_block(j))),
            pl.BlockSpec((tm, kw), lambda i, j: (i, 0)),
        ],
        out_shape=[
            jax.ShapeDtypeStruct((tok, n_in), BF16),
            jax.ShapeDtypeStruct((tok, n_gate), BF16),
            jax.ShapeDtypeStruct((tok, kw), F32),
        ],
        scratch_shapes=[pltpu.VMEM((tm, d), BF16), pltpu.VMEM((tm, d), F32), pltpu.SemaphoreType.DMA(())],
        compiler_params=_params("arbitrary", "arbitrary"),
        name="projections",
    )(x, gain, w_in_t, w_gate, w_kw_t, scale.reshape(n_in // tn, tn), bias.reshape(gate_blocks, tn))


def _bias_tiles_body(rel_ref, o_ref, *, t):
    row = lax.broadcasted_iota(I32, (t, t), 0)
    col = lax.broadcasted_iota(I32, (t, t), 1)
    for off in range(2):
        n = jnp.maximum(col - row + off * t, 0)
        large = jnp.full((t, t), MAX_EXACT, I32)
        for thr in BUCKET_THRESHOLDS:
            large = large + jnp.where(n >= thr, 1, 0)
        bucket = jnp.where(n < MAX_EXACT, n, large)
        for h in range(N_HEADS):
            val = jnp.zeros((t, t), F32)
            for b in range(NUM_BUCKETS):
                val = jnp.where(bucket == b, rel_ref[b, h], val)
            o_ref[h, off] = (val - rel_ref[NUM_BUCKETS - 1, h]) * LOG2E


def _bias_tiles(rel_bias, t):
    return pl.pallas_call(
        functools.partial(_bias_tiles_body, t=t),
        in_specs=[pl.BlockSpec(memory_space=pltpu.SMEM)],
        out_specs=pl.BlockSpec(memory_space=pltpu.VMEM),
        out_shape=jax.ShapeDtypeStruct((N_HEADS, 2, t, t), F32),
        compiler_params=pltpu.CompilerParams(vmem_limit_bytes=VMEM_LIMIT),
        name="bias_tiles",
    )(rel_bias)


def _attn_body(*refs, t, topk, n_casts):
    (q_ref, k_ref, v_ref, qia_ref, qib_ref, kiw_ref, bias_ref), refs = refs[:7], refs[7:]
    cast_src, refs = refs[:n_casts], refs[n_casts:]
    o_ref, cast_dst, refs = refs[0], refs[1:1 + n_casts], refs[1 + n_casts:]
    (sc_ref, sch_ref, thr_ref, vt_ref, kib_ref, qih_ref, mask_ref, alpha_ref, m_ref, acc_ref), refs = (
        refs[:10], refs[10:])
    s_ref, p_ref = refs[:N_HEADS], refs[N_HEADS:]

    for src, dst in zip(cast_src, cast_dst):
        if len(dst.shape) == 2:
            dst[...] = src[...].astype(BF16)
        else:
            width = dst.shape[-1]
            for c in range(dst.shape[0]):
                dst[c] = src[:, c * width:(c + 1) * width].astype(BF16)

    i = pl.program_id(1)
    nkb = vt_ref.shape[0]
    q0 = pl.multiple_of(i * t, t)
    key_pos = lax.broadcasted_iota(I32, (t, 1), 0)
    qry_pos = lax.broadcasted_iota(I32, (1, t), 1)
    causal = key_pos <= qry_pos

    @pl.when(i == 0)
    def _():
        for kb in range(nkb):
            rows = slice(kb * t, (kb + 1) * t)
            vb = v_ref[rows, :].astype(F32)
            for g in range(N_KV_HEADS):
                vt_ref[kb, g, 0:HEAD_DIM, :] = vb[:, g * HEAD_DIM:(g + 1) * HEAD_DIM].T.astype(BF16)
                vt_ref[kb, g, HEAD_DIM:, :] = jnp.ones((vt_ref.shape[2] - HEAD_DIM, t), BF16)
            kib_ref[rows, :] = kiw_ref[rows, 0:IDX_DIM].astype(BF16)

    half = IDX_HEADS // 2
    for h in range(IDX_HEADS):
        src = qia_ref if h < half else qib_ref
        qih_ref[h] = src[:, (h % half) * IDX_DIM:(h % half + 1) * IDX_DIM]
    w_t = kiw_ref[pl.ds(q0, t), :].T[IDX_DIM:IDX_DIM + IDX_HEADS, :] * (IDX_HEADS ** -0.5)

    def score_block(kb, diag):
        kblk = kib_ref[pl.ds(pl.multiple_of(kb * t, t), t), :]
        acc = jnp.zeros((t, t), F32)
        for h in range(IDX_HEADS):
            acc = acc + w_t[h:h + 1, :] * jnp.maximum(_dot_nt(kblk, qih_ref[h]), 0.0)
        if diag:
            acc = jnp.where(causal, acc, -jnp.inf)
        bits = pltpu.bitcast(acc, I32)
        bits = jnp.where(bits == INT_MIN, 0, bits)
        key = bits ^ ((bits >> 31) & INT_MAX)
        sc_ref[kb] = key
        sch_ref[kb] = (key >> HALF_BITS).astype(I16)

    def score_loop(kb, carry):
        score_block(kb, False)
        return carry

    lax.fori_loop(0, i, score_loop, 0)
    score_block(i, True)

    @pl.when(i % 2 == 0)
    def _():
        sc_ref[i + 1] = jnp.full((t, t), INT_MIN, I32)
        sch_ref[i + 1] = jnp.full((t, t), INT_MIN >> HALF_BITS, I16)

    @pl.when(q0 + t > topk)
    def _():
        needs = q0 + qry_pos >= topk
        rows = t // BF16_SUBLANES
        half_lo, half_hi = INT_MIN >> HALF_BITS, -(INT_MIN >> HALF_BITS)

        def halving(_, state):
            lo, hi, cnt_lo, cnt_hi = state
            mid = (lo & hi) + ((lo ^ hi) >> 1)
            mid_c = mid.astype(I16)

            def count(pair, c):
                for kb in (2 * pair, 2 * pair + 1):
                    ge = jnp.where(sch_ref[kb] >= mid_c, jnp.int16(1), jnp.int16(0))
                    tiles = [ge[r * BF16_SUBLANES:(r + 1) * BF16_SUBLANES] for r in range(rows)]
                    while len(tiles) > 1:
                        tiles = [a + b for a, b in zip(tiles[::2], tiles[1::2])]
                    c = c + tiles[0]
                return c

            c = lax.fori_loop(0, (i + 2) // 2, count, jnp.zeros((BF16_SUBLANES, t), I16))
            c = jnp.sum(c.astype(I32), axis=0, keepdims=True)
            fresh = mid > lo
            up = jnp.logical_and(fresh, c >= topk)
            down = jnp.logical_and(fresh, c < topk)
            return (jnp.where(up, mid, lo), jnp.where(down, mid, hi),
                    jnp.where(up, c, cnt_lo), jnp.where(down, c, cnt_hi))

        band, _, cnt_lo, cnt_hi = lax.fori_loop(
            0, HALF_BITS, halving,
            (jnp.full((1, t), half_lo, I32), jnp.full((1, t), half_hi, I32),
             jnp.full((1, t), topk + 1, I32), jnp.zeros((1, t), I32)))

        def low_halves(kb, carry):
            key = sc_ref[kb]
            high = key >> HALF_BITS
            low = (key & (2 * half_hi - 1)) + half_lo
            sch_ref[kb] = jnp.where(high == band, low, jnp.where(high > band, half_hi - 1, half_lo)).astype(I16)
            return carry

        lax.fori_loop(0, i + 1, low_halves, 0)
        low, _, cnt_lo, cnt_hi = lax.fori_loop(
            0, HALF_BITS, halving,
            (jnp.full((1, t), half_lo, I32), jnp.full((1, t), half_hi, I32), cnt_lo, cnt_hi))
        lo = (band << HALF_BITS) + (low - half_lo)
        thr_ref[...] = jnp.where(needs, lo, INT_MIN)

        tied = jnp.where(jnp.logical_and(needs, cnt_lo > topk), 1, 0)

        @pl.when(jnp.max(tied) > 0)
        def _():
            wanted = (topk - cnt_hi).astype(F32)
            tied_f = tied.astype(F32)
            earlier = jnp.where(lax.broadcasted_iota(I32, (t, t), 1) < lax.broadcasted_iota(I32, (t, t), 0),
                                1.0, 0.0).astype(BF16)

            def demote(kb, seen):
                key = sc_ref[kb]
                is_tie = jnp.where(key == lo, tied_f, 0.0)
                rank = seen + jnp.dot(earlier, is_tie.astype(BF16), preferred_element_type=F32)
                drop = is_tie * jnp.where(rank >= wanted, 1.0, 0.0)
                sc_ref[kb] = jnp.where(drop > 0.0, lo - 1, key)
                return seen + jnp.sum(is_tie, axis=0, keepdims=True)

            lax.fori_loop(0, i + 1, demote, jnp.zeros((1, t), F32))

    @pl.when(q0 + t <= topk)
    def _():
        thr_ref[...] = jnp.full((1, t), INT_MIN, I32)

    m_ref[...] = jnp.full(m_ref.shape, NEG, F32)
    acc_ref[...] = jnp.zeros(acc_ref.shape, F32)
    thr = thr_ref[...]

    def attn_block(kb, mode):
        ks = pl.ds(pl.multiple_of(kb * t, t), t)
        mask = jnp.where(sc_ref[kb] >= thr, 0.0, NEG)
        if mode == 0:
            mask = jnp.where(causal, mask, NEG)
        mask_ref[...] = mask

        def logits(h):
            g = h // GROUP
            s_ref[h][...] = _dot_nt(k_ref[ks, g * HEAD_DIM:(g + 1) * HEAD_DIM],
                                    q_ref[:, h * HEAD_DIM:(h + 1) * HEAD_DIM])

        def numerators(h):
            s = s_ref[h][...] + mask_ref[...]
            if mode != "far":
                s = s + bias_ref[h, mode]
            m_old = m_ref[h]
            m_new = jnp.maximum(m_old, jnp.max(s, axis=0, keepdims=True))
            p_ref[h][...] = jnp.exp2(s - m_new).astype(BF16)
            alpha_ref[h] = jnp.exp2(m_old - m_new)
            m_ref[h] = m_new

        def weighted_values(h):
            pv = jnp.dot(vt_ref[kb, h // GROUP], p_ref[h][...], preferred_element_type=F32)
            acc_ref[h] = alpha_ref[h] * acc_ref[h] + pv

        for stage in (logits, numerators, weighted_values):
            for h in range(N_HEADS):
                stage(h)

    def far_loop(kb, carry):
        attn_block(kb, "far")
        return carry

    lax.fori_loop(0, jnp.maximum(i - 1, 0), far_loop, 0)

    @pl.when(i >= 1)
    def _():
        attn_block(i - 1, 1)

    attn_block(i, 0)

    for h in range(N_HEADS):
        acc = acc_ref[h]
        out_t = acc[0:HEAD_DIM, :] / acc[HEAD_DIM:HEAD_DIM + 1, :]
        o_ref[:, h * HEAD_DIM:(h + 1) * HEAD_DIM] = out_t.T.astype(o_ref.dtype)


def _attention(proj, kiw, bias_tiles, batch, seq, q_col, layer, casts):
    t = ATT_BLOCK
    nq = seq // t
    steps = batch * nq
    cast_in, cast_out, cast_shapes = [], [], []
    for w, col_tile in casts:
        rows = w.shape[1] // steps
        assert rows * steps == w.shape[1] and rows % BF16_SUBLANES == 0
        cast_in.append(pl.BlockSpec((None, rows, w.shape[2]), lambda b, i: (layer, b * nq + i, 0)))
        if col_tile is None:
            cast_out.append(pl.BlockSpec((rows, w.shape[2]), lambda b, i: (b * nq + i, 0)))
            cast_shapes.append(jax.ShapeDtypeStruct(w.shape[1:], BF16))
        else:
            n_tiles = w.shape[2] // col_tile
            cast_out.append(pl.BlockSpec((n_tiles, rows, col_tile), lambda b, i: (0, b * nq + i, 0)))
            cast_shapes.append(jax.ShapeDtypeStruct((n_tiles, w.shape[1], col_tile), BF16))
    topk = min(TOPK_MAX, seq // 4)
    attn_dim = N_HEADS * HEAD_DIM
    kv_dim = N_KV_HEADS * HEAD_DIM
    qi_half = IDX_HEADS * IDX_DIM // 2
    k_col = q_col + attn_dim
    qi_col = k_col + 2 * kv_dim
    assert q_col % attn_dim == 0 and k_col % kv_dim == 0 and qi_col % qi_half == 0
    return pl.pallas_call(
        functools.partial(_attn_body, t=t, topk=topk, n_casts=len(casts)),
        grid=(batch, nq),
        in_specs=[
            pl.BlockSpec((t, attn_dim), lambda b, i: (b * nq + i, q_col // attn_dim)),
            pl.BlockSpec((seq, kv_dim), lambda b, i: (b, k_col // kv_dim)),
            pl.BlockSpec((seq, kv_dim), lambda b, i: (b, k_col // kv_dim + 1)),
            pl.BlockSpec((t, qi_half), lambda b, i: (b * nq + i, qi_col // qi_half)),
            pl.BlockSpec((t, qi_half), lambda b, i: (b * nq + i, qi_col // qi_half + 1)),
            pl.BlockSpec((seq, IDX_PAD), lambda b, i: (b, 0)),
            pl.BlockSpec((N_HEADS, 2, t, t), lambda b, i: (0, 0, 0, 0)),
        ] + cast_in,
        out_specs=[pl.BlockSpec((t, attn_dim), lambda b, i: (b * nq + i, 0))] + cast_out,
        out_shape=[jax.ShapeDtypeStruct((batch * seq, attn_dim), BF16)] + cast_shapes,
        scratch_shapes=[
            pltpu.VMEM((nq + 1, t, t), I32),
            pltpu.VMEM((nq + 1, t, t), I16),
            pltpu.VMEM((1, t), I32),
            pltpu.VMEM((nq, N_KV_HEADS, HEAD_DIM + ONES_ROWS, t), BF16),
            pltpu.VMEM((seq, IDX_DIM), BF16),
            pltpu.VMEM((IDX_HEADS, t, IDX_DIM), BF16),
            pltpu.VMEM((t, t), F32),
            pltpu.VMEM((N_HEADS, 1, t), F32),
            pltpu.VMEM((N_HEADS, 1, t), F32),
            pltpu.VMEM((N_HEADS, HEAD_DIM + ONES_ROWS, t), F32),
        ] + [pltpu.VMEM((t, t), F32)] * N_HEADS + [pltpu.VMEM((t, t), BF16)] * N_HEADS,
        compiler_params=_params("parallel", "arbitrary"),
        name="sparse_attention",
    )(proj, proj, proj, proj, proj, kiw, bias_tiles, *[w for w, _ in casts])


def _mixer_body(main_ref, halo_ref, ya_ref, g0_ref, g1_ref, g2_ref, wb_ref, pw_ref, ps_ref, cw_ref,
                o_ref, y_ref, acc_ref, *, tm, tn, seq, pool_dim, conv_dim):
    pos0 = (pl.program_id(0) * tm) % seq
    pos = lax.broadcasted_iota(I32, (tm, 1), 0) + pos0
    keep = jnp.where(pos0 == 0, 0.0, 1.0)
    n_out = o_ref.shape[-1]

    def branch(n, y, c0):
        return g_refs[n][:, c0:c0 + tn].astype(F32) * jnp.dot(y, wb_ref[n, :, c0:c0 + tn], preferred_element_type=F32)

    g_refs = (g0_ref, g1_ref, g2_ref)
    for c0 in range(0, n_out, tn):
        acc_ref[:, c0:c0 + tn] = branch(2, ya_ref[...], c0)

    def ext(c0, c1):
        halo = halo_ref[:, c0:c1].astype(F32) * keep
        return jnp.concatenate([halo, main_ref[:, c0:c1].astype(F32)], axis=0)

    for g, win in enumerate(POOL_WINDOWS):
        c0 = g * POOL_GROUP_DIM
        u = ext(c0, c0 + POOL_GROUP_DIM)
        s, sh = u, 1
        while sh < win:
            s = s + pltpu.roll(s, sh, axis=0)
            sh *= 2
        cnt = jnp.minimum(pos + 1, win).astype(F32)
        dlt = s[POOL_HALO:] / cnt - u[POOL_HALO:]
        yp = jnp.dot(dlt.astype(BF16), pw_ref[g], preferred_element_type=F32)
        y_ref[0, :, c0:c0 + POOL_GROUP_DIM] = (yp * ps_ref[:, c0:c0 + POOL_GROUP_DIM]).astype(BF16)

    z = ext(pool_dim, pool_dim + conv_dim) * ext(pool_dim + conv_dim, pool_dim + 2 * conv_dim)
    yc = z[POOL_HALO:] * cw_ref[CONV_WIDTH - 1:CONV_WIDTH, :]
    for tap in range(1, CONV_WIDTH):
        yc = yc + pltpu.roll(z, tap, axis=0)[POOL_HALO:] * cw_ref[CONV_WIDTH - 1 - tap:CONV_WIDTH - tap, :]
    gate_b = main_ref[:, pool_dim + 2 * conv_dim:pool_dim + 3 * conv_dim].astype(F32)
    y_ref[1] = (gate_b * yc).astype(BF16)

    for c0 in range(0, n_out, tn):
        mixed = acc_ref[:, c0:c0 + tn] + branch(0, y_ref[0], c0) + branch(1, y_ref[1], c0)
        o_ref[:, c0:c0 + tn] = mixed.astype(o_ref.dtype)


def _mixer(main, gates, y_attn, w_branch, pool_w, pool_scale, conv_w, layer, seq, *, tm, tn):
    tok = main.shape[0]
    branch_dim = y_attn.shape[-1]
    width = 4 * branch_dim
    d = w_branch.shape[-1]
    assert d % tn == 0
    gj = 0
    halo_blocks = tm // POOL_HALO
    resident = dict(pipeline_mode=pl.Buffered(1))
    return pl.pallas_call(
        functools.partial(_mixer_body, tm=tm, tn=tn, seq=seq, pool_dim=branch_dim, conv_dim=branch_dim),
        grid=(tok // tm,),
        in_specs=[
            pl.BlockSpec((tm, width), lambda i: (i, 0)),
            pl.BlockSpec((POOL_HALO, width), lambda i: (jnp.maximum(i * halo_blocks - 1, 0), 0)),
            pl.BlockSpec((tm, branch_dim), lambda i: (i, 0)),
            pl.BlockSpec((tm, d), lambda i: (i, gj)),
            pl.BlockSpec((tm, d), lambda i: (i, gj + 1)),
            pl.BlockSpec((tm, d), lambda i: (i, gj + 2)),
            pl.BlockSpec((None, 3, branch_dim, d), lambda i: (0, 0, 0, 0), **resident),
            pl.BlockSpec((None,) + pool_w.shape[1:], lambda i: (layer, 0, 0, 0), **resident),
            pl.BlockSpec((None, 1, branch_dim), lambda i: (layer, 0, 0)),
            pl.BlockSpec((None, CONV_WIDTH, branch_dim), lambda i: (layer, 0, 0)),
        ],
        out_specs=pl.BlockSpec((tm, d), lambda i: (i, 0)),
        out_shape=jax.ShapeDtypeStruct((tok, d), BF16),
        scratch_shapes=[pltpu.VMEM((2, tm, branch_dim), BF16), pltpu.VMEM((tm, d), F32)],
        compiler_params=_params("parallel"),
        name="mixer_branches",
    )(main, main, y_attn, gates, gates, gates, w_branch, pool_w, pool_scale, conv_w)


def _proj_norm_res_body(a_ref, w_ref, x_ref, g_ref, o_ref):
    half = a_ref.shape[0] // 2
    for rows in (slice(0, half), slice(half, 2 * half)):
        m = jnp.dot(a_ref[rows, :], w_ref[...], preferred_element_type=F32)
        o_ref[rows, :] = x_ref[rows, :] + _rms(m, g_ref[...])


def _proj_norm_res(a, w, layer, x, gain, *, tm):
    tok, d = x.shape
    k = a.shape[-1]
    return pl.pallas_call(
        _proj_norm_res_body,
        grid=(tok // tm,),
        in_specs=[
            pl.BlockSpec((tm, k), lambda i: (i, 0)),
            pl.BlockSpec((None, k, d), lambda i: (layer, 0, 0), pipeline_mode=pl.Buffered(1)),
            pl.BlockSpec((tm, d), lambda i: (i, 0)),
            pl.BlockSpec((1, d), lambda i: (0, 0)),
        ],
        out_specs=pl.BlockSpec((tm, d), lambda i: (i, 0)),
        out_shape=jax.ShapeDtypeStruct((tok, d), F32),
        compiler_params=_params("parallel"),
        name="out_proj",
    )(a, w, x, gain)


def _mlp_body(x_hbm, gpre_ref, gpost_ref, wu_ref, wd_ref, o_hbm, h_ref, acc_ref, x_ring, x_sem, o_sem, *, n_tiles):
    i, j = pl.program_id(0), pl.program_id(1)
    tm = h_ref.shape[0]
    chunks = [slice(r, r + NORM_ROWS) for r in range(0, tm, NORM_ROWS)]

    def tile_rows(tile):
        return pl.ds(pl.multiple_of(tile * tm, tm), tm)

    def x_copy(tile):
        return pltpu.make_async_copy(x_hbm.at[tile_rows(tile), :], x_ring.at[tile % 2], x_sem.at[tile % 2])

    def o_copy(tile):
        return pltpu.make_async_copy(x_ring.at[tile % 2], o_hbm.at[tile_rows(tile), :], o_sem.at[tile % 2])

    def finish_tile(tile):
        x_tile = x_ring.at[tile % 2]
        for rows in chunks:
            x_tile[rows, :] = x_tile[rows, :] + _rms(acc_ref[rows, :], gpost_ref[...])

    def start_tile(tile):
        x_tile = x_ring.at[tile % 2]
        for rows in chunks:
            h_ref[rows, :] = _rms(x_tile[rows, :], gpre_ref[...]).astype(BF16)

    def hidden_tile():
        hid = jnp.dot(h_ref[...], wu_ref[...], preferred_element_type=F32)
        hid = jnp.square(jnp.maximum(hid, 0.0)).astype(BF16)
        return jnp.dot(hid, wd_ref[...], preferred_element_type=F32)

    @pl.when(jnp.logical_and(j == 0, i == 0))
    def _():
        x_copy(0).start()

    @pl.when(jnp.logical_and(j == 1, i + 1 < n_tiles))
    def _():
        @pl.when(i > 0)
        def _():
            o_copy(i - 1).wait()

        x_copy(i + 1).start()

    @pl.when(jnp.logical_and(j == 0, i == 0))
    def _():
        x_copy(0).wait()
        start_tile(0)
        acc_ref[...] = hidden_tile()

    @pl.when(jnp.logical_and(j == 0, i > 0))
    def _():
        x_copy(i).wait()
        finish_tile(i - 1)
        start_tile(i)
        acc_ref[...] = hidden_tile()
        o_copy(i - 1).start()

    @pl.when(j > 0)
    def _():
        acc_ref[...] += hidden_tile()

    @pl.when(jnp.logical_and(j == pl.num_programs(1) - 1, i == n_tiles - 1))
    def _():
        if n_tiles > 1:
            o_copy(n_tiles - 2).wait()
        finish_tile(n_tiles - 1)
        o_copy(n_tiles - 1).start()
        o_copy(n_tiles - 1).wait()


def _mlp(x, gpre, gpost, w_up_tiles, w_down, *, tm):
    tok, d = x.shape
    n_hidden_tiles, _, th = w_up_tiles.shape
    n_tiles = tok // tm
    assert tm % NORM_ROWS == 0 and n_hidden_tiles >= 2
    return pl.pallas_call(
        functools.partial(_mlp_body, n_tiles=n_tiles),
        grid=(n_tiles, n_hidden_tiles),
        in_specs=[
            pl.BlockSpec(memory_space=pl.ANY),
            pl.BlockSpec((1, d), lambda i, j: (0, 0)),
            pl.BlockSpec((1, d), lambda i, j: (0, 0)),
            pl.BlockSpec((None, d, th), lambda i, j: (j, 0, 0)),
            pl.BlockSpec((th, d), lambda i, j: (j, 0)),
        ],
        out_specs=pl.BlockSpec(memory_space=pl.ANY),
        out_shape=jax.ShapeDtypeStruct((tok, d), F32),
        scratch_shapes=[
            pltpu.VMEM((tm, d), BF16),
            pltpu.VMEM((tm, d), F32),
            pltpu.VMEM((2, tm, d), F32),
            pltpu.SemaphoreType.DMA((2,)),
            pltpu.SemaphoreType.DMA((2,)),
        ],
        compiler_params=_params("arbitrary", "arbitrary"),
        name="mlp",
    )(x, gpre, gpost, w_up_tiles, w_down)


def _tile(n, pref):
    t = min(n, pref)
    assert n % t == 0, (n, pref)
    return t


@jax.jit
def _forward(x, norm_gains, w_in, pool_w, pool_scale, conv_w, rel_bias, w_branch,
             w_gate, b_gate, w_out, w_up, w_down):
    batch, seq, d = x.shape
    depth = w_in.shape[0]
    tok = batch * seq
    branch_dim = d // 2
    attn_dim = N_HEADS * HEAD_DIM
    kv_dim = N_KV_HEADS * HEAD_DIM
    qi_dim = IDX_HEADS * IDX_DIM
    assert branch_dim == attn_dim == len(POOL_WINDOWS) * POOL_GROUP_DIM
    assert seq % ATT_BLOCK == 0 and ATT_BLOCK >= MAX_DISTANCE

    main_w = 4 * branch_dim
    n_in = main_w + attn_dim + 2 * kv_dim + qi_dim
    kw_w = IDX_DIM + IDX_HEADS
    assert w_in.shape[-1] == n_in + kw_w
    w_in_t = jnp.swapaxes(w_in, 1, 2)
    w_kw_t = jnp.pad(w_in_t[:, n_in:, :], ((0, 0), (0, IDX_PAD - kw_w), (0, 0))).astype(BF16)
    later_weights = ((w_branch.reshape(depth, 3 * branch_dim, d), None), (w_out, None),
                     (w_up, MLP_HIDDEN_TILE), (w_down, None))
    pool_w_b = pool_w.astype(BF16)

    q_col = main_w
    scale = jnp.concatenate([jnp.ones((q_col,), F32), jnp.full((attn_dim,), HEAD_DIM ** -0.5 * LOG2E, F32),
                             jnp.ones((n_in - main_w - attn_dim,), F32)]).reshape(1, n_in)

    bias_tiles = _bias_tiles(rel_bias, ATT_BLOCK)

    tm = _tile(tok, 1024)
    tm_s = _tile(seq, 512)
    xf = x.reshape(tok, d)
    for l in range(depth):
        g = norm_gains[l]
        g0, g1, g2, g3 = (g[n:n + 1] for n in range(4))
        proj, gates, kiw = _projections(xf, g0, w_in_t, w_gate, w_kw_t, l, scale, b_gate[l].reshape(1, 3 * d), n_in,
                                        tm=_tile(tok, 2048), tn=512)
        y_attn, w_branch_b, w_out_b, w_up_b, w_down_b = _attention(proj, kiw, bias_tiles, batch, seq, q_col,
                                                                    l, later_weights)
        mixed = _mixer(proj, gates, y_attn, w_branch_b.reshape(1, 3, branch_dim, d), pool_w_b,
                       pool_scale.reshape(depth, 1, branch_dim), conv_w, l, seq, tm=tm_s, tn=1024)
        xf = _proj_norm_res(mixed, w_out_b[None], 0, xf, g1, tm=tm)
        xf = _mlp(xf, g2, g3, w_up_b, w_down_b, tm=tm)
    return xf.reshape(batch, seq, d)


def kernel(x, norm_gains, w_in, pool_w, pool_scale, conv_w, rel_bias, w_branch, w_gate, b_gate, w_out, w_up, w_down):
    return _forward(x, norm_gains, w_in, pool_w, pool_scale, conv_w, rel_bias, w_branch,
                    w_gate, b_gate, w_out, w_up, w_down)
```
